```python
import jax, jax.numpy as jnp
from jax import lax
import numpy as np

D_MODEL = 4096
BATCH = 2
SEQ = 8192
DEPTH = 2
DEC_BATCH = 16
DEC_SEQ = 64
PAST_LEN = 1024

CHUNK = 64
N_A = DEPTH // 2
N_B = DEPTH - N_A
GLA_HEADS = 12
GLA_DK = 128
GLA_DV = 256
GLA_RANK = 16
GLA_TAU = 16.0
FOX_HEADS = 24
FOX_DH = 128
Q_BLOCK = 128
N_MEM = 256
MEM_HEADS = 4
MEM_DH = 256
D_FF = 11008
CONV_W = 3
EPS = 1e-6

A_Q = GLA_HEADS * GLA_DK
A_V = GLA_HEADS * GLA_DV
M_Q = MEM_HEADS * MEM_DH
B_Q = FOX_HEADS * FOX_DH
A_IN = 2 * A_Q + 2 * A_V + GLA_RANK + M_Q
B_IN = B_Q + M_Q
MIX = A_V + M_Q
SPLIT_A = [A_Q, 2 * A_Q, 2 * A_Q + A_V, 2 * A_Q + 2 * A_V, 2 * A_Q + 2 * A_V + GLA_RANK]
KV_OUT = 2 * B_Q + FOX_HEADS

kernel_name = 'yoco_gla_fox_streaming_step'

F32 = jnp.float32


def rmsnorm(x, g):
    xf = x.astype(F32)
    y = xf * lax.rsqrt(jnp.mean(xf * xf, axis=-1, keepdims=True) + EPS)
    return (y * g.astype(F32)).astype(x.dtype)


def gla_chunk(S, q, k, v, la):
    q, k, v = q.astype(F32), k.astype(F32), v.astype(F32)
    b = jnp.cumsum(la, axis=1)
    C = q.shape[1]
    causal = jnp.tril(jnp.ones((C, C), dtype=bool))[None, :, :, None, None]
    decay = jnp.exp(jnp.where(causal, b[:, :, None] - b[:, None, :], -jnp.inf))
    att = jnp.einsum('bthd,bshd,btshd->bhts', q, k, decay)
    o = jnp.einsum('bhts,bshv->bthv', att, v) + jnp.einsum('bthd,bhdv->bthv', q * jnp.exp(b), S)
    b_end = b[:, -1]
    S_new = jnp.exp(b_end)[..., None] * S + jnp.einsum('bshd,bshv->bhdv', k * jnp.exp(b_end[:, None] - b), v)
    return o, S_new


def gla_recurrence(q, k, v, la, S0):
    B, T = q.shape[:2]
    cl = min(T, CHUNK)
    nc = T // cl

    def blocks(t):
        return jnp.moveaxis(t.reshape(B, nc, cl, *t.shape[2:]), 1, 0)

    def step(S, xs):
        o, S = gla_chunk(S, *xs)
        return S, o

    S, o = lax.scan(step, S0.astype(F32), (blocks(q), blocks(k), blocks(v), blocks(la)))
    return jnp.moveaxis(o, 0, 1).reshape(B, T, GLA_HEADS, GLA_DV), S


def mem_kv(mem, g, w):
    kv = rmsnorm(mem, g) @ w
    return kv[..., :M_Q], kv[..., M_Q:]


def mem_attend(qm, mk, mv):
    B, T, _ = qm.shape
    N = mk.shape[1]
    q = qm.reshape(B, T, MEM_HEADS, MEM_DH) * (MEM_DH ** -0.5)
    k = mk.reshape(B, N, MEM_HEADS, MEM_DH)
    v = mv.reshape(B, N, MEM_HEADS, MEM_DH)
    p = jax.nn.softmax(jnp.einsum('bthd,bnhd->bhtn', q, k).astype(F32), axis=-1).astype(v.dtype)
    return jnp.einsum('bhtn,bnhd->bthd', p, v).reshape(B, T, M_Q)


def gla_branch(xn, w_in, w_gate, b_gate, g_head, mk, mv, S0):
    B, T, _ = xn.shape
    q, k, v, r, lr, qm = jnp.split(xn @ w_in, SPLIT_A, axis=-1)
    la = jax.nn.log_sigmoid((lr @ w_gate + b_gate).astype(F32)) / GLA_TAU
    q = q.reshape(B, T, GLA_HEADS, GLA_DK) * (GLA_DK ** -0.5)
    k = k.reshape(B, T, GLA_HEADS, GLA_DK)
    v = v.reshape(B, T, GLA_HEADS, GLA_DV)
    la = la.reshape(B, T, GLA_HEADS, GLA_DK)
    o, S = gla_recurrence(q, k, v, la, S0)
    o = rmsnorm(o, g_head).reshape(B, T, A_V).astype(xn.dtype) * jax.nn.silu(r)
    return jnp.concatenate([o, mem_attend(qm, mk, mv)], axis=-1), S


def shared_kv(x, g_kv, w_kv, b_forget):
    B, T, _ = x.shape
    kvf = rmsnorm(x, g_kv) @ w_kv
    k, v, fl = jnp.split(kvf, [B_Q, 2 * B_Q], axis=-1)
    logf = jax.nn.log_sigmoid((fl + b_forget).astype(F32))
    return k.reshape(B, T, FOX_HEADS, FOX_DH), v.reshape(B, T, FOX_HEADS, FOX_DH), logf


def fox_attend(q, c_q, pos_q, k, v, c_k, pos_k):
    B, T = q.shape[:2]
    qb = min(T, Q_BLOCK)
    nb = T // qb
    ck = jnp.swapaxes(c_k, 1, 2)

    def block(args):
        qi, cqi, pi = args
        s = jnp.einsum('bqhd,bkhd->bhqk', qi, k).astype(F32)
        s = s + jnp.swapaxes(cqi, 1, 2)[..., None] - ck[:, :, None, :]
        s = jnp.where(pos_k[None, :] <= pi[:, None], s, -jnp.inf)
        p = jax.nn.softmax(s, axis=-1).astype(v.dtype)
        return jnp.einsum('bhqk,bkhd->bqhd', p, v)

    xs = (jnp.moveaxis(q.reshape(B, nb, qb, FOX_HEADS, FOX_DH), 1, 0),
          jnp.moveaxis(c_q.reshape(B, nb, qb, FOX_HEADS), 1, 0),
          pos_q.reshape(nb, qb))
    o = lax.map(block, xs)
    return jnp.moveaxis(o, 0, 1).reshape(B, T, FOX_HEADS, FOX_DH)


def fox_branch(xn, w_in, mk, mv, k, v, c_k, c_q, pos_q, pos_k):
    B, T, _ = xn.shape
    q, qm = jnp.split(xn @ w_in, [B_Q], axis=-1)
    q = q.reshape(B, T, FOX_HEADS, FOX_DH) * (FOX_DH ** -0.5)
    o = fox_attend(q, c_q, pos_q, k, v, c_k, pos_k).reshape(B, T, B_Q)
    return jnp.concatenate([o, mem_attend(qm, mk, mv)], axis=-1)


def conv_ffn(hn, w_up, w_conv, b_conv, w_down, prev):
    T = hn.shape[1]
    u = hn @ w_up
    up = jnp.concatenate([prev.astype(u.dtype), u], axis=1)
    z = b_conv + sum(w_conv[i] * up[:, i:i + T] for i in range(CONV_W))
    a, g = jnp.split(z, 2, axis=-1)
    return (jax.nn.silu(g) * a) @ w_down, up[:, -(CONV_W - 1):]


def setup_inputs(seed: int = 0) -> dict:
    key = jax.random.key(seed)
    ks = jax.random.split(key, 32)
    D = D_MODEL

    def nrm(k, shape, scale):
        return jax.random.normal(k, shape, F32) * scale

    return {
        'x_prompt': nrm(ks[0], (BATCH, SEQ, D), 1.0),
        'x_sample': nrm(ks[1], (DEC_BATCH, DEC_SEQ, D), 1.0),
        'state_gla': nrm(ks[2], (N_A, DEC_BATCH, GLA_HEADS, GLA_DK, GLA_DV), 0.5),
        'cache_fox_k': nrm(ks[3], (DEC_BATCH, PAST_LEN, FOX_HEADS, FOX_DH), 1.0),
        'cache_fox_v': nrm(ks[4], (DEC_BATCH, PAST_LEN, FOX_HEADS, FOX_DH), 1.0),
        'cache_fox_logf': jax.nn.log_sigmoid(3.0 + nrm(ks[5], (DEC_BATCH, PAST_LEN, FOX_HEADS), 1.0)),
        'cache_mem_k': nrm(ks[6], (DEPTH, DEC_BATCH, N_MEM, M_Q), 1.0),
        'cache_mem_v': nrm(ks[7], (DEPTH, DEC_BATCH, N_MEM, M_Q), 1.0),
        'state_ffn_conv': nrm(ks[8], (DEPTH, DEC_BATCH, CONV_W - 1, 2 * D_FF), 1.0),
        'mem_prompt': nrm(ks[9], (BATCH, N_MEM, D), 1.0),
        'g_attn': 1.0 + nrm(ks[10], (DEPTH, D), 0.02),
        'g_ffn': 1.0 + nrm(ks[11], (DEPTH, D), 0.02),
        'g_mem': 1.0 + nrm(ks[12], (DEPTH, D), 0.02),
        'w_mem_kv': nrm(ks[13], (DEPTH, D, 2 * M_Q), D ** -0.5),
        'w_in_a': nrm(ks[14], (N_A, D, A_IN), D ** -0.5),
        'w_gate_a': nrm(ks[15], (N_A, GLA_RANK, A_Q), GLA_RANK ** -0.5),
        'b_gate_a': nrm(ks[16], (N_A, A_Q), 0.1),
        'g_gla': 1.0 + nrm(ks[17], (N_A, GLA_DV), 0.02),
        'w_in_b': nrm(ks[18], (N_B, D, B_IN), D ** -0.5),
        'g_kv': 1.0 + nrm(ks[19], (D,), 0.02),
        'w_kv': nrm(ks[20], (D, KV_OUT), D ** -0.5),
        'b_forget': 3.0 + nrm(ks[21], (FOX_HEADS,), 0.1),
        'w_out': nrm(ks[22], (DEPTH, MIX, D), MIX ** -0.5),
        'w_up': nrm(ks[23], (DEPTH, D, 2 * D_FF), D ** -0.5),
        'w_conv': nrm(ks[24], (DEPTH, CONV_W, 2 * D_FF), CONV_W ** -0.5),
        'b_conv': nrm(ks[25], (DEPTH, 2 * D_FF), 0.02),
        'w_down': nrm(ks[26], (DEPTH, D_FF, D), D_FF ** -0.5),
        'g_final': 1.0 + nrm(ks[27], (D,), 0.02),
    }


def reference(x_prompt, x_sample, state_gla, cache_fox_k, cache_fox_v, cache_fox_logf,
              cache_mem_k, cache_mem_v, state_ffn_conv, mem_prompt,
              g_attn, g_ffn, g_mem, w_mem_kv, w_in_a, w_gate_a, b_gate_a, g_gla,
              w_in_b, g_kv, w_kv, b_forget, w_out, w_up, w_conv, b_conv, w_down, g_final):
    xp, xs = x_prompt, x_sample
    Bp, Tp = xp.shape[:2]
    Ts = xs.shape[1]
    P = cache_fox_k.shape[1]
    gla_p, gla_s, mk_list, mv_list, conv_p, conv_s = [], [], [], [], [], []
    for l in range(DEPTH):
        mk_p, mv_p = mem_kv(mem_prompt, g_mem[l], w_mem_kv[l])
        mk_list.append(mk_p)
        mv_list.append(mv_p)
        xpn = rmsnorm(xp, g_attn[l])
        xsn = rmsnorm(xs, g_attn[l])
        if l < N_A:
            mix_p, S_p = gla_branch(xpn, w_in_a[l], w_gate_a[l], b_gate_a[l], g_gla[l], mk_p, mv_p,
                                    jnp.zeros((Bp, GLA_HEADS, GLA_DK, GLA_DV), F32))
            mix_s, S_s = gla_branch(xsn, w_in_a[l], w_gate_a[l], b_gate_a[l], g_gla[l],
                                    cache_mem_k[l], cache_mem_v[l], state_gla[l])
            gla_p.append(S_p.astype(xp.dtype))
            gla_s.append(S_s.astype(xs.dtype))
        else:
            if l == N_A:
                k_p, v_p, lf_p = shared_kv(xp, g_kv, w_kv, b_forget)
                c_p = jnp.cumsum(lf_p, axis=1)
                pos_p = jnp.arange(Tp)
                k_new, v_new, lf_new = shared_kv(xs, g_kv, w_kv, b_forget)
                k_s = jnp.concatenate([cache_fox_k.astype(k_new.dtype), k_new], axis=1)
                v_s = jnp.concatenate([cache_fox_v.astype(v_new.dtype), v_new], axis=1)
                c_s = jnp.cumsum(jnp.concatenate([cache_fox_logf.astype(F32), lf_new], axis=1), axis=1)
                pos_ks = jnp.arange(P + Ts)
                pos_qs = P + jnp.arange(Ts)
            j = l - N_A
            mix_p = fox_branch(xpn, w_in_b[j], mk_p, mv_p, k_p, v_p, c_p, c_p, pos_p, pos_p)
            mix_s = fox_branch(xsn, w_in_b[j], cache_mem_k[l], cache_mem_v[l],
                               k_s, v_s, c_s, c_s[:, P:], pos_qs, pos_ks)
        xp = xp + mix_p @ w_out[l]
        xs = xs + mix_s @ w_out[l]
        f_p, cv_p = conv_ffn(rmsnorm(xp, g_ffn[l]), w_up[l], w_conv[l], b_conv[l], w_down[l],
                             jnp.zeros((Bp, CONV_W - 1, 2 * D_FF), xp.dtype))
        f_s, cv_s = conv_ffn(rmsnorm(xs, g_ffn[l]), w_up[l], w_conv[l], b_conv[l], w_down[l],
                             state_ffn_conv[l])
        xp = xp + f_p
        xs = xs + f_s
        conv_p.append(cv_p)
        conv_s.append(cv_s)
    return (rmsnorm(xp, g_final), rmsnorm(xs, g_final),
            jnp.stack(gla_p), jnp.stack(gla_s),
            k_p, v_p, lf_p, k_new, v_new, lf_new,
            jnp.stack(mk_list), jnp.stack(mv_list),
            jnp.stack(conv_p), jnp.stack(conv_s))
```

```python
import functools

import numpy as np
import jax
import jax.numpy as jnp
from jax import lax
from jax.experimental import pallas as pl
from jax.experimental.pallas import tpu as pltpu

F32 = jnp.float32
BF16 = jnp.bfloat16

CHUNK = 64
GLA_RANK = 16
GLA_TAU = 16.0
MEM_DH = 256
CONV_W = 3
EPS = 1e-6

LANES = 128
SUBLANES = 8
VMEM_PHYSICAL_BYTES = 64 * 1024 * 1024
VMEM_HEADROOM_BYTES = 6 * 1024 * 1024


def _vmem_limit(estimate_bytes):
    want = int(estimate_bytes * 1.25) + 4 * 1024 * 1024
    return int(min(max(want, 16 * 1024 * 1024), VMEM_PHYSICAL_BYTES - VMEM_HEADROOM_BYTES))


def _params(semantics, estimate_bytes):
    return pltpu.CompilerParams(dimension_semantics=semantics, vmem_limit_bytes=_vmem_limit(estimate_bytes))


def _pick_tile(dim, preferred, unit):
    best = None
    for t in range(unit, min(dim, preferred) + 1, unit):
        if dim % t == 0:
            best = t
    assert best is not None, (dim, preferred, unit)
    return best


def _nt(a, b):
    return lax.dot_general(a, b, (((1,), (1,)), ((), ())), preferred_element_type=F32)


def _tn(a, b):
    return lax.dot_general(a, b, (((0,), (0,)), ((), ())), preferred_element_type=F32)


def _dot(a, b):
    return jnp.dot(a, b, preferred_element_type=F32)


def _split_bf16(x, parts):
    out = []
    rem = x
    for _ in range(parts):
        hi = rem.astype(BF16)
        out.append(hi)
        rem = rem - hi.astype(F32)
    return out


def _log_sigmoid(z):
    return jnp.minimum(z, 0.0) - jnp.log(1.0 + jnp.exp(-jnp.abs(z)))


def _silu(z):
    return z * (1.0 / (1.0 + jnp.exp(-z)))


def _rmsnorm_kernel(x_ref, g_ref, *o_refs):
    x = x_ref[...]
    y = x * lax.rsqrt(jnp.mean(x * x, axis=-1, keepdims=True) + EPS)
    for i, o_ref in enumerate(o_refs):
        o_ref[...] = (y * g_ref[i:i + 1, :]).astype(o_ref.dtype)


def rmsnorm(x, gains, out_dtype):
    rows, d = x.shape
    n = gains.shape[0]
    tm = min(256, rows)
    out_bytes = jnp.dtype(out_dtype).itemsize
    est = 2 * tm * d * 4 + n * 2 * tm * d * out_bytes + 2 * tm * d * 4
    outs = pl.pallas_call(
        _rmsnorm_kernel,
        grid=(rows // tm,),
        in_specs=[pl.BlockSpec((tm, d), lambda i: (i, 0)),
                  pl.BlockSpec((n, d), lambda i: (0, 0))],
        out_specs=[pl.BlockSpec((tm, d), lambda i: (i, 0)) for _ in range(n)],
        out_shape=[jax.ShapeDtypeStruct((rows, d), out_dtype) for _ in range(n)],
        compiler_params=_params(("parallel",), est),
        name="rmsnorm",
    )(x, gains)
    return list(outs)


def _matmul_kernel(*refs, part_blocks, has_scale, has_bias, has_res, act, n_out):
    n_parts = len(part_blocks)
    a_refs = refs[:n_parts]
    pos = n_parts
    w_ref = refs[pos]; pos += 1
    scale_ref = bias_ref = res_ref = None
    if has_scale:
        scale_ref = refs[pos]; pos += 1
    if has_bias:
        bias_ref = refs[pos]; pos += 1
    if has_res:
        res_ref = refs[pos]; pos += 1
    o_refs = refs[pos:pos + n_out]
    pos += n_out
    nk = sum(part_blocks)
    acc_ref = refs[pos] if nk > 1 else None
    k = pl.program_id(2)

    def finish(y):
        if has_scale:
            y = y * scale_ref[...]
        if has_bias:
            y = y + bias_ref[...]
        if act == "log_sigmoid":
            y = _log_sigmoid(y)
        if has_res:
            y = y + res_ref[...]
        for o_ref in o_refs:
            o_ref[...] = y.astype(o_ref.dtype)

    if nk == 1:
        finish(_dot(a_refs[0][...], w_ref[...]))
        return

    start = 0
    for a_ref, nb in zip(a_refs, part_blocks):
        lo, hi = start, start + nb

        @pl.when(jnp.logical_and(k >= lo, k < hi))
        def _(a_ref=a_ref):
            p = _dot(a_ref[...], w_ref[...])

            @pl.when(k == 0)
            def _():
                acc_ref[...] = p

            @pl.when(k > 0)
            def _():
                acc_ref[...] += p

        start = hi

    @pl.when(k == nk - 1)
    def _():
        finish(acc_ref[...])


def matmul(a_parts, w, out_dtypes, *, tm, tn, tk=None, scale=None, bias=None, res=None, act=None, name="matmul"):
    rows = a_parts[0].shape[0]
    kdim, n = w.shape
    tm = _pick_tile(rows, tm, SUBLANES)
    tn = _pick_tile(n, tn, LANES)
    if tk is None:
        assert len(a_parts) == 1
        tk = kdim
    part_blocks = tuple(a.shape[1] // tk for a in a_parts)
    assert all(a.shape[1] % tk == 0 for a in a_parts) and sum(a.shape[1] for a in a_parts) == kdim
    nk = sum(part_blocks)
    in_specs = []
    start = 0
    for nb in part_blocks:
        in_specs.append(pl.BlockSpec(
            (tm, tk), lambda i, j, k, s=start, nb=nb: (i, jnp.clip(k - s, 0, nb - 1))))
        start += nb
    in_specs.append(pl.BlockSpec((tk, tn), lambda i, j, k: (k, j)))
    args = list(a_parts) + [w]
    for extra in (scale, bias):
        if extra is not None:
            in_specs.append(pl.BlockSpec((1, tn), lambda i, j, k: (0, j)))
            args.append(extra)
    if res is not None:
        in_specs.append(pl.BlockSpec((tm, tn), lambda i, j, k: (i, j)))
        args.append(res)
    out_bytes = sum(jnp.dtype(d).itemsize for d in out_dtypes)
    est = (len(a_parts) * 2 * tm * tk * 2 + 2 * tk * tn * 2 + 2 * tm * tn * out_bytes
           + (2 * tm * tn * 4 if res is not None else 0) + 2 * tm * tn * 4)
    kern = functools.partial(
        _matmul_kernel, part_blocks=part_blocks, has_scale=scale is not None, has_bias=bias is not None,
        has_res=res is not None, act=act, n_out=len(out_dtypes))
    outs = pl.pallas_call(
        kern,
        grid=(rows // tm, n // tn, nk),
        in_specs=in_specs,
        out_specs=[pl.BlockSpec((tm, tn), lambda i, j, k: (i, j)) for _ in out_dtypes],
        out_shape=[jax.ShapeDtypeStruct((rows, n), d) for d in out_dtypes],
        scratch_shapes=[pltpu.VMEM((tm, tn), F32)] if nk > 1 else [],
        compiler_params=_params(("parallel", "parallel", "arbitrary"), est),
        name=name,
    )(*args)
    return list(outs)


def _ffn_up_kernel(*refs, seqs, rows_per_seq, tiles_per_seq, has_prev):
    if has_prev:
        (xn_ref, wa_ref, wg_ref, ca_ref, cg_ref, ba_ref, bg_ref, pa_ref, pg_ref,
         h_ref, ta_ref, tg_ref, buf_ref) = refs
        carry_ref = None
    else:
        (xn_ref, wa_ref, wg_ref, ca_ref, cg_ref, ba_ref, bg_ref,
         h_ref, ta_ref, tg_ref, buf_ref, carry_ref) = refs
    i = pl.program_id(0)
    f = pl.program_id(1)
    S, L = seqs, rows_per_seq
    tf = wa_ref.shape[1]
    xn = xn_ref[...]
    halo = CONV_W - 1
    base = SUBLANES

    def conv(which, w_ref, c_ref, b_ref, prev_ref, tail_ref):
        u = _dot(xn, w_ref[...]).reshape(S, L, tf)
        if has_prev:
            prev = prev_ref[...]
        else:
            @pl.when((i % tiles_per_seq) == 0)
            def _():
                carry_ref[f, which] = jnp.zeros(carry_ref.shape[2:], F32)

            prev = carry_ref[f, which, 0:halo, :][None]
        buf_ref[which, :, base:base + L, :] = u
        buf_ref[which, :, base - halo:base, :] = prev
        z = b_ref[...].reshape(1, 1, tf) + c_ref[halo:halo + 1, :].reshape(1, 1, tf) * u
        for d in range(1, CONV_W):
            z = z + c_ref[halo - d:halo - d + 1, :].reshape(1, 1, tf) * buf_ref[which, :, base - d:base - d + L, :]
        tail = u[:, L - halo:, :]
        tail_ref[...] = tail
        if not has_prev:
            carry_ref[f, which, 0:halo, :] = tail[0]
        return z

    za = conv(0, wa_ref, ca_ref, ba_ref, pa_ref if has_prev else None, ta_ref)
    zg = conv(1, wg_ref, cg_ref, bg_ref, pg_ref if has_prev else None, tg_ref)
    h_ref[...] = (_silu(zg) * za).reshape(S * L, tf).astype(h_ref.dtype)


def ffn_up(xn, w_up, w_conv, b_conv, prev, *, nseq, seq_len, tf):
    rows, d = xn.shape
    ff = w_up.shape[1] // 2
    nf = ff // tf
    has_prev = prev is not None
    if has_prev:
        S, L, tiles_per_seq = nseq, seq_len, 1
        assert rows == S * L
        n_tiles = 1
    else:
        L = min(1024, seq_len)
        S, tiles_per_seq = 1, seq_len // L
        n_tiles = nseq * tiles_per_seq
    tm = S * L
    halo = CONV_W - 1
    in_specs = [
        pl.BlockSpec((tm, d), lambda i, f: (i, 0)),
        pl.BlockSpec((d, tf), lambda i, f: (0, f)),
        pl.BlockSpec((d, tf), lambda i, f: (0, nf + f)),
        pl.BlockSpec((CONV_W, tf), lambda i, f: (0, f)),
        pl.BlockSpec((CONV_W, tf), lambda i, f: (0, nf + f)),
        pl.BlockSpec((1, tf), lambda i, f: (0, f)),
        pl.BlockSpec((1, tf), lambda i, f: (0, nf + f)),
    ]
    args = [xn, w_up, w_up, w_conv, w_conv, b_conv, b_conv]
    scratch = [pltpu.VMEM((2, S, L + SUBLANES, tf), F32)]
    if has_prev:
        in_specs += [pl.BlockSpec((S, halo, tf), lambda i, f: (0, 0, f)),
                     pl.BlockSpec((S, halo, tf), lambda i, f: (0, 0, nf + f))]
        args += [prev, prev]
    else:
        scratch.append(pltpu.VMEM((nf, 2, SUBLANES, tf), F32))
    est = (2 * tm * d * 2 + 2 * 2 * d * tf * 2 + 2 * tm * tf * 2
           + 2 * S * (L + SUBLANES) * tf * 4 + 6 * tm * tf * 4)
    kern = functools.partial(_ffn_up_kernel, seqs=S, rows_per_seq=L, tiles_per_seq=tiles_per_seq,
                             has_prev=has_prev)
    h, tail_a, tail_g = pl.pallas_call(
        kern,
        grid=(n_tiles, nf),
        in_specs=in_specs,
        out_specs=[pl.BlockSpec((tm, tf), lambda i, f: (i, f)),
                   pl.BlockSpec((S, halo, tf), lambda i, f: (i, 0, f)),
                   pl.BlockSpec((S, halo, tf), lambda i, f: (i, 0, f))],
        out_shape=[jax.ShapeDtypeStruct((rows, ff), BF16),
                   jax.ShapeDtypeStruct((n_tiles * S, halo, ff), F32),
                   jax.ShapeDtypeStruct((n_tiles * S, halo, ff), F32)],
        scratch_shapes=scratch,
        compiler_params=_params(("arbitrary", "arbitrary"), est),
        name="ffn_up",
    )(*args)
    tails = jnp.concatenate([tail_a, tail_g], axis=-1)
    if not has_prev:
        tails = tails[tiles_per_seq - 1::tiles_per_seq]
    return h, tails


def _gla_tables():
    C = CHUNK
    t = np.arange(C)[:, None]
    s = np.arange(C)[None, :]
    mats = [(s <= t)]
    masks = [(s == t)]
    m = C
    while m >= 2:
        mid = (t // m) * m + m // 2
        mats.append(((s > mid) & (s <= t)) | ((s > t) & (s <= mid)))
        masks.append((t // m == s // m) & (t % m >= m // 2) & (s % m < m // 2))
        m //= 2
    mats.append(s > t)
    return (np.concatenate(mats, axis=0).astype(np.float32),
            np.stack(masks, axis=0).astype(np.float32))


def _gla_kernel(*refs, heads, dk, dv, n_chunks, zero_init):
    if zero_init:
        (q_ref, k_ref, v_ref, r_ref, lr_ref, wg_ref, bg_ref, gn_ref, tab_ref, msk_ref,
         o_ref, sout_ref, st_ref, la_ref) = refs
        s0_ref = None
    else:
        (q_ref, k_ref, v_ref, r_ref, lr_ref, wg_ref, bg_ref, gn_ref, tab_ref, msk_ref, s0_ref,
         o_ref, sout_ref, st_ref, la_ref) = refs
    t = pl.program_id(2)
    n_tiles = pl.num_programs(2)
    C = CHUNK
    n_levels = msk_ref.shape[0]

    @pl.when(t == 0)
    def _():
        for h in range(heads):
            if zero_init:
                st_ref[h] = jnp.zeros((dv, dk), F32)
            else:
                st_ref[h] = s0_ref[0, h].T

    z = _dot(lr_ref[...].astype(BF16), wg_ref[...]) + bg_ref[...]
    la_ref[...] = _log_sigmoid(z) * (1.0 / GLA_TAU)
    tab = tab_ref[...]

    def chunk(c, carry):
        r0 = pl.multiple_of(c * C, C)
        la = la_ref[pl.ds(r0, C), :]
        sums = None
        for part in _split_bf16(la, 2):
            p = _dot(tab, part)
            sums = p if sums is None else sums + p
        decay = jnp.exp(sums)
        for h in range(heads):
            ks = slice(h * dk, (h + 1) * dk)
            vs = slice(h * dv, (h + 1) * dv)
            q = q_ref[pl.ds(r0, C), ks].astype(F32)
            k = k_ref[pl.ds(r0, C), ks].astype(F32)
            v = v_ref[pl.ds(r0, C), vs]
            att = msk_ref[0] * _nt(q.astype(BF16), k.astype(BF16))
            for lvl in range(1, n_levels):
                e = decay[lvl * C:(lvl + 1) * C, ks]
                att = att + msk_ref[lvl] * _nt((q * e).astype(BF16), (k * e).astype(BF16))
            e_in = decay[0:C, ks]
            st = st_ref[h]
            o = _dot(att.astype(BF16), v) + _nt((q * e_in).astype(BF16), st.astype(BF16))
            k_end = (k * decay[n_levels * C:(n_levels + 1) * C, ks]).astype(BF16)
            st_ref[h] = st * e_in[C - 1:C, :] + _tn(v, k_end)
            on = o * lax.rsqrt(jnp.mean(o * o, axis=-1, keepdims=True) + EPS) * gn_ref[...]
            gate = _silu(r_ref[pl.ds(r0, C), vs].astype(F32))
            o_ref[pl.ds(r0, C), vs] = (on * gate).astype(o_ref.dtype)
        return carry

    lax.fori_loop(0, n_chunks, chunk, 0)

    @pl.when(t == n_tiles - 1)
    def _():
        for h in range(heads):
            sout_ref[0, h] = st_ref[h].T


def gla(proj, lr, wg, bg, gn, s0, *, nseq, seq_len, n_heads, dk, dv):
    rows = proj.shape[0]
    hg = min(4, n_heads)
    n_groups = n_heads // hg
    tt = min(512, seq_len)
    n_tiles = seq_len // tt
    tab, msk = _gla_tables()
    tab = jnp.asarray(tab, BF16)
    msk = jnp.asarray(msk, F32)
    qw, vw = hg * dk, hg * dv
    k_off = (n_heads * dk) // qw
    v_off = (2 * n_heads * dk) // vw
    r_off = (2 * n_heads * dk + n_heads * dv) // vw
    row = lambda b, g, t: b * n_tiles + t
    in_specs = [
        pl.BlockSpec((tt, qw), lambda b, g, t: (row(b, g, t), g)),
        pl.BlockSpec((tt, qw), lambda b, g, t: (row(b, g, t), k_off + g)),
        pl.BlockSpec((tt, vw), lambda b, g, t: (row(b, g, t), v_off + g)),
        pl.BlockSpec((tt, vw), lambda b, g, t: (row(b, g, t), r_off + g)),
        pl.BlockSpec((tt, LANES), lambda b, g, t: (row(b, g, t), 0)),
        pl.BlockSpec((LANES, qw), lambda b, g, t: (0, g)),
        pl.BlockSpec((1, qw), lambda b, g, t: (0, g)),
        pl.BlockSpec((1, dv), lambda b, g, t: (0, 0)),
        pl.BlockSpec(tab.shape, lambda b, g, t: (0, 0)),
        pl.BlockSpec(msk.shape, lambda b, g, t: (0, 0, 0)),
    ]
    args = [proj, proj, proj, proj, lr, wg, bg, gn, tab, msk]
    zero_init = s0 is None
    if not zero_init:
        in_specs.append(pl.BlockSpec((1, hg, dk, dv), lambda b, g, t: (b, g, 0, 0)))
        args.append(s0)
    est = (2 * tt * (2 * qw + 2 * vw) * 2 + 2 * tt * LANES * 4 + 2 * tt * vw * 2
           + 4 * hg * dk * dv * 4 + hg * dk * dv * 4 + tt * qw * 4 + 4 * tab.shape[0] * qw * 4)
    kern = functools.partial(_gla_kernel, heads=hg, dk=dk, dv=dv, n_chunks=tt // CHUNK, zero_init=zero_init)
    o, s_out = pl.pallas_call(
        kern,
        grid=(nseq, n_groups, n_tiles),
        in_specs=in_specs,
        out_specs=[pl.BlockSpec((tt, vw), lambda b, g, t: (row(b, g, t), g)),
                   pl.BlockSpec((1, hg, dk, dv), lambda b, g, t: (b, g, 0, 0))],
        out_shape=[jax.ShapeDtypeStruct((rows, n_heads * dv), BF16),
                   jax.ShapeDtypeStruct((nseq, n_heads, dk, dv), F32)],
        scratch_shapes=[pltpu.VMEM((hg, dv, dk), F32), pltpu.VMEM((tt, qw), F32)],
        compiler_params=_params(("parallel", "parallel", "arbitrary"), est),
        name="gla",
    )(*args)
    return o, s_out


def _mem_attn_kernel(q_ref, k_ref, v_ref, o_ref, *, n_heads):
    for h in range(n_heads):
        cs = slice(h * MEM_DH, (h + 1) * MEM_DH)
        s = _nt(q_ref[:, cs], k_ref[0, :, cs].astype(BF16))
        p = jnp.exp(s - jnp.max(s, axis=-1, keepdims=True))
        l = jnp.sum(p, axis=-1, keepdims=True)
        o = _dot(p.astype(BF16), v_ref[0, :, cs].astype(BF16))
        o_ref[:, cs] = (o / l).astype(o_ref.dtype)


def mem_attn(qsrc, q_col_block, mk, mv, *, seq_len):
    rows = qsrc.shape[0]
    nseq, n_mem, mq = mk.shape
    tm = min(1024, seq_len)
    tiles_per_seq = seq_len // tm
    est = 2 * tm * mq * 2 * 2 + 2 * 2 * n_mem * mq * 4 + 6 * tm * n_mem * 4
    return pl.pallas_call(
        functools.partial(_mem_attn_kernel, n_heads=mq // MEM_DH),
        grid=(rows // tm,),
        in_specs=[pl.BlockSpec((tm, mq), lambda i: (i, q_col_block)),
                  pl.BlockSpec((1, n_mem, mq), lambda i: (i // tiles_per_seq, 0, 0)),
                  pl.BlockSpec((1, n_mem, mq), lambda i: (i // tiles_per_seq, 0, 0))],
        out_specs=pl.BlockSpec((tm, mq), lambda i: (i, 0)),
        out_shape=jax.ShapeDtypeStruct((rows, mq), BF16),
        compiler_params=_params(("parallel",), est),
        name="mem_attn",
    )(qsrc, mk, mv)


def _cumsum_kernel(lf_ref, c_ref, ct_ref, crow_ref, ccol_ref):
    j = pl.program_id(1)
    tb = lf_ref.shape[1]

    @pl.when(j == 0)
    def _():
        crow_ref[...] = jnp.zeros_like(crow_ref)
        ccol_ref[...] = jnp.zeros_like(ccol_ref)

    r = lax.broadcasted_iota(jnp.int32, (tb, tb), 0)
    s = lax.broadcasted_iota(jnp.int32, (tb, tb), 1)
    lower = jnp.where(s <= r, 1.0, 0.0).astype(BF16)
    upper = jnp.where(r <= s, 1.0, 0.0).astype(BF16)
    c = crow_ref[0:1, :]
    ct = ccol_ref[:, 0:1]
    for part in _split_bf16(lf_ref[0], 3):
        c = c + _dot(lower, part)
        ct = ct + _tn(part, upper)
    c_ref[0] = c
    ct_ref[0] = ct
    crow_ref[...] = jnp.broadcast_to(c[tb - 1:tb, :], crow_ref.shape)
    ccol_ref[...] = jnp.broadcast_to(ct[:, tb - 1:tb], ccol_ref.shape)


def cumsum_time(lf):
    nseq, L, w = lf.shape
    tb = L if L <= 2048 else 512
    est = 4 * tb * w * 4 + 4 * tb * tb * 4 + 8 * tb * w * 4
    return pl.pallas_call(
        _cumsum_kernel,
        grid=(nseq, L // tb),
        in_specs=[pl.BlockSpec((1, tb, w), lambda b, j: (b, j, 0))],
        out_specs=[pl.BlockSpec((1, tb, w), lambda b, j: (b, j, 0)),
                   pl.BlockSpec((1, w, tb), lambda b, j: (b, 0, j))],
        out_shape=[jax.ShapeDtypeStruct((nseq, L, w), F32), jax.ShapeDtypeStruct((nseq, w, L), F32)],
        scratch_shapes=[pltpu.VMEM((SUBLANES, w), F32), pltpu.VMEM((w, LANES), F32)],
        compiler_params=_params(("parallel", "arbitrary"), est),
        name="cumsum_time",
    )(lf)


def _pick_lane(x, lane):
    ids = lax.broadcasted_iota(jnp.int32, x.shape, 1)
    return jnp.sum(jnp.where(ids == lane, x, 0.0), axis=-1, keepdims=True)


def _fox_kernel(qi_ref, ki_ref, q_ref, k_ref, v_ref, c_ref, ct_ref, o_ref, m_ref, l_ref, acc_ref, cq_ref):
    h = pl.program_id(1)
    p = pl.program_id(2)
    qi = qi_ref[p]
    ki = ki_ref[p]
    tq, tk = q_ref.shape[0], k_ref.shape[0]

    @pl.when(ki == 0)
    def _():
        m_ref[...] = jnp.full_like(m_ref, -jnp.inf)
        l_ref[...] = jnp.zeros_like(l_ref)
        acc_ref[...] = jnp.zeros_like(acc_ref)
        cq_ref[...] = _pick_lane(c_ref[0], h)

    def step(masked):
        s = _nt(q_ref[...], k_ref[...]) + cq_ref[...] - ct_ref[0, 0]
        if masked:
            row = lax.broadcasted_iota(jnp.int32, (tq, tk), 0)
            col = lax.broadcasted_iota(jnp.int32, (tq, tk), 1)
            s = jnp.where(col <= row, s, -jnp.inf)
        m_old = m_ref[...]
        m_new = jnp.maximum(m_old, jnp.max(s, axis=-1, keepdims=True))
        alpha = jnp.exp(m_old - m_new)
        pr = jnp.exp(s - m_new)
        l_ref[...] = alpha * l_ref[...] + jnp.sum(pr, axis=-1, keepdims=True)
        acc_ref[...] = alpha * acc_ref[...] + _dot(pr.astype(BF16), v_ref[...])
        m_ref[...] = m_new

    @pl.when(ki < qi)
    def _():
        step(False)

    @pl.when(ki == qi)
    def _():
        step(True)
        o_ref[...] = (acc_ref[...] / l_ref[...]).astype(o_ref.dtype)


def fox_prompt(qsrc, kbf, vbf, c, ct4, *, nseq, seq_len, n_heads, dh):
    rows = qsrc.shape[0]
    tb = min(1024, seq_len)
    nb = seq_len // tb
    pairs = [(qi, ki) for qi in range(nb) for ki in range(qi + 1)]
    qi_tab = jnp.asarray([p[0] for p in pairs], jnp.int32)
    ki_tab = jnp.asarray([p[1] for p in pairs], jnp.int32)
    est = (2 * 3 * tb * dh * 2 + 2 * tb * LANES * 4 + 2 * tb * dh * 2 + tb * dh * 4
           + 3 * tb * LANES * 4 + 6 * tb * tb * 4)
    grid_spec = pltpu.PrefetchScalarGridSpec(
        num_scalar_prefetch=2,
        grid=(nseq, n_heads, len(pairs)),
        in_specs=[
            pl.BlockSpec((tb, dh), lambda b, h, p, qt, kt: (b * nb + qt[p], h)),
            pl.BlockSpec((tb, dh), lambda b, h, p, qt, kt: (b * nb + kt[p], h)),
            pl.BlockSpec((tb, dh), lambda b, h, p, qt, kt: (b * nb + kt[p], h)),
            pl.BlockSpec((1, tb, LANES), lambda b, h, p, qt, kt: (b, qt[p], 0)),
            pl.BlockSpec((1, 1, 1, tb), lambda b, h, p, qt, kt: (b, h, 0, kt[p])),
        ],
        out_specs=pl.BlockSpec((tb, dh), lambda b, h, p, qt, kt: (b * nb + qt[p], h)),
        scratch_shapes=[pltpu.VMEM((tb, 1), F32), pltpu.VMEM((tb, 1), F32),
                        pltpu.VMEM((tb, dh), F32), pltpu.VMEM((tb, 1), F32)],
    )
    return pl.pallas_call(
        _fox_kernel,
        grid_spec=grid_spec,
        out_shape=jax.ShapeDtypeStruct((rows, n_heads * dh), BF16),
        compiler_params=_params(("parallel", "parallel", "arbitrary"), est),
        name="fox_prompt",
    )(qi_tab, ki_tab, qsrc, kbf, vbf, c, ct4)


def _fox_decode_kernel(q_ref, kc_ref, vc_ref, kn_ref, vn_ref, c_ref, ct_ref, o_ref):
    h = pl.program_id(1)
    ts = q_ref.shape[0]
    past = kc_ref.shape[1]
    q = q_ref[...]
    cq = _pick_lane(c_ref[0], h)
    ck = ct_ref[0, 0]
    s1 = _nt(q, kc_ref[0].astype(BF16)) + cq - ck[:, 0:past]
    s2 = _nt(q, kn_ref[...]) + cq - ck[:, past:past + ts]
    row = lax.broadcasted_iota(jnp.int32, (ts, ts), 0)
    col = lax.broadcasted_iota(jnp.int32, (ts, ts), 1)
    s2 = jnp.where(col <= row, s2, -jnp.inf)
    m = jnp.maximum(jnp.max(s1, axis=-1, keepdims=True), jnp.max(s2, axis=-1, keepdims=True))
    p1 = jnp.exp(s1 - m)
    p2 = jnp.exp(s2 - m)
    l = jnp.sum(p1, axis=-1, keepdims=True) + jnp.sum(p2, axis=-1, keepdims=True)
    o = _dot(p1.astype(BF16), vc_ref[0].astype(BF16)) + _dot(p2.astype(BF16), vn_ref[...])
    o_ref[...] = (o / l).astype(o_ref.dtype)


def fox_decode(qsrc, kcache, vcache, kbf, vbf, c, ct4, *, n_heads, dh):
    nseq, past, _ = kcache.shape
    rows = qsrc.shape[0]
    ts = rows // nseq
    tot = past + ts
    est = 2 * 2 * past * dh * 4 + 8 * ts * dh * 2 + 2 * ts * LANES * 4 + 8 * ts * tot * 4 + 2 * past * dh * 2
    return pl.pallas_call(
        _fox_decode_kernel,
        grid=(nseq, n_heads),
        in_specs=[
            pl.BlockSpec((ts, dh), lambda b, h: (b, h)),
            pl.BlockSpec((1, past, dh), lambda b, h: (b, 0, h)),
            pl.BlockSpec((1, past, dh), lambda b, h: (b, 0, h)),
            pl.BlockSpec((ts, dh), lambda b, h: (b, h)),
            pl.BlockSpec((ts, dh), lambda b, h: (b, h)),
            pl.BlockSpec((1, ts, LANES), lambda b, h: (b, past // ts, 0)),
            pl.BlockSpec((1, 1, 1, tot), lambda b, h: (b, h, 0, 0)),
        ],
        out_specs=pl.BlockSpec((ts, dh), lambda b, h: (b, h)),
        out_shape=jax.ShapeDtypeStruct((rows, n_heads * dh), BF16),
        compiler_params=_params(("parallel", "parallel"), est),
        name="fox_decode",
    )(qsrc, kcache, vcache, kbf, vbf, c, ct4)


def _pad_cols(x, width):
    return jnp.pad(x, ((0, 0), (0, width - x.shape[1])))


def _ffn_block(x, g, w_up, w_conv, b_conv, w_down, prev, *, nseq, seq_len):
    (xn,) = rmsnorm(x, g[None], BF16)
    ff = w_down.shape[0]
    tf = 256 if ff % 256 == 0 else LANES
    h, tails = ffn_up(xn, w_up, w_conv, b_conv, prev, nseq=nseq, seq_len=seq_len, tf=tf)
    (x,) = matmul([h], w_down, [F32], tm=512, tn=512, res=x, name="ffn_down")
    return x, tails


def kernel(x_prompt, x_sample, state_gla, cache_fox_k, cache_fox_v, cache_fox_logf, cache_mem_k, cache_mem_v,
           state_ffn_conv, mem_prompt, g_attn, g_ffn, g_mem, w_mem_kv, w_in_a, w_gate_a, b_gate_a, g_gla,
           w_in_b, g_kv, w_kv, b_forget, w_out, w_up, w_conv, b_conv, w_down, g_final):
    n_p, t_p, d = x_prompt.shape
    n_s, t_s, _ = x_sample.shape
    depth = g_attn.shape[0]
    n_a = state_gla.shape[0]
    _, _, gla_heads, dk, dv = state_gla.shape
    _, past, fox_heads, dh = cache_fox_k.shape
    n_mem, mq = cache_mem_k.shape[2:]
    a_q, a_v, b_q = gla_heads * dk, gla_heads * dv, fox_heads * dh

    streams = [
        dict(x=x_prompt.reshape(n_p * t_p, d), nseq=n_p, seq_len=t_p),
        dict(x=x_sample.reshape(n_s * t_s, d), nseq=n_s, seq_len=t_s),
    ]
    mem_rows = mem_prompt.reshape(n_p * n_mem, d)
    mk_out, mv_out, conv_out = [], [], [[], []]
    gla_out = [[], []]
    fox = [None, None]

    for l in range(depth):
        (mem_n,) = rmsnorm(mem_rows, g_mem[l][None], BF16)
        (mkv,) = matmul([mem_n], w_mem_kv[l].astype(BF16), [F32], tm=512, tn=1024, name="mem_kv")
        mk_p = mkv[:, :mq].reshape(n_p, n_mem, mq)
        mv_p = mkv[:, mq:].reshape(n_p, n_mem, mq)
        mk_out.append(mk_p)
        mv_out.append(mv_p)
        mem_kv = [(mk_p, mv_p), (cache_mem_k[l], cache_mem_v[l])]
        w_o = w_out[l].astype(BF16)

        if l < n_a:
            w = w_in_a[l]
            main = 2 * a_q + 2 * a_v
            w_main = jnp.concatenate([w[:, :main], w[:, main + GLA_RANK:]], axis=1).astype(BF16)
            w_lr = _pad_cols(w[:, main:main + GLA_RANK], LANES).astype(BF16)
            col_scale = jnp.concatenate([
                jnp.full((a_q,), dk ** -0.5, F32),
                jnp.ones((a_q + 2 * a_v,), F32),
                jnp.full((mq,), MEM_DH ** -0.5, F32)])[None]
            wg = jnp.pad(w_gate_a[l], ((0, LANES - GLA_RANK), (0, 0))).astype(BF16)
            bg = b_gate_a[l][None]
            gn = g_gla[l][None]
        else:
            j = l - n_a
            w_main = w_in_b[j].astype(BF16)
            col_scale = jnp.concatenate([jnp.full((b_q,), dh ** -0.5, F32),
                                         jnp.full((mq,), MEM_DH ** -0.5, F32)])[None]
            if l == n_a:
                w_kv_k = w_kv[:, :b_q].astype(BF16)
                w_kv_v = w_kv[:, b_q:2 * b_q].astype(BF16)
                w_kv_f = _pad_cols(w_kv[:, 2 * b_q:], LANES).astype(BF16)
                b_f = _pad_cols(b_forget[None], LANES)

        for si, st in enumerate(streams):
            x, nseq, seq_len = st["x"], st["nseq"], st["seq_len"]
            mk, mv = mem_kv[si]
            if l < n_a:
                (xn,) = rmsnorm(x, g_attn[l][None], BF16)
                (proj,) = matmul([xn], w_main, [BF16], tm=1024, tn=1024, scale=col_scale, name="in_proj_a")
                (lr,) = matmul([xn], w_lr, [F32], tm=1024, tn=LANES, name="in_proj_a_gate")
                s0 = None if si == 0 else state_gla[l]
                o, s_new = gla(proj, lr, wg, bg, gn, s0, nseq=nseq, seq_len=seq_len,
                               n_heads=gla_heads, dk=dk, dv=dv)
                gla_out[si].append(s_new)
                mo = mem_attn(proj, (2 * a_q + 2 * a_v) // mq, mk, mv, seq_len=seq_len)
            else:
                if l == n_a:
                    xkv, xn = rmsnorm(x, jnp.stack([g_kv, g_attn[l]]), BF16)
                    k32, kbf = matmul([xkv], w_kv_k, [F32, BF16], tm=1024, tn=1024, name="k_proj")
                    v32, vbf = matmul([xkv], w_kv_v, [F32, BF16], tm=1024, tn=1024, name="v_proj")
                    (lf,) = matmul([xkv], w_kv_f, [F32], tm=1024, tn=LANES, bias=b_f, act="log_sigmoid",
                                   name="kv_proj_forget")
                    lf3 = lf.reshape(nseq, seq_len, LANES)
                    if si == 0:
                        lf_all = lf3
                    else:
                        lf_all = jnp.concatenate(
                            [jnp.pad(cache_fox_logf, ((0, 0), (0, 0), (0, LANES - fox_heads))), lf3], axis=1)
                    c, ct = cumsum_time(lf_all)
                    ct4 = ct[:, :fox_heads].reshape(nseq, fox_heads, 1, ct.shape[-1])
                    fox[si] = dict(k32=k32, v32=v32, kbf=kbf, vbf=vbf, lf=lf3[:, :, :fox_heads], c=c, ct4=ct4)
                else:
                    (xn,) = rmsnorm(x, g_attn[l][None], BF16)
                fx = fox[si]
                (proj,) = matmul([xn], w_main, [BF16], tm=1024, tn=1024, scale=col_scale, name="in_proj_b")
                kb, vb = fx["kbf"], fx["vbf"]
                if si == 0:
                    o = fox_prompt(proj, kb, vb, fx["c"], fx["ct4"], nseq=nseq, seq_len=seq_len,
                                   n_heads=fox_heads, dh=dh)
                else:
                    o = fox_decode(proj, cache_fox_k.reshape(nseq, past, b_q), cache_fox_v.reshape(nseq, past, b_q),
                                   kb, vb, fx["c"], fx["ct4"], n_heads=fox_heads, dh=dh)
                mo = mem_attn(proj, b_q // mq, mk, mv, seq_len=seq_len)
            (x,) = matmul([o, mo], w_o, [F32], tm=1024, tn=1024, tk=mq, res=x, name="out_proj")
            prev = None if si == 0 else state_ffn_conv[l]
            x, tails = _ffn_block(x, g_ffn[l], w_up[l].astype(BF16), w_conv[l], b_conv[l][None],
                                  w_down[l].astype(BF16), prev, nseq=nseq, seq_len=seq_len)
            conv_out[si].append(tails)
            st["x"] = x

    ys = []
    for st in streams:
        (y,) = rmsnorm(st["x"], g_final[None], F32)
        ys.append(y)
    fp, fs = fox
    return (ys[0].reshape(n_p, t_p, d), ys[1].reshape(n_s, t_s, d),
            jnp.stack(gla_out[0]), jnp.stack(gla_out[1]),
            fp["k32"].reshape(n_p, t_p, fox_heads, dh), fp["v32"].reshape(n_p, t_p, fox_heads, dh), fp["lf"],
            fs["k32"].reshape(n_s, t_s, fox_heads, dh), fs["v32"].reshape(n_s, t_s, fox_heads, dh), fs["lf"],
            jnp.stack(mk_out), jnp.stack(mv_out),
            jnp.stack(conv_out[0]), jnp.stack(conv_out[1]))
```

```python
import functools

import numpy as np
import jax
import jax.numpy as jnp
from jax import lax
from jax.experimental import pallas as pl
from jax.experimental.pallas import tpu as pltpu

F32 = jnp.float32
BF16 = jnp.bfloat16

CHUNK = 64
GLA_RANK = 16
GLA_TAU = 16.0
MEM_DH = 256
CONV_W = 3
EPS = 1e-6

LANES = 128
SUBLANES = 8
VMEM_PHYSICAL_BYTES = 64 * 1024 * 1024
VMEM_HEADROOM_BYTES = 6 * 1024 * 1024


def _vmem_limit(estimate_bytes):
    want = int(estimate_bytes * 1.25) + 4 * 1024 * 1024
    return int(min(max(want, 16 * 1024 * 1024), VMEM_PHYSICAL_BYTES - VMEM_HEADROOM_BYTES))


def _params(semantics, estimate_bytes):
    return pltpu.CompilerParams(dimension_semantics=semantics, vmem_limit_bytes=_vmem_limit(estimate_bytes))


def _pick_tile(dim, preferred, unit):
    best = None
    for t in range(unit, min(dim, preferred) + 1, unit):
        if dim % t == 0:
            best = t
    assert best is not None, (dim, preferred, unit)
    return best


def _nt(a, b):
    return lax.dot_general(a, b, (((1,), (1,)), ((), ())), preferred_element_type=F32)


def _tn(a, b):
    return lax.dot_general(a, b, (((0,), (0,)), ((), ())), preferred_element_type=F32)


def _dot(a, b):
    return jnp.dot(a, b, preferred_element_type=F32)


def _split_bf16(x, parts):
    out = []
    rem = x
    for _ in range(parts):
        hi = rem.astype(BF16)
        out.append(hi)
        rem = rem - hi.astype(F32)
    return out


def _log_sigmoid(z):
    return jnp.minimum(z, 0.0) - jnp.log(1.0 + jnp.exp(-jnp.abs(z)))


def _silu(z):
    return z * (1.0 / (1.0 + jnp.exp(-z)))


def _rmsnorm_kernel(x_ref, g_ref, *o_refs):
    x = x_ref[...]
    y = x * lax.rsqrt(jnp.mean(x * x, axis=-1, keepdims=True) + EPS)
    for i, o_ref in enumerate(o_refs):
        o_ref[...] = (y * g_ref[i:i + 1, :]).astype(o_ref.dtype)


def rmsnorm(x, gains, out_dtype):
    rows, d = x.shape
    n = gains.shape[0]
    tm = min(256, rows)
    out_bytes = jnp.dtype(out_dtype).itemsize
    est = 2 * tm * d * 4 + n * 2 * tm * d * out_bytes + 2 * tm * d * 4
    outs = pl.pallas_call(
        _rmsnorm_kernel,
        grid=(rows // tm,),
        in_specs=[pl.BlockSpec((tm, d), lambda i: (i, 0)),
                  pl.BlockSpec((n, d), lambda i: (0, 0))],
        out_specs=[pl.BlockSpec((tm, d), lambda i: (i, 0)) for _ in range(n)],
        out_shape=[jax.ShapeDtypeStruct((rows, d), out_dtype) for _ in range(n)],
        compiler_params=_params(("parallel",), est),
        name="rmsnorm",
    )(x, gains)
    return list(outs)


def _matmul_kernel(*refs, n_parts, has_scale, has_bias, has_res, act, n_out):
    a_refs = refs[:n_parts]
    w_refs = refs[n_parts:2 * n_parts]
    pos = 2 * n_parts
    scale_ref = bias_ref = res_ref = None
    if has_scale:
        scale_ref = refs[pos]; pos += 1
    if has_bias:
        bias_ref = refs[pos]; pos += 1
    if has_res:
        res_ref = refs[pos]; pos += 1
    o_refs = refs[pos:pos + n_out]

    y = _dot(a_refs[0][...], w_refs[0][...])
    for a_ref, w_ref in zip(a_refs[1:], w_refs[1:]):
        y = y + _dot(a_ref[...], w_ref[...])
    if has_scale:
        y = y * scale_ref[...]
    if has_bias:
        y = y + bias_ref[...]
    if act == "log_sigmoid":
        y = _log_sigmoid(y)
    if has_res:
        y = y + res_ref[...]
    for o_ref in o_refs:
        o_ref[...] = y.astype(o_ref.dtype)


def matmul(a_parts, w, out_dtypes, *, tm, tn, scale=None, bias=None, res=None, act=None, name="matmul"):
    rows = a_parts[0].shape[0]
    kdim, n = w.shape
    tm = _pick_tile(rows, tm, SUBLANES)
    tn = _pick_tile(n, tn, LANES)
    assert sum(a.shape[1] for a in a_parts) == kdim
    a_specs, w_specs = [], []
    start = 0
    for a in a_parts:
        kp = a.shape[1]
        assert start % kp == 0, "a part must start at a multiple of its own width in W"
        a_specs.append(pl.BlockSpec((tm, kp), lambda i, j: (i, 0)))
        w_specs.append(pl.BlockSpec((kp, tn), lambda i, j, b=start // kp: (b, j)))
        start += kp
    in_specs = a_specs + w_specs
    args = list(a_parts) + [w] * len(a_parts)
    for extra in (scale, bias):
        if extra is not None:
            in_specs.append(pl.BlockSpec((1, tn), lambda i, j: (0, j)))
            args.append(extra)
    if res is not None:
        in_specs.append(pl.BlockSpec((tm, tn), lambda i, j: (i, j)))
        args.append(res)
    out_bytes = sum(jnp.dtype(d).itemsize for d in out_dtypes)
    est = (2 * tm * kdim * 2 + 2 * kdim * tn * 2 + 2 * tm * tn * out_bytes
           + (2 * tm * tn * 4 if res is not None else 0) + 2 * tm * tn * 4)
    kern = functools.partial(
        _matmul_kernel, n_parts=len(a_parts), has_scale=scale is not None, has_bias=bias is not None,
        has_res=res is not None, act=act, n_out=len(out_dtypes))
    outs = pl.pallas_call(
        kern,
        grid=(rows // tm, n // tn),
        in_specs=in_specs,
        out_specs=[pl.BlockSpec((tm, tn), lambda i, j: (i, j)) for _ in out_dtypes],
        out_shape=[jax.ShapeDtypeStruct((rows, n), d) for d in out_dtypes],
        compiler_params=_params(("parallel", "parallel"), est),
        name=name,
    )(*args)
    return list(outs)


FFN_ROW_BLOCK = 256


def _shift_rows(u, d, prev):
    halo = CONV_W - 1
    rolled = pltpu.roll(u, d, 0)
    head = rolled[0:SUBLANES]
    row = lax.broadcasted_iota(jnp.int32, head.shape, 0)
    for r in range(d):
        head = jnp.where(row == r, prev[halo - d + r:halo - d + r + 1, :], head)
    return jnp.concatenate([head, rolled[SUBLANES:]], axis=0)


def _ffn_up_kernel(*refs, rows_per_seq, tiles_per_seq, has_prev, rb):
    if has_prev:
        (xn_ref, wa_ref, wg_ref, ca_ref, cg_ref, ba_ref, bg_ref, pa_ref, pg_ref,
         h_ref, ta_ref, tg_ref) = refs
        carry_ref = None
        prev_refs = (pa_ref, pg_ref)
    else:
        (xn_ref, wa_ref, wg_ref, ca_ref, cg_ref, ba_ref, bg_ref,
         h_ref, ta_ref, tg_ref, carry_ref) = refs
        prev_refs = (None, None)
    i = pl.program_id(0)
    f = pl.program_id(1)
    tm = xn_ref.shape[0]
    halo = CONV_W - 1
    seg = min(rb, rows_per_seq)
    conv_refs = ((ca_ref, ba_ref, ta_ref), (cg_ref, bg_ref, tg_ref))

    tails = [None, None]
    if not has_prev:
        @pl.when((i % tiles_per_seq) == 0)
        def _():
            carry_ref[f] = jnp.zeros(carry_ref.shape[1:], F32)

        tails = [carry_ref[f, 0, 0:halo, :], carry_ref[f, 1, 0:halo, :]]

    wa, wg = wa_ref[...], wg_ref[...]
    for j in range(tm // rb):
        xs = xn_ref[j * rb:(j + 1) * rb, :]
        us = (_dot(xs, wa), _dot(xs, wg))
        for s in range(rb // seg):
            r0 = j * rb + s * seg
            zs = []
            for which, (c_ref, b_ref, t_ref) in enumerate(conv_refs):
                u = us[which][s * seg:(s + 1) * seg]
                prev = prev_refs[which][r0 // rows_per_seq] if has_prev else tails[which]
                z = b_ref[...] + c_ref[halo:halo + 1, :] * u
                for d in range(1, CONV_W):
                    z = z + c_ref[halo - d:halo - d + 1, :] * _shift_rows(u, d, prev)
                tail = u[seg - halo:seg]
                if has_prev:
                    t_ref[r0 // rows_per_seq] = tail
                else:
                    tails[which] = tail
                zs.append(z)
            h_ref[r0:r0 + seg, :] = (_silu(zs[1]) * zs[0]).astype(h_ref.dtype)
    if not has_prev:
        for which, (_, _, t_ref) in enumerate(conv_refs):
            t_ref[0] = tails[which]
            carry_ref[f, which, 0:halo, :] = tails[which]


def ffn_up(xn, w_up, w_conv, b_conv, prev, *, nseq, seq_len, tf):
    rows, d = xn.shape
    ff = w_up.shape[1] // 2
    nf = ff // tf
    has_prev = prev is not None
    halo = CONV_W - 1
    if has_prev:
        tm, tiles_per_seq, n_tiles, S = rows, 1, 1, nseq
    else:
        tm = min(1024, seq_len)
        tiles_per_seq = seq_len // tm
        n_tiles, S = nseq * tiles_per_seq, 1
    rb = min(FFN_ROW_BLOCK, tm)
    assert seq_len % min(rb, seq_len) == 0 and rb % min(rb, seq_len) == 0
    in_specs = [
        pl.BlockSpec((tm, d), lambda i, f: (i, 0)),
        pl.BlockSpec((d, tf), lambda i, f: (0, f)),
        pl.BlockSpec((d, tf), lambda i, f: (0, nf + f)),
        pl.BlockSpec((CONV_W, tf), lambda i, f: (0, f)),
        pl.BlockSpec((CONV_W, tf), lambda i, f: (0, nf + f)),
        pl.BlockSpec((1, tf), lambda i, f: (0, f)),
        pl.BlockSpec((1, tf), lambda i, f: (0, nf + f)),
    ]
    args = [xn, w_up, w_up, w_conv, w_conv, b_conv, b_conv]
    scratch = []
    if has_prev:
        in_specs += [pl.BlockSpec((S, halo, tf), lambda i, f: (0, 0, f)),
                     pl.BlockSpec((S, halo, tf), lambda i, f: (0, 0, nf + f))]
        args += [prev, prev]
    else:
        scratch.append(pltpu.VMEM((nf, 2, SUBLANES, tf), F32))
    est = 2 * tm * d * 2 + 2 * 2 * d * tf * 2 + 2 * tm * tf * 2 + 12 * rb * tf * 4
    kern = functools.partial(_ffn_up_kernel, rows_per_seq=seq_len, tiles_per_seq=tiles_per_seq,
                             has_prev=has_prev, rb=rb)
    h, tail_a, tail_g = pl.pallas_call(
        kern,
        grid=(n_tiles, nf),
        in_specs=in_specs,
        out_specs=[pl.BlockSpec((tm, tf), lambda i, f: (i, f)),
                   pl.BlockSpec((S, halo, tf), lambda i, f: (i, 0, f)),
                   pl.BlockSpec((S, halo, tf), lambda i, f: (i, 0, f))],
        out_shape=[jax.ShapeDtypeStruct((rows, ff), BF16),
                   jax.ShapeDtypeStruct((n_tiles * S, halo, ff), F32),
                   jax.ShapeDtypeStruct((n_tiles * S, halo, ff), F32)],
        scratch_shapes=scratch,
        compiler_params=_params(("arbitrary", "arbitrary"), est),
        name="ffn_up",
    )(*args)
    tails = jnp.concatenate([tail_a, tail_g], axis=-1)
    if not has_prev:
        tails = tails[tiles_per_seq - 1::tiles_per_seq]
    return h, tails


def _gla_tables():
    C = CHUNK
    t = np.arange(C)[:, None]
    s = np.arange(C)[None, :]
    mats = [(s <= t)]
    masks = [(s == t)]
    m = C
    while m >= 2:
        mid = (t // m) * m + m // 2
        mats.append(((s > mid) & (s <= t)) | ((s > t) & (s <= mid)))
        masks.append((t // m == s // m) & (t % m >= m // 2) & (s % m < m // 2))
        m //= 2
    mats.append(s > t)
    return (np.concatenate(mats, axis=0).astype(np.float32),
            np.stack(masks, axis=0).astype(np.float32))


def _gla_kernel(*refs, heads, dk, dv, n_chunks, zero_init):
    if zero_init:
        (q_ref, k_ref, v_ref, r_ref, lr_ref, wg_ref, bg_ref, gn_ref, tab_ref, msk_ref,
         o_ref, sout_ref, st_ref, la_ref) = refs
        s0_ref = None
    else:
        (q_ref, k_ref, v_ref, r_ref, lr_ref, wg_ref, bg_ref, gn_ref, tab_ref, msk_ref, s0_ref,
         o_ref, sout_ref, st_ref, la_ref) = refs
    t = pl.program_id(2)
    n_tiles = pl.num_programs(2)
    C = CHUNK
    n_levels = msk_ref.shape[0]

    @pl.when(t == 0)
    def _():
        for h in range(heads):
            if zero_init:
                st_ref[h] = jnp.zeros((dv, dk), F32)
            else:
                st_ref[h] = s0_ref[0, h].T

    z = _dot(lr_ref[...].astype(BF16), wg_ref[...]) + bg_ref[...]
    la_ref[...] = _log_sigmoid(z) * (1.0 / GLA_TAU)
    tab = tab_ref[...]

    def chunk(c, carry):
        r0 = pl.multiple_of(c * C, C)
        la = la_ref[pl.ds(r0, C), :]
        sums = None
        for part in _split_bf16(la, 2):
            p = _dot(tab, part)
            sums = p if sums is None else sums + p
        decay = jnp.exp(sums)
        for h in range(heads):
            ks = slice(h * dk, (h + 1) * dk)
            vs = slice(h * dv, (h + 1) * dv)
            q = q_ref[pl.ds(r0, C), ks].astype(F32)
            k = k_ref[pl.ds(r0, C), ks].astype(F32)
            v = v_ref[pl.ds(r0, C), vs]
            att = msk_ref[0] * _nt(q.astype(BF16), k.astype(BF16))
            for lvl in range(1, n_levels):
                e = decay[lvl * C:(lvl + 1) * C, ks]
                att = att + msk_ref[lvl] * _nt((q * e).astype(BF16), (k * e).astype(BF16))
            e_in = decay[0:C, ks]
            st = st_ref[h]
            o = _dot(att.astype(BF16), v) + _nt((q * e_in).astype(BF16), st.astype(BF16))
            k_end = (k * decay[n_levels * C:(n_levels + 1) * C, ks]).astype(BF16)
            st_ref[h] = st * e_in[C - 1:C, :] + _tn(v, k_end)
            on = o * lax.rsqrt(jnp.mean(o * o, axis=-1, keepdims=True) + EPS) * gn_ref[...]
            gate = _silu(r_ref[pl.ds(r0, C), vs].astype(F32))
            o_ref[pl.ds(r0, C), vs] = (on * gate).astype(o_ref.dtype)
        return carry

    lax.fori_loop(0, n_chunks, chunk, 0)

    @pl.when(t == n_tiles - 1)
    def _():
        for h in range(heads):
            sout_ref[0, h] = st_ref[h].T


def gla(proj, lr, wg, bg, gn, s0, *, nseq, seq_len, n_heads, dk, dv):
    rows = proj.shape[0]
    hg = min(4, n_heads)
    n_groups = n_heads // hg
    tt = min(512, seq_len)
    n_tiles = seq_len // tt
    tab, msk = _gla_tables()
    tab = jnp.asarray(tab, BF16)
    msk = jnp.asarray(msk, F32)
    qw, vw = hg * dk, hg * dv
    k_off = (n_heads * dk) // qw
    v_off = (2 * n_heads * dk) // vw
    r_off = (2 * n_heads * dk + n_heads * dv) // vw
    row = lambda b, g, t: b * n_tiles + t
    in_specs = [
        pl.BlockSpec((tt, qw), lambda b, g, t: (row(b, g, t), g)),
        pl.BlockSpec((tt, qw), lambda b, g, t: (row(b, g, t), k_off + g)),
        pl.BlockSpec((tt, vw), lambda b, g, t: (row(b, g, t), v_off + g)),
        pl.BlockSpec((tt, vw), lambda b, g, t: (row(b, g, t), r_off + g)),
        pl.BlockSpec((tt, LANES), lambda b, g, t: (row(b, g, t), 0)),
        pl.BlockSpec((LANES, qw), lambda b, g, t: (0, g)),
        pl.BlockSpec((1, qw), lambda b, g, t: (0, g)),
        pl.BlockSpec((1, dv), lambda b, g, t: (0, 0)),
        pl.BlockSpec(tab.shape, lambda b, g, t: (0, 0)),
        pl.BlockSpec(msk.shape, lambda b, g, t: (0, 0, 0)),
    ]
    args = [proj, proj, proj, proj, lr, wg, bg, gn, tab, msk]
    zero_init = s0 is None
    if not zero_init:
        in_specs.append(pl.BlockSpec((1, hg, dk, dv), lambda b, g, t: (b, g, 0, 0)))
        args.append(s0)
    est = (2 * tt * (2 * qw + 2 * vw) * 2 + 2 * tt * LANES * 4 + 2 * tt * vw * 2
           + 4 * hg * dk * dv * 4 + hg * dk * dv * 4 + tt * qw * 4 + 4 * tab.shape[0] * qw * 4)
    kern = functools.partial(_gla_kernel, heads=hg, dk=dk, dv=dv, n_chunks=tt // CHUNK, zero_init=zero_init)
    o, s_out = pl.pallas_call(
        kern,
        grid=(nseq, n_groups, n_tiles),
        in_specs=in_specs,
        out_specs=[pl.BlockSpec((tt, vw), lambda b, g, t: (row(b, g, t), g)),
                   pl.BlockSpec((1, hg, dk, dv), lambda b, g, t: (b, g, 0, 0))],
        out_shape=[jax.ShapeDtypeStruct((rows, n_heads * dv), BF16),
                   jax.ShapeDtypeStruct((nseq, n_heads, dk, dv), F32)],
        scratch_shapes=[pltpu.VMEM((hg, dv, dk), F32), pltpu.VMEM((tt, qw), F32)],
        compiler_params=_params(("parallel", "parallel", "arbitrary"), est),
        name="gla",
    )(*args)
    return o, s_out


def _mem_attn_kernel(q_ref, k_ref, v_ref, o_ref, *, n_heads):
    for h in range(n_heads):
        cs = slice(h * MEM_DH, (h + 1) * MEM_DH)
        s = _nt(q_ref[:, cs], k_ref[0, :, cs].astype(BF16))
        p = jnp.exp(s - jnp.max(s, axis=-1, keepdims=True))
        l = jnp.sum(p, axis=-1, keepdims=True)
        o = _dot(p.astype(BF16), v_ref[0, :, cs].astype(BF16))
        o_ref[:, cs] = (o / l).astype(o_ref.dtype)


def mem_attn(qsrc, q_col_block, mk, mv, *, seq_len):
    rows = qsrc.shape[0]
    nseq, n_mem, mq = mk.shape
    tm = min(1024, seq_len)
    tiles_per_seq = seq_len // tm
    est = 2 * tm * mq * 2 * 2 + 2 * 2 * n_mem * mq * 4 + 6 * tm * n_mem * 4
    return pl.pallas_call(
        functools.partial(_mem_attn_kernel, n_heads=mq // MEM_DH),
        grid=(rows // tm,),
        in_specs=[pl.BlockSpec((tm, mq), lambda i: (i, q_col_block)),
                  pl.BlockSpec((1, n_mem, mq), lambda i: (i // tiles_per_seq, 0, 0)),
                  pl.BlockSpec((1, n_mem, mq), lambda i: (i // tiles_per_seq, 0, 0))],
        out_specs=pl.BlockSpec((tm, mq), lambda i: (i, 0)),
        out_shape=jax.ShapeDtypeStruct((rows, mq), BF16),
        compiler_params=_params(("parallel",), est),
        name="mem_attn",
    )(qsrc, mk, mv)


def _cumsum_kernel(lf_ref, c_ref, ct_ref, crow_ref, ccol_ref):
    j = pl.program_id(1)
    tb = lf_ref.shape[1]

    @pl.when(j == 0)
    def _():
        crow_ref[...] = jnp.zeros_like(crow_ref)
        ccol_ref[...] = jnp.zeros_like(ccol_ref)

    r = lax.broadcasted_iota(jnp.int32, (tb, tb), 0)
    s = lax.broadcasted_iota(jnp.int32, (tb, tb), 1)
    lower = jnp.where(s <= r, 1.0, 0.0).astype(BF16)
    upper = jnp.where(r <= s, 1.0, 0.0).astype(BF16)
    c = crow_ref[0:1, :]
    ct = ccol_ref[:, 0:1]
    for part in _split_bf16(lf_ref[0], 3):
        c = c + _dot(lower, part)
        ct = ct + _tn(part, upper)
    c_ref[0] = c
    ct_ref[0] = ct
    crow_ref[...] = jnp.broadcast_to(c[tb - 1:tb, :], crow_ref.shape)
    ccol_ref[...] = jnp.broadcast_to(ct[:, tb - 1:tb], ccol_ref.shape)


def cumsum_time(lf):
    nseq, L, w = lf.shape
    tb = L if L <= 2048 else 512
    est = 4 * tb * w * 4 + 4 * tb * tb * 4 + 8 * tb * w * 4
    return pl.pallas_call(
        _cumsum_kernel,
        grid=(nseq, L // tb),
        in_specs=[pl.BlockSpec((1, tb, w), lambda b, j: (b, j, 0))],
        out_specs=[pl.BlockSpec((1, tb, w), lambda b, j: (b, j, 0)),
                   pl.BlockSpec((1, w, tb), lambda b, j: (b, 0, j))],
        out_shape=[jax.ShapeDtypeStruct((nseq, L, w), F32), jax.ShapeDtypeStruct((nseq, w, L), F32)],
        scratch_shapes=[pltpu.VMEM((SUBLANES, w), F32), pltpu.VMEM((w, LANES), F32)],
        compiler_params=_params(("parallel", "arbitrary"), est),
        name="cumsum_time",
    )(lf)


def _pick_lane(x, lane):
    ids = lax.broadcasted_iota(jnp.int32, x.shape, 1)
    return jnp.sum(jnp.where(ids == lane, x, 0.0), axis=-1, keepdims=True)


LOG2E = 1.4426950408889634
FOX_Q_BLOCK = 1024
FOX_ROW_CHAIN = 512
FOX_K_BLOCK = 1024
FOX_AUG = 3


def _fox_kernel(q_ref, k_ref, v_ref, c_ref, o_ref, augq_ref, augk_ref, vaug_ref, *chain_refs, tk, tkf):
    h = pl.program_id(1)
    qi = pl.program_id(2)
    seq_len, dh = k_ref.shape
    tq = q_ref.shape[0]
    n_sub = tq // tk
    chains = [chain_refs[3 * i:3 * i + 3] for i in range(n_sub)]

    @pl.when(qi == 0)
    def _():
        def build(r, carry):
            r0 = pl.multiple_of(r * tk, tk)
            cc = _pick_lane(c_ref[0, pl.ds(r0, tk), :], h) * LOG2E
            lane = lax.broadcasted_iota(jnp.int32, (tk, dh), 1)
            aq = jnp.where(lane < 2 * FOX_AUG, 1.0, 0.0)
            ak = aq
            for n, part in enumerate(_split_bf16(cc, FOX_AUG)):
                aq = jnp.where(lane == n, part.astype(F32), aq)
                ak = jnp.where(lane == FOX_AUG + n, -part.astype(F32), ak)
            augq_ref[pl.ds(r0, tk), :] = aq.astype(BF16)
            augk_ref[pl.ds(r0, tk), :] = ak.astype(BF16)
            vaug_ref[pl.ds(r0, tk), 0:dh] = v_ref[pl.ds(r0, tk), :]
            vaug_ref[pl.ds(r0, tk), dh:2 * dh] = jnp.ones((tk, dh), BF16)
            return carry

        lax.fori_loop(0, seq_len // tk, build, 0)

    q0 = pl.multiple_of(qi * tq, tq)
    for sub, (qa_ref, m_ref, acc_ref) in enumerate(chains):
        qa_ref[:, 0:dh] = q_ref[sub * tk:(sub + 1) * tk, :]
        qa_ref[:, dh:2 * dh] = augq_ref[pl.ds(pl.multiple_of(q0 + sub * tk, tk), tk), :]
        m_ref[...] = jnp.full_like(m_ref, -jnp.inf)
        acc_ref[...] = jnp.zeros_like(acc_ref)

    def scores(sub, k0, width):
        ka = jnp.concatenate([k_ref[pl.ds(k0, width), :], augk_ref[pl.ds(k0, width), :]], axis=1)
        return _nt(chains[sub][0][...], ka)

    def softmax_update(sub, s, masked):
        m_ref = chains[sub][1]
        if masked:
            row = lax.broadcasted_iota(jnp.int32, s.shape, 0)
            col = lax.broadcasted_iota(jnp.int32, s.shape, 1)
            s = jnp.where(col <= row, s, -jnp.inf)
        blocks = [s[:, j * LANES:(j + 1) * LANES] for j in range(s.shape[1] // LANES)]
        mx = blocks[0]
        for blk in blocks[1:]:
            mx = jnp.maximum(mx, blk)
        m_old = m_ref[...]
        m_new = jnp.maximum(m_old, jnp.max(mx, axis=-1, keepdims=True))
        m_ref[...] = m_new
        alpha = jnp.exp2(m_old - m_new)
        pr = jnp.concatenate([jnp.exp2(blk - m_new) for blk in blocks], axis=1)
        return alpha, pr.astype(BF16)

    def accumulate(sub, k0, width, alpha, pr):
        acc_ref = chains[sub][2]
        acc_ref[...] = (jnp.concatenate([alpha] * (2 * dh // LANES), axis=1) * acc_ref[...]
                        + _dot(pr, vaug_ref[pl.ds(k0, width), :]))

    def step(jobs, width):
        ss = [scores(sub, k0, width) for sub, k0, _ in jobs]
        ps = [softmax_update(sub, s, masked) for (sub, _, masked), s in zip(jobs, ss)]
        for (sub, k0, _), (alpha, pr) in zip(jobs, ps):
            accumulate(sub, k0, width, alpha, pr)

    def full_chunk(j, carry):
        k0 = pl.multiple_of(j * tkf, tkf)
        step([(sub, k0, False) for sub in range(n_sub)], tkf)
        return carry

    lax.fori_loop(0, qi * (tq // tkf), full_chunk, 0)
    step([(sub, pl.multiple_of(q0 + d * tk, tk), sub == d) for d in range(n_sub) for sub in range(d, n_sub)], tk)
    for sub, (_, _, acc_ref) in enumerate(chains):
        o_ref[sub * tk:(sub + 1) * tk, :] = (acc_ref[:, 0:dh] / acc_ref[:, dh:2 * dh]).astype(o_ref.dtype)


def fox_prompt(qsrc, kbf, vbf, c, *, nseq, seq_len, n_heads, dh):
    rows = qsrc.shape[0]
    tq = min(FOX_Q_BLOCK, seq_len)
    tk = min(FOX_ROW_CHAIN, tq)
    tkf = min(FOX_K_BLOCK, tq)
    nq = seq_len // tq
    est = (2 * tq * dh * 2 * 2 + 2 * 2 * seq_len * dh * 2 + 2 * seq_len * LANES * 4 + 4 * seq_len * dh * 2
           + tq * 2 * dh * 2 + 3 * tq * LANES * 4 + (tq // tk) * 4 * tk * tkf * 4)
    return pl.pallas_call(
        functools.partial(_fox_kernel, tk=tk, tkf=tkf),
        grid=(nseq, n_heads, nq),
        in_specs=[
            pl.BlockSpec((tq, dh), lambda b, h, qi: (b * nq + qi, h)),
            pl.BlockSpec((seq_len, dh), lambda b, h, qi: (b, h)),
            pl.BlockSpec((seq_len, dh), lambda b, h, qi: (b, h)),
            pl.BlockSpec((1, seq_len, LANES), lambda b, h, qi: (b, 0, 0)),
        ],
        out_specs=pl.BlockSpec((tq, dh), lambda b, h, qi: (b * nq + qi, h)),
        out_shape=jax.ShapeDtypeStruct((rows, n_heads * dh), BF16),
        scratch_shapes=[pltpu.VMEM((seq_len, dh), BF16), pltpu.VMEM((seq_len, dh), BF16),
                        pltpu.VMEM((seq_len, 2 * dh), BF16)]
        + [pltpu.VMEM((tk, 2 * dh), BF16), pltpu.VMEM((tk, LANES), F32),
           pltpu.VMEM((tk, 2 * dh), F32)] * (tq // tk),
        compiler_params=_params(("parallel", "parallel", "arbitrary"), est),
        name="fox_prompt",
    )(qsrc, kbf, vbf, c)


def _fox_decode_kernel(q_ref, kc_ref, vc_ref, kn_ref, vn_ref, c_ref, ct_ref, o_ref):
    h = pl.program_id(1)
    ts = q_ref.shape[0]
    past = kc_ref.shape[1]
    q = q_ref[...]
    cq = _pick_lane(c_ref[0], h) * LOG2E
    ck = ct_ref[0, 0] * LOG2E
    s1 = _nt(q, kc_ref[0].astype(BF16)) + cq - ck[:, 0:past]
    s2 = _nt(q, kn_ref[...]) + cq - ck[:, past:past + ts]
    row = lax.broadcasted_iota(jnp.int32, (ts, ts), 0)
    col = lax.broadcasted_iota(jnp.int32, (ts, ts), 1)
    s2 = jnp.where(col <= row, s2, -jnp.inf)
    m = jnp.maximum(jnp.max(s1, axis=-1, keepdims=True), jnp.max(s2, axis=-1, keepdims=True))
    p1 = jnp.exp2(s1 - m)
    p2 = jnp.exp2(s2 - m)
    l = jnp.sum(p1, axis=-1, keepdims=True) + jnp.sum(p2, axis=-1, keepdims=True)
    o = _dot(p1.astype(BF16), vc_ref[0].astype(BF16)) + _dot(p2.astype(BF16), vn_ref[...])
    o_ref[...] = (o / l).astype(o_ref.dtype)


def fox_decode(qsrc, kcache, vcache, kbf, vbf, c, ct4, *, n_heads, dh):
    nseq, past, _ = kcache.shape
    rows = qsrc.shape[0]
    ts = rows // nseq
    tot = past + ts
    est = 2 * 2 * past * dh * 4 + 8 * ts * dh * 2 + 2 * ts * LANES * 4 + 8 * ts * tot * 4 + 2 * past * dh * 2
    return pl.pallas_call(
        _fox_decode_kernel,
        grid=(nseq, n_heads),
        in_specs=[
            pl.BlockSpec((ts, dh), lambda b, h: (b, h)),
            pl.BlockSpec((1, past, dh), lambda b, h: (b, 0, h)),
            pl.BlockSpec((1, past, dh), lambda b, h: (b, 0, h)),
            pl.BlockSpec((ts, dh), lambda b, h: (b, h)),
            pl.BlockSpec((ts, dh), lambda b, h: (b, h)),
            pl.BlockSpec((1, ts, LANES), lambda b, h: (b, past // ts, 0)),
            pl.BlockSpec((1, 1, 1, tot), lambda b, h: (b, h, 0, 0)),
        ],
        out_specs=pl.BlockSpec((ts, dh), lambda b, h: (b, h)),
        out_shape=jax.ShapeDtypeStruct((rows, n_heads * dh), BF16),
        compiler_params=_params(("parallel", "parallel"), est),
        name="fox_decode",
    )(qsrc, kcache, vcache, kbf, vbf, c, ct4)


def _pad_cols(x, width):
    return jnp.pad(x, ((0, 0), (0, width - x.shape[1])))


def _ffn_block(x, g, w_up, w_conv, b_conv, w_down, prev, *, nseq, seq_len):
    (xn,) = rmsnorm(x, g[None], BF16)
    ff = w_down.shape[0]
    tf = 256 if ff % 256 == 0 else LANES
    h, tails = ffn_up(xn, w_up, w_conv, b_conv, prev, nseq=nseq, seq_len=seq_len, tf=tf)
    (x,) = matmul([h], w_down, [F32], tm=512, tn=512, res=x, name="ffn_down")
    return x, tails


def kernel(x_prompt, x_sample, state_gla, cache_fox_k, cache_fox_v, cache_fox_logf, cache_mem_k, cache_mem_v,
           state_ffn_conv, mem_prompt, g_attn, g_ffn, g_mem, w_mem_kv, w_in_a, w_gate_a, b_gate_a, g_gla,
           w_in_b, g_kv, w_kv, b_forget, w_out, w_up, w_conv, b_conv, w_down, g_final):
    n_p, t_p, d = x_prompt.shape
    n_s, t_s, _ = x_sample.shape
    depth = g_attn.shape[0]
    n_a = state_gla.shape[0]
    _, _, gla_heads, dk, dv = state_gla.shape
    _, past, fox_heads, dh = cache_fox_k.shape
    n_mem, mq = cache_mem_k.shape[2:]
    a_q, a_v, b_q = gla_heads * dk, gla_heads * dv, fox_heads * dh

    streams = [
        dict(x=x_prompt.reshape(n_p * t_p, d), nseq=n_p, seq_len=t_p),
        dict(x=x_sample.reshape(n_s * t_s, d), nseq=n_s, seq_len=t_s),
    ]
    mem_rows = mem_prompt.reshape(n_p * n_mem, d)
    mk_out, mv_out, conv_out = [], [], [[], []]
    gla_out = [[], []]
    fox = [None, None]

    for l in range(depth):
        (mem_n,) = rmsnorm(mem_rows, g_mem[l][None], BF16)
        (mkv,) = matmul([mem_n], w_mem_kv[l].astype(BF16), [F32], tm=512, tn=1024, name="mem_kv")
        mk_p = mkv[:, :mq].reshape(n_p, n_mem, mq)
        mv_p = mkv[:, mq:].reshape(n_p, n_mem, mq)
        mk_out.append(mk_p)
        mv_out.append(mv_p)
        mem_kv = [(mk_p, mv_p), (cache_mem_k[l], cache_mem_v[l])]
        w_o = w_out[l].astype(BF16)

        if l < n_a:
            w = w_in_a[l]
            main = 2 * a_q + 2 * a_v
            w_main = jnp.concatenate([w[:, :main], w[:, main + GLA_RANK:]], axis=1).astype(BF16)
            w_lr = _pad_cols(w[:, main:main + GLA_RANK], LANES).astype(BF16)
            col_scale = jnp.concatenate([
                jnp.full((a_q,), dk ** -0.5, F32),
                jnp.ones((a_q + 2 * a_v,), F32),
                jnp.full((mq,), MEM_DH ** -0.5, F32)])[None]
            wg = jnp.pad(w_gate_a[l], ((0, LANES - GLA_RANK), (0, 0))).astype(BF16)
            bg = b_gate_a[l][None]
            gn = g_gla[l][None]
        else:
            j = l - n_a
            w_main = w_in_b[j].astype(BF16)
            col_scale = jnp.concatenate([jnp.full((b_q,), LOG2E * dh ** -0.5, F32),
                                         jnp.full((mq,), MEM_DH ** -0.5, F32)])[None]
            if l == n_a:
                w_kv_k = w_kv[:, :b_q].astype(BF16)
                w_kv_v = w_kv[:, b_q:2 * b_q].astype(BF16)
                w_kv_f = _pad_cols(w_kv[:, 2 * b_q:], LANES).astype(BF16)
                b_f = _pad_cols(b_forget[None], LANES)

        for si, st in enumerate(streams):
            x, nseq, seq_len = st["x"], st["nseq"], st["seq_len"]
            mk, mv = mem_kv[si]
            if l < n_a:
                (xn,) = rmsnorm(x, g_attn[l][None], BF16)
                (proj,) = matmul([xn], w_main, [BF16], tm=1024, tn=1024, scale=col_scale, name="in_proj_a")
                (lr,) = matmul([xn], w_lr, [F32], tm=1024, tn=LANES, name="in_proj_a_gate")
                s0 = None if si == 0 else state_gla[l]
                o, s_new = gla(proj, lr, wg, bg, gn, s0, nseq=nseq, seq_len=seq_len,
                               n_heads=gla_heads, dk=dk, dv=dv)
                gla_out[si].append(s_new)
                mo = mem_attn(proj, (2 * a_q + 2 * a_v) // mq, mk, mv, seq_len=seq_len)
            else:
                if l == n_a:
                    xkv, xn = rmsnorm(x, jnp.stack([g_kv, g_attn[l]]), BF16)
                    k32, kbf = matmul([xkv], w_kv_k, [F32, BF16], tm=1024, tn=1024, name="k_proj")
                    v32, vbf = matmul([xkv], w_kv_v, [F32, BF16], tm=1024, tn=1024, name="v_proj")
                    (lf,) = matmul([xkv], w_kv_f, [F32], tm=1024, tn=LANES, bias=b_f, act="log_sigmoid",
                                   name="kv_proj_forget")
                    lf3 = lf.reshape(nseq, seq_len, LANES)
                    if si == 0:
                        lf_all = lf3
                    else:
                        lf_all = jnp.concatenate(
                            [jnp.pad(cache_fox_logf, ((0, 0), (0, 0), (0, LANES - fox_heads))), lf3], axis=1)
                    c, ct = cumsum_time(lf_all)
                    ct4 = ct[:, :fox_heads].reshape(nseq, fox_heads, 1, ct.shape[-1])
                    fox[si] = dict(k32=k32, v32=v32, kbf=kbf, vbf=vbf, lf=lf3[:, :, :fox_heads], c=c, ct4=ct4)
                else:
                    (xn,) = rmsnorm(x, g_attn[l][None], BF16)
                fx = fox[si]
                (proj,) = matmul([xn], w_main, [BF16], tm=1024, tn=1024, scale=col_scale, name="in_proj_b")
                kb, vb = fx["kbf"], fx["vbf"]
                if si == 0:
                    o = fox_prompt(proj, kb, vb, fx["c"], nseq=nseq, seq_len=seq_len,
                                   n_heads=fox_heads, dh=dh)
                else:
                    o = fox_decode(proj, cache_fox_k.reshape(nseq, past, b_q), cache_fox_v.reshape(nseq, past, b_q),
                                   kb, vb, fx["c"], fx["ct4"], n_heads=fox_heads, dh=dh)
                mo = mem_attn(proj, b_q // mq, mk, mv, seq_len=seq_len)
            (x,) = matmul([o, mo], w_o, [F32], tm=1024, tn=1024, res=x, name="out_proj")
            prev = None if si == 0 else state_ffn_conv[l]
            x, tails = _ffn_block(x, g_ffn[l], w_up[l].astype(BF16), w_conv[l], b_conv[l][None],
                                  w_down[l].astype(BF16), prev, nseq=nseq, seq_len=seq_len)
            conv_out[si].append(tails)
            st["x"] = x

    ys = []
    for st in streams:
        (y,) = rmsnorm(st["x"], g_final[None], F32)
        ys.append(y)
    fp, fs = fox
    return (ys[0].reshape(n_p, t_p, d), ys[1].reshape(n_s, t_s, d),
            jnp.stack(gla_out[0]), jnp.stack(gla_out[1]),
            fp["k32"].reshape(n_p, t_p, fox_heads, dh), fp["v32"].reshape(n_p, t_p, fox_heads, dh), fp["lf"],
            fs["k32"].reshape(n_s, t_s, fox_heads, dh), fs["v32"].reshape(n_s, t_s, fox_heads, dh), fs["lf"],
            jnp.stack(mk_out), jnp.stack(mv_out),
            jnp.stack(conv_out[0]), jnp.stack(conv_out[1]))
```

```python
import functools

import numpy as np
import jax
import jax.numpy as jnp
from jax import lax
from jax.experimental import pallas as pl
from jax.experimental.pallas import tpu as pltpu

F32 = jnp.float32
BF16 = jnp.bfloat16

CHUNK = 64
GLA_RANK = 16
GLA_TAU = 16.0
MEM_DH = 256
CONV_W = 3
EPS = 1e-6

LANES = 128
SUBLANES = 8
VMEM_PHYSICAL_BYTES = 64 * 1024 * 1024
VMEM_HEADROOM_BYTES = 6 * 1024 * 1024


def _vmem_limit(estimate_bytes):
    want = int(estimate_bytes * 1.25) + 4 * 1024 * 1024
    return int(min(max(want, 16 * 1024 * 1024), VMEM_PHYSICAL_BYTES - VMEM_HEADROOM_BYTES))


def _params(semantics, estimate_bytes):
    return pltpu.CompilerParams(dimension_semantics=semantics, vmem_limit_bytes=_vmem_limit(estimate_bytes))


def _pick_tile(dim, preferred, unit):
    best = None
    for t in range(unit, min(dim, preferred) + 1, unit):
        if dim % t == 0:
            best = t
    assert best is not None, (dim, preferred, unit)
    return best


def _nt(a, b):
    return lax.dot_general(a, b, (((1,), (1,)), ((), ())), preferred_element_type=F32)


def _tn(a, b):
    return lax.dot_general(a, b, (((0,), (0,)), ((), ())), preferred_element_type=F32)


def _dot(a, b):
    return jnp.dot(a, b, preferred_element_type=F32)


def _split_bf16(x, parts):
    out = []
    rem = x
    for _ in range(parts):
        hi = rem.astype(BF16)
        out.append(hi)
        rem = rem - hi.astype(F32)
    return out


def _log_sigmoid(z):
    return jnp.minimum(z, 0.0) - jnp.log(1.0 + jnp.exp(-jnp.abs(z)))


def _silu(z):
    return z * (1.0 / (1.0 + jnp.exp(-z)))


def _rmsnorm_kernel(x_ref, g_ref, *o_refs):
    x = x_ref[...]
    y = x * lax.rsqrt(jnp.mean(x * x, axis=-1, keepdims=True) + EPS)
    for i, o_ref in enumerate(o_refs):
        o_ref[...] = (y * g_ref[i:i + 1, :]).astype(o_ref.dtype)


def rmsnorm(x, gains, out_dtype):
    rows, d = x.shape
    n = gains.shape[0]
    tm = min(256, rows)
    out_bytes = jnp.dtype(out_dtype).itemsize
    est = 2 * tm * d * 4 + n * 2 * tm * d * out_bytes + 2 * tm * d * 4
    outs = pl.pallas_call(
        _rmsnorm_kernel,
        grid=(rows // tm,),
        in_specs=[pl.BlockSpec((tm, d), lambda i: (i, 0)),
                  pl.BlockSpec((n, d), lambda i: (0, 0))],
        out_specs=[pl.BlockSpec((tm, d), lambda i: (i, 0)) for _ in range(n)],
        out_shape=[jax.ShapeDtypeStruct((rows, d), out_dtype) for _ in range(n)],
        compiler_params=_params(("parallel",), est),
        name="rmsnorm",
    )(x, gains)
    return list(outs)


def _matmul_kernel(*refs, n_parts, has_scale, has_bias, has_res, act, n_out):
    a_refs = refs[:n_parts]
    w_refs = refs[n_parts:2 * n_parts]
    pos = 2 * n_parts
    scale_ref = bias_ref = res_ref = None
    if has_scale:
        scale_ref = refs[pos]; pos += 1
    if has_bias:
        bias_ref = refs[pos]; pos += 1
    if has_res:
        res_ref = refs[pos]; pos += 1
    o_refs = refs[pos:pos + n_out]

    y = _dot(a_refs[0][...], w_refs[0][...])
    for a_ref, w_ref in zip(a_refs[1:], w_refs[1:]):
        y = y + _dot(a_ref[...], w_ref[...])
    if has_scale:
        y = y * scale_ref[...]
    if has_bias:
        y = y + bias_ref[...]
    if act == "log_sigmoid":
        y = _log_sigmoid(y)
    if has_res:
        y = y + res_ref[...]
    for o_ref in o_refs:
        if len(o_ref.shape) == 3:
            dh = o_ref.shape[2]
            for hh in range(o_ref.shape[1]):
                o_ref[:, hh, :] = y[:, hh * dh:(hh + 1) * dh].astype(o_ref.dtype)
        else:
            o_ref[...] = y.astype(o_ref.dtype)


def matmul(a_parts, w, out_dtypes, *, tm, tn, scale=None, bias=None, res=None, act=None, head_dims=None,
           name="matmul"):
    rows = a_parts[0].shape[0]
    kdim, n = w.shape
    tm = _pick_tile(rows, tm, SUBLANES)
    tn = _pick_tile(n, tn, LANES)
    assert sum(a.shape[1] for a in a_parts) == kdim
    a_specs, w_specs = [], []
    start = 0
    for a in a_parts:
        kp = a.shape[1]
        assert start % kp == 0, "a part must start at a multiple of its own width in W"
        a_specs.append(pl.BlockSpec((tm, kp), lambda i, j: (i, 0)))
        w_specs.append(pl.BlockSpec((kp, tn), lambda i, j, b=start // kp: (b, j)))
        start += kp
    in_specs = a_specs + w_specs
    args = list(a_parts) + [w] * len(a_parts)
    for extra in (scale, bias):
        if extra is not None:
            in_specs.append(pl.BlockSpec((1, tn), lambda i, j: (0, j)))
            args.append(extra)
    if res is not None:
        in_specs.append(pl.BlockSpec((tm, tn), lambda i, j: (i, j)))
        args.append(res)
    out_bytes = sum(jnp.dtype(d).itemsize for d in out_dtypes)
    est = (2 * tm * kdim * 2 + 2 * kdim * tn * 2 + 2 * tm * tn * out_bytes
           + (2 * tm * tn * 4 if res is not None else 0) + 2 * tm * tn * 4)
    kern = functools.partial(
        _matmul_kernel, n_parts=len(a_parts), has_scale=scale is not None, has_bias=bias is not None,
        has_res=res is not None, act=act, n_out=len(out_dtypes))
    head_dims = head_dims or [None] * len(out_dtypes)
    out_specs, out_shape = [], []
    for dt, dh in zip(out_dtypes, head_dims):
        if dh is None:
            out_specs.append(pl.BlockSpec((tm, tn), lambda i, j: (i, j)))
            out_shape.append(jax.ShapeDtypeStruct((rows, n), dt))
        else:
            assert tn % (SUBLANES * dh) == 0 or tn == n
            out_specs.append(pl.BlockSpec((tm, tn // dh, dh), lambda i, j: (i, j, 0)))
            out_shape.append(jax.ShapeDtypeStruct((rows, n // dh, dh), dt))
    outs = pl.pallas_call(
        kern,
        grid=(rows // tm, n // tn),
        in_specs=in_specs,
        out_specs=out_specs,
        out_shape=out_shape,
        compiler_params=_params(("parallel", "parallel"), est),
        name=name,
    )(*args)
    return list(outs)


FFN_ROW_BLOCK = 256
FFN_ROW_TILE = 2048


def _shift_rows(u, d, prev):
    halo = CONV_W - 1
    rolled = pltpu.roll(u, d, 0)
    head = rolled[0:SUBLANES]
    row = lax.broadcasted_iota(jnp.int32, head.shape, 0)
    for r in range(d):
        head = jnp.where(row == r, prev[halo - d + r:halo - d + r + 1, :], head)
    return jnp.concatenate([head, rolled[SUBLANES:]], axis=0)


def _ffn_up_kernel(*refs, rows_per_seq, tiles_per_seq, has_prev, rb):
    if has_prev:
        (xn_ref, wa_ref, wg_ref, ca_ref, cg_ref, ba_ref, bg_ref, pa_ref, pg_ref,
         h_ref, ta_ref, tg_ref) = refs
        carry_ref = None
        prev_refs = (pa_ref, pg_ref)
    else:
        (xn_ref, wa_ref, wg_ref, ca_ref, cg_ref, ba_ref, bg_ref,
         h_ref, ta_ref, tg_ref, carry_ref) = refs
        prev_refs = (None, None)
    i = pl.program_id(0)
    f = pl.program_id(1)
    tm = xn_ref.shape[0]
    halo = CONV_W - 1
    seg = min(rb, rows_per_seq)
    conv_refs = ((ca_ref, ba_ref, ta_ref), (cg_ref, bg_ref, tg_ref))

    tails = [None, None]
    if not has_prev:
        @pl.when((i % tiles_per_seq) == 0)
        def _():
            carry_ref[f] = jnp.zeros(carry_ref.shape[1:], F32)

        tails = [carry_ref[f, 0, 0:halo, :], carry_ref[f, 1, 0:halo, :]]

    wa, wg = wa_ref[...], wg_ref[...]
    for j in range(tm // rb):
        xs = xn_ref[j * rb:(j + 1) * rb, :]
        us = (_dot(xs, wa), _dot(xs, wg))
        for s in range(rb // seg):
            r0 = j * rb + s * seg
            zs = []
            for which, (c_ref, b_ref, t_ref) in enumerate(conv_refs):
                u = us[which][s * seg:(s + 1) * seg]
                prev = prev_refs[which][r0 // rows_per_seq] if has_prev else tails[which]
                z = b_ref[...] + c_ref[halo:halo + 1, :] * u
                for d in range(1, CONV_W):
                    z = z + c_ref[halo - d:halo - d + 1, :] * _shift_rows(u, d, prev)
                tail = u[seg - halo:seg]
                if has_prev:
                    t_ref[r0 // rows_per_seq] = tail
                else:
                    tails[which] = tail
                zs.append(z)
            h_ref[r0:r0 + seg, :] = (_silu(zs[1]) * zs[0]).astype(h_ref.dtype)
    if not has_prev:
        for which, (_, _, t_ref) in enumerate(conv_refs):
            t_ref[0] = tails[which]
            carry_ref[f, which, 0:halo, :] = tails[which]


def ffn_up(xn, w_up, w_conv, b_conv, prev, *, nseq, seq_len, tf):
    rows, d = xn.shape
    ff = w_up.shape[1] // 2
    nf = ff // tf
    has_prev = prev is not None
    halo = CONV_W - 1
    if has_prev:
        tm, tiles_per_seq, n_tiles, S = rows, 1, 1, nseq
    else:
        tm = min(FFN_ROW_TILE, seq_len)
        tiles_per_seq = seq_len // tm
        n_tiles, S = nseq * tiles_per_seq, 1
    rb = min(FFN_ROW_BLOCK, tm)
    assert seq_len % min(rb, seq_len) == 0 and rb % min(rb, seq_len) == 0
    in_specs = [
        pl.BlockSpec((tm, d), lambda i, f: (i, 0)),
        pl.BlockSpec((d, tf), lambda i, f: (0, f)),
        pl.BlockSpec((d, tf), lambda i, f: (0, nf + f)),
        pl.BlockSpec((CONV_W, tf), lambda i, f: (0, f)),
        pl.BlockSpec((CONV_W, tf), lambda i, f: (0, nf + f)),
        pl.BlockSpec((1, tf), lambda i, f: (0, f)),
        pl.BlockSpec((1, tf), lambda i, f: (0, nf + f)),
    ]
    args = [xn, w_up, w_up, w_conv, w_conv, b_conv, b_conv]
    scratch = []
    if has_prev:
        in_specs += [pl.BlockSpec((S, halo, tf), lambda i, f: (0, 0, f)),
                     pl.BlockSpec((S, halo, tf), lambda i, f: (0, 0, nf + f))]
        args += [prev, prev]
    else:
        scratch.append(pltpu.VMEM((nf, 2, SUBLANES, tf), F32))
    est = 2 * tm * d * 2 + 2 * 2 * d * tf * 2 + 2 * tm * tf * 2 + 12 * rb * tf * 4
    kern = functools.partial(_ffn_up_kernel, rows_per_seq=seq_len, tiles_per_seq=tiles_per_seq,
                             has_prev=has_prev, rb=rb)
    h, tail_a, tail_g = pl.pallas_call(
        kern,
        grid=(n_tiles, nf),
        in_specs=in_specs,
        out_specs=[pl.BlockSpec((tm, tf), lambda i, f: (i, f)),
                   pl.BlockSpec((S, halo, tf), lambda i, f: (i, 0, f)),
                   pl.BlockSpec((S, halo, tf), lambda i, f: (i, 0, f))],
        out_shape=[jax.ShapeDtypeStruct((rows, ff), BF16),
                   jax.ShapeDtypeStruct((n_tiles * S, halo, ff), F32),
                   jax.ShapeDtypeStruct((n_tiles * S, halo, ff), F32)],
        scratch_shapes=scratch,
        compiler_params=_params(("arbitrary", "arbitrary"), est),
        name="ffn_up",
    )(*args)
    tails = jnp.concatenate([tail_a, tail_g], axis=-1)
    if not has_prev:
        tails = tails[tiles_per_seq - 1::tiles_per_seq]
    return h, tails


def _gla_tables():
    C = CHUNK
    t = np.arange(C)[:, None]
    s = np.arange(C)[None, :]
    mats = [(s <= t)]
    masks = [(s == t)]
    m = C
    while m >= 2:
        mid = (t // m) * m + m // 2
        mats.append(((s > mid) & (s <= t)) | ((s > t) & (s <= mid)))
        masks.append((t // m == s // m) & (t % m >= m // 2) & (s % m < m // 2))
        m //= 2
    mats.append(s > t)
    return (np.concatenate(mats, axis=0).astype(np.float32),
            np.stack(masks, axis=0).astype(np.float32))


def _gla_kernel(*refs, heads, dk, dv, n_chunks, zero_init):
    if zero_init:
        (q_ref, k_ref, v_ref, r_ref, lr_ref, wg_ref, bg_ref, gn_ref, tab_ref, msk_ref,
         o_ref, sout_ref, st_ref, la_ref) = refs
        s0_ref = None
    else:
        (q_ref, k_ref, v_ref, r_ref, lr_ref, wg_ref, bg_ref, gn_ref, tab_ref, msk_ref, s0_ref,
         o_ref, sout_ref, st_ref, la_ref) = refs
    t = pl.program_id(2)
    n_tiles = pl.num_programs(2)
    C = CHUNK
    n_levels = msk_ref.shape[0]

    @pl.when(t == 0)
    def _():
        for h in range(heads):
            if zero_init:
                st_ref[h] = jnp.zeros((dv, dk), F32)
            else:
                st_ref[h] = s0_ref[0, h].T

    z = _dot(lr_ref[...].astype(BF16), wg_ref[...]) + bg_ref[...]
    la_ref[...] = _log_sigmoid(z) * (1.0 / GLA_TAU)
    tab = tab_ref[...]

    def chunk(c, carry):
        r0 = pl.multiple_of(c * C, C)
        la = la_ref[pl.ds(r0, C), :]
        sums = None
        for part in _split_bf16(la, 2):
            p = _dot(tab, part)
            sums = p if sums is None else sums + p
        decay = jnp.exp(sums)
        for h in range(heads):
            ks = slice(h * dk, (h + 1) * dk)
            vs = slice(h * dv, (h + 1) * dv)
            q = q_ref[pl.ds(r0, C), ks].astype(F32)
            k = k_ref[pl.ds(r0, C), ks].astype(F32)
            v = v_ref[pl.ds(r0, C), vs]
            att = msk_ref[0] * _nt(q.astype(BF16), k.astype(BF16))
            for lvl in range(1, n_levels):
                e = decay[lvl * C:(lvl + 1) * C, ks]
                att = att + msk_ref[lvl] * _nt((q * e).astype(BF16), (k * e).astype(BF16))
            e_in = decay[0:C, ks]
            st = st_ref[h]
            o = _dot(att.astype(BF16), v) + _nt((q * e_in).astype(BF16), st.astype(BF16))
            k_end = (k * decay[n_levels * C:(n_levels + 1) * C, ks]).astype(BF16)
            st_ref[h] = st * e_in[C - 1:C, :] + _tn(v, k_end)
            on = o * lax.rsqrt(jnp.mean(o * o, axis=-1, keepdims=True) + EPS) * gn_ref[...]
            gate = _silu(r_ref[pl.ds(r0, C), vs].astype(F32))
            o_ref[pl.ds(r0, C), vs] = (on * gate).astype(o_ref.dtype)
        return carry

    lax.fori_loop(0, n_chunks, chunk, 0)

    @pl.when(t == n_tiles - 1)
    def _():
        for h in range(heads):
            sout_ref[0, h] = st_ref[h].T


def gla(proj, lr, wg, bg, gn, s0, *, nseq, seq_len, n_heads, dk, dv):
    rows = proj.shape[0]
    hg = min(4, n_heads)
    n_groups = n_heads // hg
    tt = min(512, seq_len)
    n_tiles = seq_len // tt
    tab, msk = _gla_tables()
    tab = jnp.asarray(tab, BF16)
    msk = jnp.asarray(msk, F32)
    qw, vw = hg * dk, hg * dv
    k_off = (n_heads * dk) // qw
    v_off = (2 * n_heads * dk) // vw
    r_off = (2 * n_heads * dk + n_heads * dv) // vw
    row = lambda b, g, t: b * n_tiles + t
    in_specs = [
        pl.BlockSpec((tt, qw), lambda b, g, t: (row(b, g, t), g)),
        pl.BlockSpec((tt, qw), lambda b, g, t: (row(b, g, t), k_off + g)),
        pl.BlockSpec((tt, vw), lambda b, g, t: (row(b, g, t), v_off + g)),
        pl.BlockSpec((tt, vw), lambda b, g, t: (row(b, g, t), r_off + g)),
        pl.BlockSpec((tt, LANES), lambda b, g, t: (row(b, g, t), 0)),
        pl.BlockSpec((LANES, qw), lambda b, g, t: (0, g)),
        pl.BlockSpec((1, qw), lambda b, g, t: (0, g)),
        pl.BlockSpec((1, dv), lambda b, g, t: (0, 0)),
        pl.BlockSpec(tab.shape, lambda b, g, t: (0, 0)),
        pl.BlockSpec(msk.shape, lambda b, g, t: (0, 0, 0)),
    ]
    args = [proj, proj, proj, proj, lr, wg, bg, gn, tab, msk]
    zero_init = s0 is None
    if not zero_init:
        in_specs.append(pl.BlockSpec((1, hg, dk, dv), lambda b, g, t: (b, g, 0, 0)))
        args.append(s0)
    est = (2 * tt * (2 * qw + 2 * vw) * 2 + 2 * tt * LANES * 4 + 2 * tt * vw * 2
           + 4 * hg * dk * dv * 4 + hg * dk * dv * 4 + tt * qw * 4 + 4 * tab.shape[0] * qw * 4)
    kern = functools.partial(_gla_kernel, heads=hg, dk=dk, dv=dv, n_chunks=tt // CHUNK, zero_init=zero_init)
    o, s_out = pl.pallas_call(
        kern,
        grid=(nseq, n_groups, n_tiles),
        in_specs=in_specs,
        out_specs=[pl.BlockSpec((tt, vw), lambda b, g, t: (row(b, g, t), g)),
                   pl.BlockSpec((1, hg, dk, dv), lambda b, g, t: (b, g, 0, 0))],
        out_shape=[jax.ShapeDtypeStruct((rows, n_heads * dv), BF16),
                   jax.ShapeDtypeStruct((nseq, n_heads, dk, dv), F32)],
        scratch_shapes=[pltpu.VMEM((hg, dv, dk), F32), pltpu.VMEM((tt, qw), F32)],
        compiler_params=_params(("parallel", "parallel", "arbitrary"), est),
        name="gla",
    )(*args)
    return o, s_out


def _mem_attn_kernel(q_ref, k_ref, v_ref, o_ref, *, n_heads):
    for h in range(n_heads):
        cs = slice(h * MEM_DH, (h + 1) * MEM_DH)
        s = _nt(q_ref[:, cs], k_ref[0, :, cs].astype(BF16))
        p = jnp.exp(s - jnp.max(s, axis=-1, keepdims=True))
        l = jnp.sum(p, axis=-1, keepdims=True)
        o = _dot(p.astype(BF16), v_ref[0, :, cs].astype(BF16))
        o_ref[:, cs] = (o / l).astype(o_ref.dtype)


def mem_attn(qsrc, q_col_block, mk, mv, *, seq_len):
    rows = qsrc.shape[0]
    nseq, n_mem, mq = mk.shape
    tm = min(1024, seq_len)
    tiles_per_seq = seq_len // tm
    est = 2 * tm * mq * 2 * 2 + 2 * 2 * n_mem * mq * 4 + 6 * tm * n_mem * 4
    return pl.pallas_call(
        functools.partial(_mem_attn_kernel, n_heads=mq // MEM_DH),
        grid=(rows // tm,),
        in_specs=[pl.BlockSpec((tm, mq), lambda i: (i, q_col_block)),
                  pl.BlockSpec((1, n_mem, mq), lambda i: (i // tiles_per_seq, 0, 0)),
                  pl.BlockSpec((1, n_mem, mq), lambda i: (i // tiles_per_seq, 0, 0))],
        out_specs=pl.BlockSpec((tm, mq), lambda i: (i, 0)),
        out_shape=jax.ShapeDtypeStruct((rows, mq), BF16),
        compiler_params=_params(("parallel",), est),
        name="mem_attn",
    )(qsrc, mk, mv)


def _cumsum_kernel(lf_ref, c_ref, ct_ref, crow_ref, ccol_ref):
    j = pl.program_id(1)
    tb = lf_ref.shape[1]

    @pl.when(j == 0)
    def _():
        crow_ref[...] = jnp.zeros_like(crow_ref)
        ccol_ref[...] = jnp.zeros_like(ccol_ref)

    r = lax.broadcasted_iota(jnp.int32, (tb, tb), 0)
    s = lax.broadcasted_iota(jnp.int32, (tb, tb), 1)
    lower = jnp.where(s <= r, 1.0, 0.0).astype(BF16)
    upper = jnp.where(r <= s, 1.0, 0.0).astype(BF16)
    c = crow_ref[0:1, :]
    ct = ccol_ref[:, 0:1]
    for part in _split_bf16(lf_ref[0], 3):
        c = c + _dot(lower, part)
        ct = ct + _tn(part, upper)
    c_ref[0] = c
    ct_ref[0] = ct
    crow_ref[...] = jnp.broadcast_to(c[tb - 1:tb, :], crow_ref.shape)
    ccol_ref[...] = jnp.broadcast_to(ct[:, tb - 1:tb], ccol_ref.shape)


def cumsum_time(lf):
    nseq, L, w = lf.shape
    tb = L if L <= 2048 else 512
    est = 4 * tb * w * 4 + 4 * tb * tb * 4 + 8 * tb * w * 4
    return pl.pallas_call(
        _cumsum_kernel,
        grid=(nseq, L // tb),
        in_specs=[pl.BlockSpec((1, tb, w), lambda b, j: (b, j, 0))],
        out_specs=[pl.BlockSpec((1, tb, w), lambda b, j: (b, j, 0)),
                   pl.BlockSpec((1, w, tb), lambda b, j: (b, 0, j))],
        out_shape=[jax.ShapeDtypeStruct((nseq, L, w), F32), jax.ShapeDtypeStruct((nseq, w, L), F32)],
        scratch_shapes=[pltpu.VMEM((SUBLANES, w), F32), pltpu.VMEM((w, LANES), F32)],
        compiler_params=_params(("parallel", "arbitrary"), est),
        name="cumsum_time",
    )(lf)


def _pick_lane(x, lane):
    ids = lax.broadcasted_iota(jnp.int32, x.shape, 1)
    return jnp.sum(jnp.where(ids == lane, x, 0.0), axis=-1, keepdims=True)


LOG2E = 1.4426950408889634
FOX_Q_BLOCK = 1024
FOX_ROW_CHAIN = 512
FOX_K_BLOCK = 1024
FOX_AUG = 3


def _fox_kernel(q_ref, k_ref, v_ref, c_ref, o_ref, augq_ref, augk_ref, vaug_ref, *chain_refs, tk, tkf):
    h = pl.program_id(1)
    qi = pl.program_id(2)
    seq_len, dh = k_ref.shape
    tq = q_ref.shape[0]
    n_sub = tq // tk
    chains = [chain_refs[3 * i:3 * i + 3] for i in range(n_sub)]

    @pl.when(qi == 0)
    def _():
        def build(r, carry):
            r0 = pl.multiple_of(r * tk, tk)
            cc = _pick_lane(c_ref[0, pl.ds(r0, tk), :], h) * LOG2E
            lane = lax.broadcasted_iota(jnp.int32, (tk, dh), 1)
            aq = jnp.where(lane < 2 * FOX_AUG, 1.0, 0.0)
            ak = aq
            for n, part in enumerate(_split_bf16(cc, FOX_AUG)):
                aq = jnp.where(lane == n, part.astype(F32), aq)
                ak = jnp.where(lane == FOX_AUG + n, -part.astype(F32), ak)
            augq_ref[pl.ds(r0, tk), :] = aq.astype(BF16)
            augk_ref[pl.ds(r0, tk), :] = ak.astype(BF16)
            vaug_ref[pl.ds(r0, tk), 0:dh] = v_ref[pl.ds(r0, tk), :]
            vaug_ref[pl.ds(r0, tk), dh:2 * dh] = jnp.ones((tk, dh), BF16)
            return carry

        lax.fori_loop(0, seq_len // tk, build, 0)

    q0 = pl.multiple_of(qi * tq, tq)
    for sub, (qa_ref, m_ref, acc_ref) in enumerate(chains):
        qa_ref[:, 0:dh] = q_ref[sub * tk:(sub + 1) * tk, :]
        qa_ref[:, dh:2 * dh] = augq_ref[pl.ds(pl.multiple_of(q0 + sub * tk, tk), tk), :]
        m_ref[...] = jnp.full_like(m_ref, -jnp.inf)
        acc_ref[...] = jnp.zeros_like(acc_ref)

    def scores(sub, k0, width):
        ka = jnp.concatenate([k_ref[pl.ds(k0, width), :], augk_ref[pl.ds(k0, width), :]], axis=1)
        return _nt(chains[sub][0][...], ka)

    def softmax_update(sub, s, masked):
        m_ref = chains[sub][1]
        if masked:
            row = lax.broadcasted_iota(jnp.int32, s.shape, 0)
            col = lax.broadcasted_iota(jnp.int32, s.shape, 1)
            s = jnp.where(col <= row, s, -jnp.inf)
        blocks = [s[:, j * LANES:(j + 1) * LANES] for j in range(s.shape[1] // LANES)]
        mx = blocks[0]
        for blk in blocks[1:]:
            mx = jnp.maximum(mx, blk)
        m_old = m_ref[...]
        m_new = jnp.maximum(m_old, jnp.max(mx, axis=-1, keepdims=True))
        m_ref[...] = m_new
        alpha = jnp.exp2(m_old - m_new)
        pr = jnp.concatenate([jnp.exp2(blk - m_new) for blk in blocks], axis=1)
        return alpha, pr.astype(BF16)

    def accumulate(sub, k0, width, alpha, pr):
        acc_ref = chains[sub][2]
        acc_ref[...] = (jnp.concatenate([alpha] * (2 * dh // LANES), axis=1) * acc_ref[...]
                        + _dot(pr, vaug_ref[pl.ds(k0, width), :]))

    def step(jobs, width):
        ss = [scores(sub, k0, width) for sub, k0, _ in jobs]
        ps = [softmax_update(sub, s, masked) for (sub, _, masked), s in zip(jobs, ss)]
        for (sub, k0, _), (alpha, pr) in zip(jobs, ps):
            accumulate(sub, k0, width, alpha, pr)

    def full_chunk(j, carry):
        k0 = pl.multiple_of(j * tkf, tkf)
        step([(sub, k0, False) for sub in range(n_sub)], tkf)
        return carry

    lax.fori_loop(0, qi * (tq // tkf), full_chunk, 0)
    step([(sub, pl.multiple_of(q0 + d * tk, tk), sub == d) for d in range(n_sub) for sub in range(d, n_sub)], tk)
    for sub, (_, _, acc_ref) in enumerate(chains):
        o_ref[sub * tk:(sub + 1) * tk, :] = (acc_ref[:, 0:dh] / acc_ref[:, dh:2 * dh]).astype(o_ref.dtype)


def fox_prompt(qsrc, kbf, vbf, c, *, nseq, seq_len, n_heads, dh):
    rows = qsrc.shape[0]
    tq = min(FOX_Q_BLOCK, seq_len)
    tk = min(FOX_ROW_CHAIN, tq)
    tkf = min(FOX_K_BLOCK, tq)
    nq = seq_len // tq
    est = (2 * tq * dh * 2 * 2 + 2 * 2 * seq_len * dh * 2 + 2 * seq_len * LANES * 4 + 4 * seq_len * dh * 2
           + tq * 2 * dh * 2 + 3 * tq * LANES * 4 + (tq // tk) * 4 * tk * tkf * 4)
    return pl.pallas_call(
        functools.partial(_fox_kernel, tk=tk, tkf=tkf),
        grid=(nseq, n_heads, nq),
        in_specs=[
            pl.BlockSpec((tq, dh), lambda b, h, qi: (b * nq + qi, h)),
            pl.BlockSpec((seq_len, dh), lambda b, h, qi: (b, h)),
            pl.BlockSpec((seq_len, dh), lambda b, h, qi: (b, h)),
            pl.BlockSpec((1, seq_len, LANES), lambda b, h, qi: (b, 0, 0)),
        ],
        out_specs=pl.BlockSpec((tq, dh), lambda b, h, qi: (b * nq + qi, h)),
        out_shape=jax.ShapeDtypeStruct((rows, n_heads * dh), BF16),
        scratch_shapes=[pltpu.VMEM((seq_len, dh), BF16), pltpu.VMEM((seq_len, dh), BF16),
                        pltpu.VMEM((seq_len, 2 * dh), BF16)]
        + [pltpu.VMEM((tk, 2 * dh), BF16), pltpu.VMEM((tk, LANES), F32),
           pltpu.VMEM((tk, 2 * dh), F32)] * (tq // tk),
        compiler_params=_params(("parallel", "parallel", "arbitrary"), est),
        name="fox_prompt",
    )(qsrc, kbf, vbf, c)


FOX_DECODE_HEADS = 8


def _fox_decode_kernel(q_ref, kc_ref, vc_ref, kn_ref, vn_ref, c_ref, ct_ref, o_ref, *, dh):
    g = pl.program_id(1)
    ts = q_ref.shape[0]
    past = kc_ref.shape[1]
    heads = q_ref.shape[1] // dh
    row = lax.broadcasted_iota(jnp.int32, (ts, ts), 0)
    col = lax.broadcasted_iota(jnp.int32, (ts, ts), 1)
    for hh in range(heads):
        cs = slice(hh * dh, (hh + 1) * dh)
        q = q_ref[:, cs]
        cq = _pick_lane(c_ref[0], g * heads + hh) * LOG2E
        ck = ct_ref[0, hh] * LOG2E
        s1 = _nt(q, kc_ref[0, :, cs].astype(BF16)) + cq - ck[:, 0:past]
        s2 = _nt(q, kn_ref[:, cs]) + cq - ck[:, past:past + ts]
        s2 = jnp.where(col <= row, s2, -jnp.inf)
        m = jnp.maximum(jnp.max(s1, axis=-1, keepdims=True), jnp.max(s2, axis=-1, keepdims=True))
        p1 = jnp.exp2(s1 - m)
        p2 = jnp.exp2(s2 - m)
        l = jnp.sum(p1, axis=-1, keepdims=True) + jnp.sum(p2, axis=-1, keepdims=True)
        o = _dot(p1.astype(BF16), vc_ref[0, :, cs].astype(BF16)) + _dot(p2.astype(BF16), vn_ref[:, cs])
        o_ref[:, cs] = (o / l).astype(o_ref.dtype)


def fox_decode(qsrc, kcache, vcache, kbf, vbf, c, ct4, *, n_heads, dh):
    nseq, past, _ = kcache.shape
    rows = qsrc.shape[0]
    ts = rows // nseq
    tot = past + ts
    hg = FOX_DECODE_HEADS if n_heads % FOX_DECODE_HEADS == 0 else n_heads
    w = hg * dh
    est = 2 * 2 * past * w * 4 + 8 * ts * w * 2 + 2 * ts * LANES * 4 + 8 * ts * tot * 4 + 4 * past * dh * 4
    return pl.pallas_call(
        functools.partial(_fox_decode_kernel, dh=dh),
        grid=(nseq, n_heads // hg),
        in_specs=[
            pl.BlockSpec((ts, w), lambda b, g: (b, g)),
            pl.BlockSpec((1, past, w), lambda b, g: (b, 0, g)),
            pl.BlockSpec((1, past, w), lambda b, g: (b, 0, g)),
            pl.BlockSpec((ts, w), lambda b, g: (b, g)),
            pl.BlockSpec((ts, w), lambda b, g: (b, g)),
            pl.BlockSpec((1, ts, LANES), lambda b, g: (b, past // ts, 0)),
            pl.BlockSpec((1, hg, 1, tot), lambda b, g: (b, g, 0, 0)),
        ],
        out_specs=pl.BlockSpec((ts, w), lambda b, g: (b, g)),
        out_shape=jax.ShapeDtypeStruct((rows, n_heads * dh), BF16),
        compiler_params=_params(("parallel", "parallel"), est),
        name="fox_decode",
    )(qsrc, kcache, vcache, kbf, vbf, c, ct4)


def _pad_cols(x, width):
    return jnp.pad(x, ((0, 0), (0, width - x.shape[1])))


def _ffn_block(x, g, w_up, w_conv, b_conv, w_down, prev, *, nseq, seq_len):
    (xn,) = rmsnorm(x, g[None], BF16)
    ff = w_down.shape[0]
    tf = 256 if ff % 256 == 0 else LANES
    h, tails = ffn_up(xn, w_up, w_conv, b_conv, prev, nseq=nseq, seq_len=seq_len, tf=tf)
    (x,) = matmul([h], w_down, [F32], tm=512, tn=512, res=x, name="ffn_down")
    return x, tails


def kernel(x_prompt, x_sample, state_gla, cache_fox_k, cache_fox_v, cache_fox_logf, cache_mem_k, cache_mem_v,
           state_ffn_conv, mem_prompt, g_attn, g_ffn, g_mem, w_mem_kv, w_in_a, w_gate_a, b_gate_a, g_gla,
           w_in_b, g_kv, w_kv, b_forget, w_out, w_up, w_conv, b_conv, w_down, g_final):
    n_p, t_p, d = x_prompt.shape
    n_s, t_s, _ = x_sample.shape
    depth = g_attn.shape[0]
    n_a = state_gla.shape[0]
    _, _, gla_heads, dk, dv = state_gla.shape
    _, past, fox_heads, dh = cache_fox_k.shape
    n_mem, mq = cache_mem_k.shape[2:]
    a_q, a_v, b_q = gla_heads * dk, gla_heads * dv, fox_heads * dh

    streams = [
        dict(x=x_prompt.reshape(n_p * t_p, d), nseq=n_p, seq_len=t_p),
        dict(x=x_sample.reshape(n_s * t_s, d), nseq=n_s, seq_len=t_s),
    ]
    mem_rows = mem_prompt.reshape(n_p * n_mem, d)
    mk_out, mv_out, conv_out = [], [], [[], []]
    gla_out = [[], []]
    fox = [None, None]

    for l in range(depth):
        (mem_n,) = rmsnorm(mem_rows, g_mem[l][None], BF16)
        (mkv,) = matmul([mem_n], w_mem_kv[l].astype(BF16), [F32], tm=512, tn=1024, name="mem_kv")
        mk_p = mkv[:, :mq].reshape(n_p, n_mem, mq)
        mv_p = mkv[:, mq:].reshape(n_p, n_mem, mq)
        mk_out.append(mk_p)
        mv_out.append(mv_p)
        mem_kv = [(mk_p, mv_p), (cache_mem_k[l], cache_mem_v[l])]
        w_o = w_out[l].astype(BF16)

        if l < n_a:
            w = w_in_a[l]
            main = 2 * a_q + 2 * a_v
            w_main = jnp.concatenate([w[:, :main], w[:, main + GLA_RANK:]], axis=1).astype(BF16)
            w_lr = _pad_cols(w[:, main:main + GLA_RANK], LANES).astype(BF16)
            col_scale = jnp.concatenate([
                jnp.full((a_q,), dk ** -0.5, F32),
                jnp.ones((a_q + 2 * a_v,), F32),
                jnp.full((mq,), MEM_DH ** -0.5, F32)])[None]
            wg = jnp.pad(w_gate_a[l], ((0, LANES - GLA_RANK), (0, 0))).astype(BF16)
            bg = b_gate_a[l][None]
            gn = g_gla[l][None]
        else:
            j = l - n_a
            w_main = w_in_b[j].astype(BF16)
            col_scale = jnp.concatenate([jnp.full((b_q,), LOG2E * dh ** -0.5, F32),
                                         jnp.full((mq,), MEM_DH ** -0.5, F32)])[None]
            if l == n_a:
                w_kv_k = w_kv[:, :b_q].astype(BF16)
                w_kv_v = w_kv[:, b_q:2 * b_q].astype(BF16)
                w_kv_f = _pad_cols(w_kv[:, 2 * b_q:], LANES).astype(BF16)
                b_f = _pad_cols(b_forget[None], LANES)

        for si, st in enumerate(streams):
            x, nseq, seq_len = st["x"], st["nseq"], st["seq_len"]
            mk, mv = mem_kv[si]
            if l < n_a:
                (xn,) = rmsnorm(x, g_attn[l][None], BF16)
                (proj,) = matmul([xn], w_main, [BF16], tm=1024, tn=1024, scale=col_scale, name="in_proj_a")
                (lr,) = matmul([xn], w_lr, [F32], tm=1024, tn=LANES, name="in_proj_a_gate")
                s0 = None if si == 0 else state_gla[l]
                o, s_new = gla(proj, lr, wg, bg, gn, s0, nseq=nseq, seq_len=seq_len,
                               n_heads=gla_heads, dk=dk, dv=dv)
                gla_out[si].append(s_new)
                mo = mem_attn(proj, (2 * a_q + 2 * a_v) // mq, mk, mv, seq_len=seq_len)
            else:
                if l == n_a:
                    xkv, xn = rmsnorm(x, jnp.stack([g_kv, g_attn[l]]), BF16)
                    k32, kbf = matmul([xkv], w_kv_k, [F32, BF16], tm=1024, tn=1024, head_dims=[dh, None],
                                      name="k_proj")
                    v32, vbf = matmul([xkv], w_kv_v, [F32, BF16], tm=1024, tn=1024, head_dims=[dh, None],
                                      name="v_proj")
                    (lf,) = matmul([xkv], w_kv_f, [F32], tm=1024, tn=LANES, bias=b_f, act="log_sigmoid",
                                   name="kv_proj_forget")
                    lf3 = lf.reshape(nseq, seq_len, LANES)
                    if si == 0:
                        lf_all = lf3
                    else:
                        lf_all = jnp.concatenate(
                            [jnp.pad(cache_fox_logf, ((0, 0), (0, 0), (0, LANES - fox_heads))), lf3], axis=1)
                    c, ct = cumsum_time(lf_all)
                    ct4 = ct[:, :fox_heads].reshape(nseq, fox_heads, 1, ct.shape[-1])
                    fox[si] = dict(k32=k32, v32=v32, kbf=kbf, vbf=vbf, lf=lf3[:, :, :fox_heads], c=c, ct4=ct4)
                else:
                    (xn,) = rmsnorm(x, g_attn[l][None], BF16)
                fx = fox[si]
                (proj,) = matmul([xn], w_main, [BF16], tm=1024, tn=1024, scale=col_scale, name="in_proj_b")
                kb, vb = fx["kbf"], fx["vbf"]
                if si == 0:
                    o = fox_prompt(proj, kb, vb, fx["c"], nseq=nseq, seq_len=seq_len,
                                   n_heads=fox_heads, dh=dh)
                else:
                    o = fox_decode(proj, cache_fox_k.reshape(nseq, past, b_q), cache_fox_v.reshape(nseq, past, b_q),
                                   kb, vb, fx["c"], fx["ct4"], n_heads=fox_heads, dh=dh)
                mo = mem_attn(proj, b_q // mq, mk, mv, seq_len=seq_len)
            (x,) = matmul([o, mo], w_o, [F32], tm=1024, tn=1024, res=x, name="out_proj")
            prev = None if si == 0 else state_ffn_conv[l]
            x, tails = _ffn_block(x, g_ffn[l], w_up[l].astype(BF16), w_conv[l], b_conv[l][None],
                                  w_down[l].astype(BF16), prev, nseq=nseq, seq_len=seq_len)
            conv_out[si].append(tails)
            st["x"] = x

    ys = []
    for st in streams:
        (y,) = rmsnorm(st["x"], g_final[None], F32)
        ys.append(y)
    fp, fs = fox
    return (ys[0].reshape(n_p, t_p, d), ys[1].reshape(n_s, t_s, d),
            jnp.stack(gla_out[0]), jnp.stack(gla_out[1]),
            fp["k32"].reshape(n_p, t_p, fox_heads, dh), fp["v32"].reshape(n_p, t_p, fox_heads, dh), fp["lf"],
            fs["k32"].reshape(n_s, t_s, fox_heads, dh), fs["v32"].reshape(n_s, t_s, fox_heads, dh), fs["lf"],
            jnp.stack(mk_out), jnp.stack(mv_out),
            jnp.stack(conv_out[0]), jnp.stack(conv_out[1]))
```

```python
import functools

import numpy as np
import jax
import jax.numpy as jnp
from jax import lax
from jax.experimental import pallas as pl
from jax.experimental.pallas import tpu as pltpu

F32 = jnp.float32
BF16 = jnp.bfloat16

CHUNK = 64
GLA_RANK = 16
GLA_TAU = 16.0
MEM_DH = 256
CONV_W = 3
EPS = 1e-6

LANES = 128
SUBLANES = 8
VMEM_PHYSICAL_BYTES = 64 * 1024 * 1024
VMEM_HEADROOM_BYTES = 6 * 1024 * 1024


def _vmem_limit(estimate_bytes):
    want = int(estimate_bytes * 1.25) + 4 * 1024 * 1024
    return int(min(max(want, 16 * 1024 * 1024), VMEM_PHYSICAL_BYTES - VMEM_HEADROOM_BYTES))


def _params(semantics, estimate_bytes):
    return pltpu.CompilerParams(dimension_semantics=semantics, vmem_limit_bytes=_vmem_limit(estimate_bytes))


def _pick_tile(dim, preferred, unit):
    best = None
    for t in range(unit, min(dim, preferred) + 1, unit):
        if dim % t == 0:
            best = t
    assert best is not None, (dim, preferred, unit)
    return best


def _nt(a, b):
    return lax.dot_general(a, b, (((1,), (1,)), ((), ())), preferred_element_type=F32)


def _tn(a, b):
    return lax.dot_general(a, b, (((0,), (0,)), ((), ())), preferred_element_type=F32)


def _dot(a, b):
    return jnp.dot(a, b, preferred_element_type=F32)


def _split_bf16(x, parts):
    out = []
    rem = x
    for _ in range(parts):
        hi = rem.astype(BF16)
        out.append(hi)
        rem = rem - hi.astype(F32)
    return out


def _log_sigmoid(z):
    return jnp.minimum(z, 0.0) - jnp.log(1.0 + jnp.exp(-jnp.abs(z)))


def _silu(z):
    return z * (1.0 / (1.0 + jnp.exp(-z)))


def _rmsnorm_kernel(x_ref, g_ref, *o_refs):
    x = x_ref[...]
    y = x * lax.rsqrt(jnp.mean(x * x, axis=-1, keepdims=True) + EPS)
    for i, o_ref in enumerate(o_refs):
        o_ref[...] = (y * g_ref[i:i + 1, :]).astype(o_ref.dtype)


def rmsnorm(x, gains, out_dtype):
    rows, d = x.shape
    n = gains.shape[0]
    tm = min(256, rows)
    out_bytes = jnp.dtype(out_dtype).itemsize
    est = 2 * tm * d * 4 + n * 2 * tm * d * out_bytes + 2 * tm * d * 4
    outs = pl.pallas_call(
        _rmsnorm_kernel,
        grid=(rows // tm,),
        in_specs=[pl.BlockSpec((tm, d), lambda i: (i, 0)),
                  pl.BlockSpec((n, d), lambda i: (0, 0))],
        out_specs=[pl.BlockSpec((tm, d), lambda i: (i, 0)) for _ in range(n)],
        out_shape=[jax.ShapeDtypeStruct((rows, d), out_dtype) for _ in range(n)],
        compiler_params=_params(("parallel",), est),
        name="rmsnorm",
    )(x, gains)
    return list(outs)


def _matmul_kernel(*refs, n_parts, has_scale, has_bias, has_res, act, n_out):
    a_refs = refs[:n_parts]
    w_refs = refs[n_parts:2 * n_parts]
    pos = 2 * n_parts
    scale_ref = bias_ref = res_ref = None
    if has_scale:
        scale_ref = refs[pos]; pos += 1
    if has_bias:
        bias_ref = refs[pos]; pos += 1
    if has_res:
        res_ref = refs[pos]; pos += 1
    o_refs = refs[pos:pos + n_out]

    y = _dot(a_refs[0][...], w_refs[0][...])
    for a_ref, w_ref in zip(a_refs[1:], w_refs[1:]):
        y = y + _dot(a_ref[...], w_ref[...])
    if has_scale:
        y = y * scale_ref[...]
    if has_bias:
        y = y + bias_ref[...]
    if act == "log_sigmoid":
        y = _log_sigmoid(y)
    if has_res:
        y = y + res_ref[...]
    for o_ref in o_refs:
        if len(o_ref.shape) == 3:
            dh = o_ref.shape[2]
            for hh in range(o_ref.shape[1]):
                o_ref[:, hh, :] = y[:, hh * dh:(hh + 1) * dh].astype(o_ref.dtype)
        else:
            o_ref[...] = y.astype(o_ref.dtype)


def matmul(a_parts, w, out_dtypes, *, tm, tn, w_layer=None, w_col0=0, n=None, scale=None, bias=None, res=None,
           act=None, head_dims=None, name="matmul"):
    rows = a_parts[0].shape[0]
    kdim, n_w = w.shape[-2:]
    n = n_w if n is None else n
    tm = _pick_tile(rows, tm, SUBLANES)
    tn = _pick_tile(n, tn, LANES)
    assert sum(a.shape[1] for a in a_parts) == kdim and w_col0 % tn == 0 and w_col0 + n <= n_w
    col0 = w_col0 // tn
    a_specs, w_specs = [], []
    start = 0
    for a in a_parts:
        kp = a.shape[1]
        assert start % kp == 0, "a part must start at a multiple of its own width in W"
        a_specs.append(pl.BlockSpec((tm, kp), lambda i, j: (i, 0)))
        if w.ndim == 3:
            w_specs.append(pl.BlockSpec((None, kp, tn), lambda i, j, b=start // kp: (w_layer, b, col0 + j)))
        else:
            w_specs.append(pl.BlockSpec((kp, tn), lambda i, j, b=start // kp: (b, col0 + j)))
        start += kp
    in_specs = a_specs + w_specs
    args = list(a_parts) + [w] * len(a_parts)
    for extra in (scale, bias):
        if extra is not None:
            in_specs.append(pl.BlockSpec((1, tn), lambda i, j: (0, j)))
            args.append(extra)
    if res is not None:
        in_specs.append(pl.BlockSpec((tm, tn), lambda i, j: (i, j)))
        args.append(res)
    out_bytes = sum(jnp.dtype(d).itemsize for d in out_dtypes)
    est = (2 * tm * kdim * 2 + 2 * kdim * tn * 2 + 2 * tm * tn * out_bytes
           + (2 * tm * tn * 4 if res is not None else 0) + 2 * tm * tn * 4)
    kern = functools.partial(
        _matmul_kernel, n_parts=len(a_parts), has_scale=scale is not None, has_bias=bias is not None,
        has_res=res is not None, act=act, n_out=len(out_dtypes))
    head_dims = head_dims or [None] * len(out_dtypes)
    out_specs, out_shape = [], []
    for dt, dh in zip(out_dtypes, head_dims):
        if dh is None:
            out_specs.append(pl.BlockSpec((tm, tn), lambda i, j: (i, j)))
            out_shape.append(jax.ShapeDtypeStruct((rows, n), dt))
        else:
            assert tn % (SUBLANES * dh) == 0 or tn == n
            out_specs.append(pl.BlockSpec((tm, tn // dh, dh), lambda i, j: (i, j, 0)))
            out_shape.append(jax.ShapeDtypeStruct((rows, n // dh, dh), dt))
    outs = pl.pallas_call(
        kern,
        grid=(rows // tm, n // tn),
        in_specs=in_specs,
        out_specs=out_specs,
        out_shape=out_shape,
        compiler_params=_params(("parallel", "parallel"), est),
        name=name,
    )(*args)
    return list(outs)


FFN_ROW_BLOCK = 256
FFN_ROW_TILE = 2048


def _shift_rows(u, d, prev):
    halo = CONV_W - 1
    rolled = pltpu.roll(u, d, 0)
    head = rolled[0:SUBLANES]
    row = lax.broadcasted_iota(jnp.int32, head.shape, 0)
    for r in range(d):
        head = jnp.where(row == r, prev[halo - d + r:halo - d + r + 1, :], head)
    return jnp.concatenate([head, rolled[SUBLANES:]], axis=0)


def _ffn_up_kernel(*refs, rows_per_seq, tiles_per_seq, has_prev, rb):
    if has_prev:
        (xn_ref, wa_ref, wg_ref, ca_ref, cg_ref, ba_ref, bg_ref, pa_ref, pg_ref,
         h_ref, ta_ref, tg_ref) = refs
        carry_ref = None
        prev_refs = (pa_ref, pg_ref)
    else:
        (xn_ref, wa_ref, wg_ref, ca_ref, cg_ref, ba_ref, bg_ref,
         h_ref, ta_ref, tg_ref, carry_ref) = refs
        prev_refs = (None, None)
    i = pl.program_id(0)
    f = pl.program_id(1)
    tm = xn_ref.shape[0]
    halo = CONV_W - 1
    seg = min(rb, rows_per_seq)
    conv_refs = ((ca_ref, ba_ref, ta_ref), (cg_ref, bg_ref, tg_ref))

    tails = [None, None]
    if not has_prev:
        @pl.when((i % tiles_per_seq) == 0)
        def _():
            carry_ref[f] = jnp.zeros(carry_ref.shape[1:], F32)

        tails = [carry_ref[f, 0, 0:halo, :], carry_ref[f, 1, 0:halo, :]]

    wa, wg = wa_ref[...], wg_ref[...]
    for j in range(tm // rb):
        xs = xn_ref[j * rb:(j + 1) * rb, :]
        us = (_dot(xs, wa), _dot(xs, wg))
        for s in range(rb // seg):
            r0 = j * rb + s * seg
            zs = []
            for which, (c_ref, b_ref, t_ref) in enumerate(conv_refs):
                u = us[which][s * seg:(s + 1) * seg]
                prev = prev_refs[which][r0 // rows_per_seq] if has_prev else tails[which]
                z = b_ref[...] + c_ref[halo:halo + 1, :] * u
                for d in range(1, CONV_W):
                    z = z + c_ref[halo - d:halo - d + 1, :] * _shift_rows(u, d, prev)
                tail = u[seg - halo:seg]
                if has_prev:
                    t_ref[r0 // rows_per_seq] = tail
                else:
                    tails[which] = tail
                zs.append(z)
            h_ref[r0:r0 + seg, :] = (_silu(zs[1]) * zs[0]).astype(h_ref.dtype)
    if not has_prev:
        for which, (_, _, t_ref) in enumerate(conv_refs):
            t_ref[0] = tails[which]
            carry_ref[f, which, 0:halo, :] = tails[which]


def ffn_up(xn, w_up, layer, w_conv, b_conv, prev, *, nseq, seq_len, tf):
    rows, d = xn.shape
    ff = w_up.shape[2] // 2
    nf = ff // tf
    has_prev = prev is not None
    halo = CONV_W - 1
    if has_prev:
        tm, tiles_per_seq, n_tiles, S = rows, 1, 1, nseq
    else:
        tm = min(FFN_ROW_TILE, seq_len)
        tiles_per_seq = seq_len // tm
        n_tiles, S = nseq * tiles_per_seq, 1
    rb = min(FFN_ROW_BLOCK, tm)
    assert seq_len % min(rb, seq_len) == 0 and rb % min(rb, seq_len) == 0
    in_specs = [
        pl.BlockSpec((tm, d), lambda i, f: (i, 0)),
        pl.BlockSpec((None, d, tf), lambda i, f: (layer, 0, f)),
        pl.BlockSpec((None, d, tf), lambda i, f: (layer, 0, nf + f)),
        pl.BlockSpec((CONV_W, tf), lambda i, f: (0, f)),
        pl.BlockSpec((CONV_W, tf), lambda i, f: (0, nf + f)),
        pl.BlockSpec((1, tf), lambda i, f: (0, f)),
        pl.BlockSpec((1, tf), lambda i, f: (0, nf + f)),
    ]
    args = [xn, w_up, w_up, w_conv, w_conv, b_conv, b_conv]
    scratch = []
    if has_prev:
        in_specs += [pl.BlockSpec((S, halo, tf), lambda i, f: (0, 0, f)),
                     pl.BlockSpec((S, halo, tf), lambda i, f: (0, 0, nf + f))]
        args += [prev, prev]
    else:
        scratch.append(pltpu.VMEM((nf, 2, SUBLANES, tf), F32))
    est = 2 * tm * d * 2 + 2 * 2 * d * tf * 2 + 2 * tm * tf * 2 + 12 * rb * tf * 4
    kern = functools.partial(_ffn_up_kernel, rows_per_seq=seq_len, tiles_per_seq=tiles_per_seq,
                             has_prev=has_prev, rb=rb)
    h, tail_a, tail_g = pl.pallas_call(
        kern,
        grid=(n_tiles, nf),
        in_specs=in_specs,
        out_specs=[pl.BlockSpec((tm, tf), lambda i, f: (i, f)),
                   pl.BlockSpec((S, halo, tf), lambda i, f: (i, 0, f)),
                   pl.BlockSpec((S, halo, tf), lambda i, f: (i, 0, f))],
        out_shape=[jax.ShapeDtypeStruct((rows, ff), BF16),
                   jax.ShapeDtypeStruct((n_tiles * S, halo, ff), F32),
                   jax.ShapeDtypeStruct((n_tiles * S, halo, ff), F32)],
        scratch_shapes=scratch,
        compiler_params=_params(("arbitrary", "arbitrary"), est),
        name="ffn_up",
    )(*args)
    tails = jnp.concatenate([tail_a, tail_g], axis=-1)
    if not has_prev:
        tails = tails[tiles_per_seq - 1::tiles_per_seq]
    return h, tails


def _gla_tables():
    C = CHUNK
    t = np.arange(C)[:, None]
    s = np.arange(C)[None, :]
    mats = [(s <= t)]
    masks = [(s == t)]
    m = C
    while m >= 2:
        mid = (t // m) * m + m // 2
        mats.append(((s > mid) & (s <= t)) | ((s > t) & (s <= mid)))
        masks.append((t // m == s // m) & (t % m >= m // 2) & (s % m < m // 2))
        m //= 2
    mats.append(s > t)
    return (np.concatenate(mats, axis=0).astype(np.float32),
            np.stack(masks, axis=0).astype(np.float32))


def _gla_kernel(*refs, heads, dk, dv, n_chunks, zero_init):
    if zero_init:
        (q_ref, k_ref, v_ref, r_ref, lr_ref, wg_ref, bg_ref, gn_ref, tab_ref, msk_ref,
         o_ref, sout_ref, st_ref, la_ref) = refs
        s0_ref = None
    else:
        (q_ref, k_ref, v_ref, r_ref, lr_ref, wg_ref, bg_ref, gn_ref, tab_ref, msk_ref, s0_ref,
         o_ref, sout_ref, st_ref, la_ref) = refs
    t = pl.program_id(2)
    n_tiles = pl.num_programs(2)
    C = CHUNK
    n_levels = msk_ref.shape[0]

    @pl.when(t == 0)
    def _():
        for h in range(heads):
            if zero_init:
                st_ref[h] = jnp.zeros((dv, dk), F32)
            else:
                st_ref[h] = s0_ref[0, h].T

    z = _dot(lr_ref[...].astype(BF16), wg_ref[...]) + bg_ref[...]
    la_ref[...] = _log_sigmoid(z) * (1.0 / GLA_TAU)
    tab = tab_ref[...]

    def chunk(c, carry):
        r0 = pl.multiple_of(c * C, C)
        la = la_ref[pl.ds(r0, C), :]
        sums = None
        for part in _split_bf16(la, 2):
            p = _dot(tab, part)
            sums = p if sums is None else sums + p
        decay = jnp.exp(sums)
        for h in range(heads):
            ks = slice(h * dk, (h + 1) * dk)
            vs = slice(h * dv, (h + 1) * dv)
            q = q_ref[pl.ds(r0, C), ks].astype(F32)
            k = k_ref[pl.ds(r0, C), ks].astype(F32)
            v = v_ref[pl.ds(r0, C), vs]
            att = msk_ref[0] * _nt(q.astype(BF16), k.astype(BF16))
            for lvl in range(1, n_levels):
                e = decay[lvl * C:(lvl + 1) * C, ks]
                att = att + msk_ref[lvl] * _nt((q * e).astype(BF16), (k * e).astype(BF16))
            e_in = decay[0:C, ks]
            st = st_ref[h]
            o = _dot(att.astype(BF16), v) + _nt((q * e_in).astype(BF16), st.astype(BF16))
            k_end = (k * decay[n_levels * C:(n_levels + 1) * C, ks]).astype(BF16)
            st_ref[h] = st * e_in[C - 1:C, :] + _tn(v, k_end)
            on = o * lax.rsqrt(jnp.mean(o * o, axis=-1, keepdims=True) + EPS) * gn_ref[...]
            gate = _silu(r_ref[pl.ds(r0, C), vs].astype(F32))
            o_ref[pl.ds(r0, C), vs] = (on * gate).astype(o_ref.dtype)
        return carry

    lax.fori_loop(0, n_chunks, chunk, 0, unroll=2 if n_chunks % 2 == 0 else 1)

    @pl.when(t == n_tiles - 1)
    def _():
        for h in range(heads):
            sout_ref[0, h] = st_ref[h].T


def gla(proj, lr, wg, bg, gn, s0, *, nseq, seq_len, n_heads, dk, dv):
    rows = proj.shape[0]
    hg = min(4, n_heads)
    n_groups = n_heads // hg
    tt = min(512, seq_len)
    n_tiles = seq_len // tt
    tab, msk = _gla_tables()
    tab = jnp.asarray(tab, BF16)
    msk = jnp.asarray(msk, F32)
    qw, vw = hg * dk, hg * dv
    k_off = (n_heads * dk) // qw
    v_off = (2 * n_heads * dk) // vw
    r_off = (2 * n_heads * dk + n_heads * dv) // vw
    row = lambda b, g, t: b * n_tiles + t
    in_specs = [
        pl.BlockSpec((tt, qw), lambda b, g, t: (row(b, g, t), g)),
        pl.BlockSpec((tt, qw), lambda b, g, t: (row(b, g, t), k_off + g)),
        pl.BlockSpec((tt, vw), lambda b, g, t: (row(b, g, t), v_off + g)),
        pl.BlockSpec((tt, vw), lambda b, g, t: (row(b, g, t), r_off + g)),
        pl.BlockSpec((tt, LANES), lambda b, g, t: (row(b, g, t), 0)),
        pl.BlockSpec((LANES, qw), lambda b, g, t: (0, g)),
        pl.BlockSpec((1, qw), lambda b, g, t: (0, g)),
        pl.BlockSpec((1, dv), lambda b, g, t: (0, 0)),
        pl.BlockSpec(tab.shape, lambda b, g, t: (0, 0)),
        pl.BlockSpec(msk.shape, lambda b, g, t: (0, 0, 0)),
    ]
    args = [proj, proj, proj, proj, lr, wg, bg, gn, tab, msk]
    zero_init = s0 is None
    if not zero_init:
        in_specs.append(pl.BlockSpec((1, hg, dk, dv), lambda b, g, t: (b, g, 0, 0)))
        args.append(s0)
    est = (2 * tt * (2 * qw + 2 * vw) * 2 + 2 * tt * LANES * 4 + 2 * tt * vw * 2
           + 4 * hg * dk * dv * 4 + hg * dk * dv * 4 + tt * qw * 4 + 4 * tab.shape[0] * qw * 4)
    kern = functools.partial(_gla_kernel, heads=hg, dk=dk, dv=dv, n_chunks=tt // CHUNK, zero_init=zero_init)
    o, s_out = pl.pallas_call(
        kern,
        grid=(nseq, n_groups, n_tiles),
        in_specs=in_specs,
        out_specs=[pl.BlockSpec((tt, vw), lambda b, g, t: (row(b, g, t), g)),
                   pl.BlockSpec((1, hg, dk, dv), lambda b, g, t: (b, g, 0, 0))],
        out_shape=[jax.ShapeDtypeStruct((rows, n_heads * dv), BF16),
                   jax.ShapeDtypeStruct((nseq, n_heads, dk, dv), F32)],
        scratch_shapes=[pltpu.VMEM((hg, dv, dk), F32), pltpu.VMEM((tt, qw), F32)],
        compiler_params=_params(("parallel", "parallel", "arbitrary"), est),
        name="gla",
    )(*args)
    return o, s_out


def _mem_attn_kernel(q_ref, k_ref, v_ref, o_ref, *, n_heads):
    for h in range(n_heads):
        cs = slice(h * MEM_DH, (h + 1) * MEM_DH)
        s = _nt(q_ref[:, cs], k_ref[0, :, cs].astype(BF16))
        p = jnp.exp(s - jnp.max(s, axis=-1, keepdims=True))
        l = jnp.sum(p, axis=-1, keepdims=True)
        o = _dot(p.astype(BF16), v_ref[0, :, cs].astype(BF16))
        o_ref[:, cs] = (o / l).astype(o_ref.dtype)


def mem_attn(qsrc, q_col_block, mk, mv, *, seq_len):
    rows = qsrc.shape[0]
    nseq, n_mem, mq = mk.shape
    tm = min(1024, seq_len)
    tiles_per_seq = seq_len // tm
    est = 2 * tm * mq * 2 * 2 + 2 * 2 * n_mem * mq * 4 + 6 * tm * n_mem * 4
    return pl.pallas_call(
        functools.partial(_mem_attn_kernel, n_heads=mq // MEM_DH),
        grid=(rows // tm,),
        in_specs=[pl.BlockSpec((tm, mq), lambda i: (i, q_col_block)),
                  pl.BlockSpec((1, n_mem, mq), lambda i: (i // tiles_per_seq, 0, 0)),
                  pl.BlockSpec((1, n_mem, mq), lambda i: (i // tiles_per_seq, 0, 0))],
        out_specs=pl.BlockSpec((tm, mq), lambda i: (i, 0)),
        out_shape=jax.ShapeDtypeStruct((rows, mq), BF16),
        compiler_params=_params(("parallel",), est),
        name="mem_attn",
    )(qsrc, mk, mv)


def _cumsum_kernel(lf_ref, c_ref, ct_ref, crow_ref, ccol_ref):
    j = pl.program_id(1)
    tb = lf_ref.shape[1]

    @pl.when(j == 0)
    def _():
        crow_ref[...] = jnp.zeros_like(crow_ref)
        ccol_ref[...] = jnp.zeros_like(ccol_ref)

    r = lax.broadcasted_iota(jnp.int32, (tb, tb), 0)
    s = lax.broadcasted_iota(jnp.int32, (tb, tb), 1)
    lower = jnp.where(s <= r, 1.0, 0.0).astype(BF16)
    upper = jnp.where(r <= s, 1.0, 0.0).astype(BF16)
    c = crow_ref[0:1, :]
    ct = ccol_ref[:, 0:1]
    for part in _split_bf16(lf_ref[0], 3):
        c = c + _dot(lower, part)
        ct = ct + _tn(part, upper)
    c_ref[0] = c
    ct_ref[0] = ct
    crow_ref[...] = jnp.broadcast_to(c[tb - 1:tb, :], crow_ref.shape)
    ccol_ref[...] = jnp.broadcast_to(ct[:, tb - 1:tb], ccol_ref.shape)


def cumsum_time(lf):
    nseq, L, w = lf.shape
    tb = L if L <= 2048 else 512
    est = 4 * tb * w * 4 + 4 * tb * tb * 4 + 8 * tb * w * 4
    return pl.pallas_call(
        _cumsum_kernel,
        grid=(nseq, L // tb),
        in_specs=[pl.BlockSpec((1, tb, w), lambda b, j: (b, j, 0))],
        out_specs=[pl.BlockSpec((1, tb, w), lambda b, j: (b, j, 0)),
                   pl.BlockSpec((1, w, tb), lambda b, j: (b, 0, j))],
        out_shape=[jax.ShapeDtypeStruct((nseq, L, w), F32), jax.ShapeDtypeStruct((nseq, w, L), F32)],
        scratch_shapes=[pltpu.VMEM((SUBLANES, w), F32), pltpu.VMEM((w, LANES), F32)],
        compiler_params=_params(("parallel", "arbitrary"), est),
        name="cumsum_time",
    )(lf)


def _pick_lane(x, lane):
    ids = lax.broadcasted_iota(jnp.int32, x.shape, 1)
    return jnp.sum(jnp.where(ids == lane, x, 0.0), axis=-1, keepdims=True)


LOG2E = 1.4426950408889634
FOX_Q_BLOCK = 1024
FOX_ROW_CHAIN = 512
FOX_K_BLOCK = 1024
FOX_AUG = 3


def _fox_kernel(q_ref, k_ref, v_ref, c_ref, o_ref, augq_ref, augk_ref, vaug_ref, *chain_refs, tk, tkf):
    h = pl.program_id(1)
    qi = pl.program_id(2)
    seq_len, dh = k_ref.shape
    tq = q_ref.shape[0]
    n_sub = tq // tk
    chains = [chain_refs[3 * i:3 * i + 3] for i in range(n_sub)]

    @pl.when(qi == 0)
    def _():
        def build(r, carry):
            r0 = pl.multiple_of(r * tk, tk)
            cc = _pick_lane(c_ref[0, pl.ds(r0, tk), :], h) * LOG2E
            lane = lax.broadcasted_iota(jnp.int32, (tk, dh), 1)
            aq = jnp.where(lane < 2 * FOX_AUG, 1.0, 0.0)
            ak = aq
            for n, part in enumerate(_split_bf16(cc, FOX_AUG)):
                aq = jnp.where(lane == n, part.astype(F32), aq)
                ak = jnp.where(lane == FOX_AUG + n, -part.astype(F32), ak)
            augq_ref[pl.ds(r0, tk), :] = aq.astype(BF16)
            augk_ref[pl.ds(r0, tk), :] = ak.astype(BF16)
            vaug_ref[pl.ds(r0, tk), 0:dh] = v_ref[pl.ds(r0, tk), :]
            vaug_ref[pl.ds(r0, tk), dh:2 * dh] = jnp.ones((tk, dh), BF16)
            return carry

        lax.fori_loop(0, seq_len // tk, build, 0)

    q0 = pl.multiple_of(qi * tq, tq)
    for sub, (qa_ref, m_ref, acc_ref) in enumerate(chains):
        qa_ref[:, 0:dh] = q_ref[sub * tk:(sub + 1) * tk, :]
        qa_ref[:, dh:2 * dh] = augq_ref[pl.ds(pl.multiple_of(q0 + sub * tk, tk), tk), :]
        m_ref[...] = jnp.full_like(m_ref, -jnp.inf)
        acc_ref[...] = jnp.zeros_like(acc_ref)

    def scores(sub, k0, width):
        ka = jnp.concatenate([k_ref[pl.ds(k0, width), :], augk_ref[pl.ds(k0, width), :]], axis=1)
        return _nt(chains[sub][0][...], ka)

    def softmax_update(sub, s, masked):
        m_ref = chains[sub][1]
        if masked:
            row = lax.broadcasted_iota(jnp.int32, s.shape, 0)
            col = lax.broadcasted_iota(jnp.int32, s.shape, 1)
            s = jnp.where(col <= row, s, -jnp.inf)
        blocks = [s[:, j * LANES:(j + 1) * LANES] for j in range(s.shape[1] // LANES)]
        mx = blocks[0]
        for blk in blocks[1:]:
            mx = jnp.maximum(mx, blk)
        m_old = m_ref[...]
        m_new = jnp.maximum(m_old, jnp.max(mx, axis=-1, keepdims=True))
        m_ref[...] = m_new
        alpha = jnp.exp2(m_old - m_new)
        pr = jnp.concatenate([jnp.exp2(blk - m_new) for blk in blocks], axis=1)
        return alpha, pr.astype(BF16)

    def accumulate(sub, k0, width, alpha, pr):
        acc_ref = chains[sub][2]
        acc_ref[...] = (jnp.concatenate([alpha] * (2 * dh // LANES), axis=1) * acc_ref[...]
                        + _dot(pr, vaug_ref[pl.ds(k0, width), :]))

    def step(jobs, width):
        ss = [scores(sub, k0, width) for sub, k0, _ in jobs]
        ps = [softmax_update(sub, s, masked) for (sub, _, masked), s in zip(jobs, ss)]
        for (sub, k0, _), (alpha, pr) in zip(jobs, ps):
            accumulate(sub, k0, width, alpha, pr)

    def full_chunk(j, carry):
        k0 = pl.multiple_of(j * tkf, tkf)
        step([(sub, k0, False) for sub in range(n_sub)], tkf)
        return carry

    lax.fori_loop(0, qi * (tq // tkf), full_chunk, 0)
    step([(sub, pl.multiple_of(q0 + d * tk, tk), sub == d) for d in range(n_sub) for sub in range(d, n_sub)], tk)
    for sub, (_, _, acc_ref) in enumerate(chains):
        o_ref[sub * tk:(sub + 1) * tk, :] = (acc_ref[:, 0:dh] / acc_ref[:, dh:2 * dh]).astype(o_ref.dtype)


def fox_prompt(qsrc, kbf, vbf, c, *, nseq, seq_len, n_heads, dh):
    rows = qsrc.shape[0]
    tq = min(FOX_Q_BLOCK, seq_len)
    tk = min(FOX_ROW_CHAIN, tq)
    tkf = min(FOX_K_BLOCK, tq)
    nq = seq_len // tq
    est = (2 * tq * dh * 2 * 2 + 2 * 2 * seq_len * dh * 2 + 2 * seq_len * LANES * 4 + 4 * seq_len * dh * 2
           + tq * 2 * dh * 2 + 3 * tq * LANES * 4 + (tq // tk) * 4 * tk * tkf * 4)
    return pl.pallas_call(
        functools.partial(_fox_kernel, tk=tk, tkf=tkf),
        grid=(nseq, n_heads, nq),
        in_specs=[
            pl.BlockSpec((tq, dh), lambda b, h, qi: (b * nq + qi, h)),
            pl.BlockSpec((seq_len, dh), lambda b, h, qi: (b, h)),
            pl.BlockSpec((seq_len, dh), lambda b, h, qi: (b, h)),
            pl.BlockSpec((1, seq_len, LANES), lambda b, h, qi: (b, 0, 0)),
        ],
        out_specs=pl.BlockSpec((tq, dh), lambda b, h, qi: (b * nq + qi, h)),
        out_shape=jax.ShapeDtypeStruct((rows, n_heads * dh), BF16),
        scratch_shapes=[pltpu.VMEM((seq_len, dh), BF16), pltpu.VMEM((seq_len, dh), BF16),
                        pltpu.VMEM((seq_len, 2 * dh), BF16)]
        + [pltpu.VMEM((tk, 2 * dh), BF16), pltpu.VMEM((tk, LANES), F32),
           pltpu.VMEM((tk, 2 * dh), F32)] * (tq // tk),
        compiler_params=_params(("parallel", "parallel", "arbitrary"), est),
        name="fox_prompt",
    )(qsrc, kbf, vbf, c)


FOX_DECODE_HEADS = 8


def _fox_decode_kernel(q_ref, kc_ref, vc_ref, kn_ref, vn_ref, c_ref, ct_ref, o_ref, *, dh):
    g = pl.program_id(1)
    ts = q_ref.shape[0]
    past = kc_ref.shape[1]
    heads = q_ref.shape[1] // dh
    row = lax.broadcasted_iota(jnp.int32, (ts, ts), 0)
    col = lax.broadcasted_iota(jnp.int32, (ts, ts), 1)
    for hh in range(heads):
        cs = slice(hh * dh, (hh + 1) * dh)
        q = q_ref[:, cs]
        cq = _pick_lane(c_ref[0], g * heads + hh) * LOG2E
        ck = ct_ref[0, hh] * LOG2E
        s1 = _nt(q, kc_ref[0, :, cs].astype(BF16)) + cq - ck[:, 0:past]
        s2 = _nt(q, kn_ref[:, cs]) + cq - ck[:, past:past + ts]
        s2 = jnp.where(col <= row, s2, -jnp.inf)
        m = jnp.maximum(jnp.max(s1, axis=-1, keepdims=True), jnp.max(s2, axis=-1, keepdims=True))
        p1 = jnp.exp2(s1 - m)
        p2 = jnp.exp2(s2 - m)
        l = jnp.sum(p1, axis=-1, keepdims=True) + jnp.sum(p2, axis=-1, keepdims=True)
        o = _dot(p1.astype(BF16), vc_ref[0, :, cs].astype(BF16)) + _dot(p2.astype(BF16), vn_ref[:, cs])
        o_ref[:, cs] = (o / l).astype(o_ref.dtype)


def fox_decode(qsrc, kcache, vcache, kbf, vbf, c, ct4, *, n_heads, dh):
    nseq, past, _ = kcache.shape
    rows = qsrc.shape[0]
    ts = rows // nseq
    tot = past + ts
    hg = FOX_DECODE_HEADS if n_heads % FOX_DECODE_HEADS == 0 else n_heads
    w = hg * dh
    est = 2 * 2 * past * w * 4 + 8 * ts * w * 2 + 2 * ts * LANES * 4 + 8 * ts * tot * 4 + 4 * past * dh * 4
    return pl.pallas_call(
        functools.partial(_fox_decode_kernel, dh=dh),
        grid=(nseq, n_heads // hg),
        in_specs=[
            pl.BlockSpec((ts, w), lambda b, g: (b, g)),
            pl.BlockSpec((1, past, w), lambda b, g: (b, 0, g)),
            pl.BlockSpec((1, past, w), lambda b, g: (b, 0, g)),
            pl.BlockSpec((ts, w), lambda b, g: (b, g)),
            pl.BlockSpec((ts, w), lambda b, g: (b, g)),
            pl.BlockSpec((1, ts, LANES), lambda b, g: (b, past // ts, 0)),
            pl.BlockSpec((1, hg, 1, tot), lambda b, g: (b, g, 0, 0)),
        ],
        out_specs=pl.BlockSpec((ts, w), lambda b, g: (b, g)),
        out_shape=jax.ShapeDtypeStruct((rows, n_heads * dh), BF16),
        compiler_params=_params(("parallel", "parallel"), est),
        name="fox_decode",
    )(qsrc, kcache, vcache, kbf, vbf, c, ct4)


def _pad_cols(x, width):
    return jnp.pad(x, ((0, 0), (0, width - x.shape[1])))


def _ffn_block(x, g, w_up, w_down, layer, w_conv, b_conv, prev, *, nseq, seq_len):
    (xn,) = rmsnorm(x, g[None], BF16)
    tf = _pick_tile(w_down.shape[1], 256, LANES)
    h, tails = ffn_up(xn, w_up, layer, w_conv, b_conv, prev, nseq=nseq, seq_len=seq_len, tf=tf)
    (x,) = matmul([h], w_down, [F32], tm=512, tn=512, w_layer=layer, res=x, name="ffn_down")
    return x, tails


def kernel(x_prompt, x_sample, state_gla, cache_fox_k, cache_fox_v, cache_fox_logf, cache_mem_k, cache_mem_v,
           state_ffn_conv, mem_prompt, g_attn, g_ffn, g_mem, w_mem_kv, w_in_a, w_gate_a, b_gate_a, g_gla,
           w_in_b, g_kv, w_kv, b_forget, w_out, w_up, w_conv, b_conv, w_down, g_final):
    n_p, t_p, d = x_prompt.shape
    n_s, t_s, _ = x_sample.shape
    depth = g_attn.shape[0]
    n_a = state_gla.shape[0]
    _, _, gla_heads, dk, dv = state_gla.shape
    _, past, fox_heads, dh = cache_fox_k.shape
    n_mem, mq = cache_mem_k.shape[2:]
    a_q, a_v, b_q = gla_heads * dk, gla_heads * dv, fox_heads * dh

    streams = [
        dict(x=x_prompt.reshape(n_p * t_p, d), nseq=n_p, seq_len=t_p),
        dict(x=x_sample.reshape(n_s * t_s, d), nseq=n_s, seq_len=t_s),
    ]
    mem_rows = mem_prompt.reshape(n_p * n_mem, d)
    mk_out, mv_out, conv_out = [], [], [[], []]
    gla_out = [[], []]
    fox = [None, None]

    w_mem_bf, w_in_a_bf, w_in_b_bf, w_kv_bf = (w.astype(BF16) for w in (w_mem_kv, w_in_a, w_in_b, w_kv))
    w_out_bf, w_up_bf, w_down_bf = (w.astype(BF16) for w in (w_out, w_up, w_down))
    mem_scale = jnp.full((1, mq), MEM_DH ** -0.5, F32)
    main = 2 * a_q + 2 * a_v

    for l in range(depth):
        (mem_n,) = rmsnorm(mem_rows, g_mem[l][None], BF16)
        (mkv,) = matmul([mem_n], w_mem_bf, [F32], tm=512, tn=1024, w_layer=l, name="mem_kv")
        mk_p = mkv[:, :mq].reshape(n_p, n_mem, mq)
        mv_p = mkv[:, mq:].reshape(n_p, n_mem, mq)
        mk_out.append(mk_p)
        mv_out.append(mv_p)
        mem_kv = [(mk_p, mv_p), (cache_mem_k[l], cache_mem_v[l])]

        if l < n_a:
            w_lr = _pad_cols(w_in_a[l][:, main:main + GLA_RANK], LANES).astype(BF16)
            w_qm = w_in_a[l][:, main + GLA_RANK:].astype(BF16)
            col_scale = jnp.concatenate([jnp.full((a_q,), dk ** -0.5, F32), jnp.ones((a_q + 2 * a_v,), F32)])[None]
            wg = jnp.pad(w_gate_a[l], ((0, LANES - GLA_RANK), (0, 0))).astype(BF16)
            bg = b_gate_a[l][None]
            gn = g_gla[l][None]
        else:
            j = l - n_a
            col_scale = jnp.concatenate([jnp.full((b_q,), LOG2E * dh ** -0.5, F32), mem_scale[0]])[None]
            if l == n_a:
                w_kv_f = _pad_cols(w_kv[:, 2 * b_q:], LANES).astype(BF16)
                b_f = _pad_cols(b_forget[None], LANES)

        for si, st in enumerate(streams):
            x, nseq, seq_len = st["x"], st["nseq"], st["seq_len"]
            mk, mv = mem_kv[si]
            if l < n_a:
                (xn,) = rmsnorm(x, g_attn[l][None], BF16)
                (proj,) = matmul([xn], w_in_a_bf, [BF16], tm=1024, tn=1024, w_layer=l, n=main, scale=col_scale,
                                 name="in_proj_a")
                (qm,) = matmul([xn], w_qm, [BF16], tm=1024, tn=1024, scale=mem_scale, name="in_proj_a_mem")
                (lr,) = matmul([xn], w_lr, [F32], tm=1024, tn=LANES, name="in_proj_a_gate")
                s0 = None if si == 0 else state_gla[l]
                o, s_new = gla(proj, lr, wg, bg, gn, s0, nseq=nseq, seq_len=seq_len,
                               n_heads=gla_heads, dk=dk, dv=dv)
                gla_out[si].append(s_new)
                mo = mem_attn(qm, 0, mk, mv, seq_len=seq_len)
            else:
                if l == n_a:
                    xkv, xn = rmsnorm(x, jnp.stack([g_kv, g_attn[l]]), BF16)
                    k32, kbf = matmul([xkv], w_kv_bf, [F32, BF16], tm=1024, tn=1024, n=b_q, head_dims=[dh, None],
                                      name="k_proj")
                    v32, vbf = matmul([xkv], w_kv_bf, [F32, BF16], tm=1024, tn=1024, w_col0=b_q, n=b_q,
                                      head_dims=[dh, None], name="v_proj")
                    (lf,) = matmul([xkv], w_kv_f, [F32], tm=1024, tn=LANES, bias=b_f, act="log_sigmoid",
                                   name="kv_proj_forget")
                    lf3 = lf.reshape(nseq, seq_len, LANES)
                    if si == 0:
                        lf_all = lf3
                    else:
                        lf_all = jnp.concatenate(
                            [jnp.pad(cache_fox_logf, ((0, 0), (0, 0), (0, LANES - fox_heads))), lf3], axis=1)
                    c, ct = cumsum_time(lf_all)
                    ct4 = ct[:, :fox_heads].reshape(nseq, fox_heads, 1, ct.shape[-1])
                    fox[si] = dict(k32=k32, v32=v32, kbf=kbf, vbf=vbf, lf=lf3[:, :, :fox_heads], c=c, ct4=ct4)
                else:
                    (xn,) = rmsnorm(x, g_attn[l][None], BF16)
                fx = fox[si]
                (proj,) = matmul([xn], w_in_b_bf, [BF16], tm=1024, tn=1024, w_layer=j, scale=col_scale,
                                 name="in_proj_b")
                kb, vb = fx["kbf"], fx["vbf"]
                if si == 0:
                    o = fox_prompt(proj, kb, vb, fx["c"], nseq=nseq, seq_len=seq_len,
                                   n_heads=fox_heads, dh=dh)
                else:
                    o = fox_decode(proj, cache_fox_k.reshape(nseq, past, b_q), cache_fox_v.reshape(nseq, past, b_q),
                                   kb, vb, fx["c"], fx["ct4"], n_heads=fox_heads, dh=dh)
                mo = mem_attn(proj, b_q // mq, mk, mv, seq_len=seq_len)
            (x,) = matmul([o, mo], w_out_bf, [F32], tm=1024, tn=1024, w_layer=l, res=x, name="out_proj")
            prev = None if si == 0 else state_ffn_conv[l]
            x, tails = _ffn_block(x, g_ffn[l], w_up_bf, w_down_bf, l, w_conv[l], b_conv[l][None], prev,
                                  nseq=nseq, seq_len=seq_len)
            conv_out[si].append(tails)
            st["x"] = x

    ys = []
    for st in streams:
        (y,) = rmsnorm(st["x"], g_final[None], F32)
        ys.append(y)
    fp, fs = fox
    return (ys[0].reshape(n_p, t_p, d), ys[1].reshape(n_s, t_s, d),
            jnp.stack(gla_out[0]), jnp.stack(gla_out[1]),
            fp["k32"].reshape(n_p, t_p, fox_heads, dh), fp["v32"].reshape(n_p, t_p, fox_heads, dh), fp["lf"],
            fs["k32"].reshape(n_s, t_s, fox_heads, dh), fs["v32"].reshape(n_s, t_s, fox_heads, dh), fs["lf"],
            jnp.stack(mk_out), jnp.stack(mv_out),
            jnp.stack(conv_out[0]), jnp.stack(conv_out[1]))
```

```python
import functools

import numpy as np
import jax
import jax.numpy as jnp
from jax import lax
from jax.experimental import pallas as pl
from jax.experimental.pallas import tpu as pltpu

F32 = jnp.float32
BF16 = jnp.bfloat16

CHUNK = 64
GLA_RANK = 16
GLA_TAU = 16.0
MEM_DH = 256
CONV_W = 3
EPS = 1e-6

LANES = 128
SUBLANES = 8
VMEM_PHYSICAL_BYTES = 64 * 1024 * 1024
VMEM_HEADROOM_BYTES = 6 * 1024 * 1024


def _vmem_limit(estimate_bytes):
    want = int(estimate_bytes * 1.25) + 4 * 1024 * 1024
    return int(min(max(want, 16 * 1024 * 1024), VMEM_PHYSICAL_BYTES - VMEM_HEADROOM_BYTES))


def _params(semantics, estimate_bytes):
    return pltpu.CompilerParams(dimension_semantics=semantics, vmem_limit_bytes=_vmem_limit(estimate_bytes))


def _pick_tile(dim, preferred, unit):
    best = None
    for t in range(unit, min(dim, preferred) + 1, unit):
        if dim % t == 0:
            best = t
    assert best is not None, (dim, preferred, unit)
    return best


def _nt(a, b):
    return lax.dot_general(a, b, (((1,), (1,)), ((), ())), preferred_element_type=F32)


def _tn(a, b):
    return lax.dot_general(a, b, (((0,), (0,)), ((), ())), preferred_element_type=F32)


def _dot(a, b):
    return jnp.dot(a, b, preferred_element_type=F32)


def _split_bf16(x, parts):
    out = []
    rem = x
    for _ in range(parts):
        hi = rem.astype(BF16)
        out.append(hi)
        rem = rem - hi.astype(F32)
    return out


def _log_sigmoid(z):
    return jnp.minimum(z, 0.0) - jnp.log(1.0 + jnp.exp(-jnp.abs(z)))


def _silu(z):
    return z * (1.0 / (1.0 + jnp.exp(-z)))


def _rmsnorm_kernel(x_ref, g_ref, *o_refs):
    x = x_ref[...]
    y = x * lax.rsqrt(jnp.mean(x * x, axis=-1, keepdims=True) + EPS)
    for i, o_ref in enumerate(o_refs):
        o_ref[...] = (y * g_ref[i:i + 1, :]).astype(o_ref.dtype)


def rmsnorm(x, gains, out_dtype):
    rows, d = x.shape
    n = gains.shape[0]
    tm = min(256, rows)
    out_bytes = jnp.dtype(out_dtype).itemsize
    est = 2 * tm * d * 4 + n * 2 * tm * d * out_bytes + 2 * tm * d * 4
    outs = pl.pallas_call(
        _rmsnorm_kernel,
        grid=(rows // tm,),
        in_specs=[pl.BlockSpec((tm, d), lambda i: (i, 0)),
                  pl.BlockSpec((n, d), lambda i: (0, 0))],
        out_specs=[pl.BlockSpec((tm, d), lambda i: (i, 0)) for _ in range(n)],
        out_shape=[jax.ShapeDtypeStruct((rows, d), out_dtype) for _ in range(n)],
        compiler_params=_params(("parallel",), est),
        name="rmsnorm",
    )(x, gains)
    return list(outs)


def _row_rsqrt(ssq, norm_dim):
    total = ssq[:, 0:LANES]
    for j in range(1, ssq.shape[1] // LANES):
        total = total + ssq[:, j * LANES:(j + 1) * LANES]
    return lax.rsqrt(total * (1.0 / norm_dim) + EPS)


def _tile_lanes(v, width):
    return jnp.concatenate([v] * (width // LANES), axis=1) if width > LANES else v


def _matmul_kernel(*refs, n_parts, norm_dim, has_scale, has_bias, has_res, act, n_out, n_gains):
    a_refs = refs[:n_parts]
    w_refs = refs[n_parts:2 * n_parts]
    pos = 2 * n_parts
    ssq_ref = scale_ref = bias_ref = res_ref = gains_ref = None
    if norm_dim:
        ssq_ref = refs[pos]; pos += 1
    if has_scale:
        scale_ref = refs[pos]; pos += 1
    if has_bias:
        bias_ref = refs[pos]; pos += 1
    if has_res:
        res_ref = refs[pos]; pos += 1
    if n_gains:
        gains_ref = refs[pos]; pos += 1
    o_refs = refs[pos:pos + n_out]
    pos += n_out
    xg_refs = refs[pos:pos + n_gains]
    ssq_out_ref = refs[pos + n_gains] if n_gains else None

    y = _dot(a_refs[0][...], w_refs[0][...])
    for a_ref, w_ref in zip(a_refs[1:], w_refs[1:]):
        y = y + _dot(a_ref[...], w_ref[...])
    if norm_dim:
        y = y * _tile_lanes(_row_rsqrt(ssq_ref[...], norm_dim), y.shape[1])
    if has_scale:
        y = y * scale_ref[...]
    if has_bias:
        y = y + bias_ref[...]
    if act == "log_sigmoid":
        y = _log_sigmoid(y)
    if has_res:
        y = y + res_ref[...]
    for o_ref in o_refs:
        if len(o_ref.shape) == 3:
            dh = o_ref.shape[2]
            for hh in range(o_ref.shape[1]):
                o_ref[:, hh, :] = y[:, hh * dh:(hh + 1) * dh].astype(o_ref.dtype)
        else:
            o_ref[...] = y.astype(o_ref.dtype)
    for gi, xg_ref in enumerate(xg_refs):
        xg_ref[...] = (y * gains_ref[gi:gi + 1, :]).astype(xg_ref.dtype)
    if n_gains:
        part = jnp.broadcast_to(jnp.sum(y * y, axis=-1, keepdims=True), ssq_out_ref.shape)
        j = pl.program_id(1)

        @pl.when(j == 0)
        def _():
            ssq_out_ref[...] = part

        @pl.when(j > 0)
        def _():
            ssq_out_ref[...] += part


def matmul(a_parts, w, out_dtypes, *, tm, tn, w_layer=None, w_col0=0, n=None, row_ssq=None, norm_dim=None,
           scale=None, bias=None, res=None, act=None, head_dims=None, next_gains=None, name="matmul"):
    rows = a_parts[0].shape[0]
    kdim, n_w = w.shape[-2:]
    n = n_w if n is None else n
    tm = _pick_tile(rows, tm, SUBLANES)
    tn = _pick_tile(n, tn, LANES)
    assert sum(a.shape[1] for a in a_parts) == kdim and w_col0 % tn == 0 and w_col0 + n <= n_w
    col0 = w_col0 // tn
    a_specs, w_specs = [], []
    start = 0
    for a in a_parts:
        kp = a.shape[1]
        assert start % kp == 0, "a part must start at a multiple of its own width in W"
        a_specs.append(pl.BlockSpec((tm, kp), lambda i, j: (i, 0)))
        if w.ndim == 3:
            w_specs.append(pl.BlockSpec((None, kp, tn), lambda i, j, b=start // kp: (w_layer, b, col0 + j)))
        else:
            w_specs.append(pl.BlockSpec((kp, tn), lambda i, j, b=start // kp: (b, col0 + j)))
        start += kp
    in_specs = a_specs + w_specs
    args = list(a_parts) + [w] * len(a_parts)
    if row_ssq is not None:
        in_specs.append(pl.BlockSpec((tm, row_ssq.shape[1]), lambda i, j: (i, 0)))
        args.append(row_ssq)
    for extra in (scale, bias):
        if extra is not None:
            in_specs.append(pl.BlockSpec((1, tn), lambda i, j: (0, j)))
            args.append(extra)
    if res is not None:
        in_specs.append(pl.BlockSpec((tm, tn), lambda i, j: (i, j)))
        args.append(res)
    n_gains = 0 if next_gains is None else next_gains.shape[0]
    if n_gains:
        in_specs.append(pl.BlockSpec((n_gains, tn), lambda i, j: (0, j)))
        args.append(next_gains)
    out_bytes = sum(jnp.dtype(d).itemsize for d in out_dtypes) + 2 * n_gains
    est = (2 * tm * kdim * 2 + 2 * kdim * tn * 2 + 2 * tm * tn * out_bytes
           + (2 * tm * tn * 4 if res is not None else 0) + 2 * tm * tn * 4
           + (2 * tm * row_ssq.shape[1] * 4 if row_ssq is not None else 0))
    kern = functools.partial(
        _matmul_kernel, n_parts=len(a_parts), norm_dim=norm_dim if row_ssq is not None else None,
        has_scale=scale is not None, has_bias=bias is not None, has_res=res is not None, act=act,
        n_out=len(out_dtypes), n_gains=n_gains)
    head_dims = head_dims or [None] * len(out_dtypes)
    out_specs, out_shape = [], []
    for dt, dh in zip(out_dtypes, head_dims):
        if dh is None:
            out_specs.append(pl.BlockSpec((tm, tn), lambda i, j: (i, j)))
            out_shape.append(jax.ShapeDtypeStruct((rows, n), dt))
        else:
            assert tn % (SUBLANES * dh) == 0 or tn == n
            out_specs.append(pl.BlockSpec((tm, tn // dh, dh), lambda i, j: (i, j, 0)))
            out_shape.append(jax.ShapeDtypeStruct((rows, n // dh, dh), dt))
    for _ in range(n_gains):
        out_specs.append(pl.BlockSpec((tm, tn), lambda i, j: (i, j)))
        out_shape.append(jax.ShapeDtypeStruct((rows, n), BF16))
    if n_gains:
        out_specs.append(pl.BlockSpec((tm, LANES), lambda i, j: (i, 0)))
        out_shape.append(jax.ShapeDtypeStruct((rows, LANES), F32))
    outs = pl.pallas_call(
        kern,
        grid=(rows // tm, n // tn),
        in_specs=in_specs,
        out_specs=out_specs,
        out_shape=out_shape,
        compiler_params=_params(("parallel", "arbitrary" if n_gains else "parallel"), est),
        name=name,
    )(*args)
    return list(outs)


FFN_ROW_BLOCK = 256
FFN_ROW_TILE = 2048


def _shift_rows(u, d, prev):
    halo = CONV_W - 1
    rolled = pltpu.roll(u, d, 0)
    head = rolled[0:SUBLANES]
    row = lax.broadcasted_iota(jnp.int32, head.shape, 0)
    for r in range(d):
        head = jnp.where(row == r, prev[halo - d + r:halo - d + r + 1, :], head)
    return jnp.concatenate([head, rolled[SUBLANES:]], axis=0)


def _ffn_up_kernel(*refs, rows_per_seq, tiles_per_seq, has_prev, rb, norm_dim):
    if has_prev:
        (xg_ref, ssq_ref, wa_ref, wg_ref, ca_ref, cg_ref, ba_ref, bg_ref, pa_ref, pg_ref,
         h_ref, ta_ref, tg_ref, rs_ref) = refs
        carry_ref = None
        prev_refs = (pa_ref, pg_ref)
    else:
        (xg_ref, ssq_ref, wa_ref, wg_ref, ca_ref, cg_ref, ba_ref, bg_ref,
         h_ref, ta_ref, tg_ref, rs_ref, carry_ref) = refs
        prev_refs = (None, None)
    i = pl.program_id(0)
    f = pl.program_id(1)
    tm = xg_ref.shape[0]
    tf = wa_ref.shape[1]
    halo = CONV_W - 1
    seg = min(rb, rows_per_seq)
    conv_refs = ((ca_ref, ba_ref, ta_ref), (cg_ref, bg_ref, tg_ref))

    @pl.when(f == 0)
    def _():
        rs_ref[...] = _row_rsqrt(ssq_ref[...], norm_dim)

    tails = [None, None]
    if not has_prev:
        @pl.when((i % tiles_per_seq) == 0)
        def _():
            carry_ref[f] = jnp.zeros(carry_ref.shape[1:], F32)

        tails = [carry_ref[f, 0, 0:halo, :], carry_ref[f, 1, 0:halo, :]]

    wa, wg = wa_ref[...], wg_ref[...]
    for j in range(tm // rb):
        xs = xg_ref[j * rb:(j + 1) * rb, :]
        rs = _tile_lanes(rs_ref[j * rb:(j + 1) * rb, :], tf)
        us = (_dot(xs, wa) * rs, _dot(xs, wg) * rs)
        for s in range(rb // seg):
            r0 = j * rb + s * seg
            zs = []
            for which, (c_ref, b_ref, t_ref) in enumerate(conv_refs):
                u = us[which][s * seg:(s + 1) * seg]
                prev = prev_refs[which][r0 // rows_per_seq] if has_prev else tails[which]
                z = b_ref[...] + c_ref[halo:halo + 1, :] * u
                for d in range(1, CONV_W):
                    z = z + c_ref[halo - d:halo - d + 1, :] * _shift_rows(u, d, prev)
                tail = u[seg - halo:seg]
                if has_prev:
                    t_ref[r0 // rows_per_seq] = tail
                else:
                    tails[which] = tail
                zs.append(z)
            h_ref[r0:r0 + seg, :] = (_silu(zs[1]) * zs[0]).astype(h_ref.dtype)
    if not has_prev:
        for which, (_, _, t_ref) in enumerate(conv_refs):
            t_ref[0] = tails[which]
            carry_ref[f, which, 0:halo, :] = tails[which]


def ffn_up(xg, row_ssq, w_up, layer, w_conv, b_conv, prev, *, nseq, seq_len, tf):
    rows, d = xg.shape
    ff = w_up.shape[2] // 2
    nf = ff // tf
    has_prev = prev is not None
    halo = CONV_W - 1
    if has_prev:
        tm, tiles_per_seq, n_tiles, S = rows, 1, 1, nseq
    else:
        tm = min(FFN_ROW_TILE, seq_len)
        tiles_per_seq = seq_len // tm
        n_tiles, S = nseq * tiles_per_seq, 1
    rb = min(FFN_ROW_BLOCK, tm)
    assert seq_len % min(rb, seq_len) == 0 and rb % min(rb, seq_len) == 0
    in_specs = [
        pl.BlockSpec((tm, d), lambda i, f: (i, 0)),
        pl.BlockSpec((tm, row_ssq.shape[1]), lambda i, f: (i, 0)),
        pl.BlockSpec((None, d, tf), lambda i, f: (layer, 0, f)),
        pl.BlockSpec((None, d, tf), lambda i, f: (layer, 0, nf + f)),
        pl.BlockSpec((CONV_W, tf), lambda i, f: (0, f)),
        pl.BlockSpec((CONV_W, tf), lambda i, f: (0, nf + f)),
        pl.BlockSpec((1, tf), lambda i, f: (0, f)),
        pl.BlockSpec((1, tf), lambda i, f: (0, nf + f)),
    ]
    args = [xg, row_ssq, w_up, w_up, w_conv, w_conv, b_conv, b_conv]
    scratch = [pltpu.VMEM((tm, LANES), F32)]
    if has_prev:
        in_specs += [pl.BlockSpec((S, halo, tf), lambda i, f: (0, 0, f)),
                     pl.BlockSpec((S, halo, tf), lambda i, f: (0, 0, nf + f))]
        args += [prev, prev]
    else:
        scratch.append(pltpu.VMEM((nf, 2, SUBLANES, tf), F32))
    est = (2 * tm * d * 2 + 2 * 2 * d * tf * 2 + 2 * tm * tf * 2 + 12 * rb * tf * 4
           + 2 * tm * row_ssq.shape[1] * 4 + tm * LANES * 4)
    kern = functools.partial(_ffn_up_kernel, rows_per_seq=seq_len, tiles_per_seq=tiles_per_seq,
                             has_prev=has_prev, rb=rb, norm_dim=d)
    h, tail_a, tail_g = pl.pallas_call(
        kern,
        grid=(n_tiles, nf),
        in_specs=in_specs,
        out_specs=[pl.BlockSpec((tm, tf), lambda i, f: (i, f)),
                   pl.BlockSpec((S, halo, tf), lambda i, f: (i, 0, f)),
                   pl.BlockSpec((S, halo, tf), lambda i, f: (i, 0, f))],
        out_shape=[jax.ShapeDtypeStruct((rows, ff), BF16),
                   jax.ShapeDtypeStruct((n_tiles * S, halo, ff), F32),
                   jax.ShapeDtypeStruct((n_tiles * S, halo, ff), F32)],
        scratch_shapes=scratch,
        compiler_params=_params(("arbitrary", "arbitrary"), est),
        name="ffn_up",
    )(*args)
    tails = jnp.concatenate([tail_a, tail_g], axis=-1)
    if not has_prev:
        tails = tails[tiles_per_seq - 1::tiles_per_seq]
    return h, tails


def _gla_tables():
    C = CHUNK
    t = np.arange(C)[:, None]
    s = np.arange(C)[None, :]
    mats = [(s <= t)]
    masks = [(s == t)]
    m = C
    while m >= 2:
        mid = (t // m) * m + m // 2
        mats.append(((s > mid) & (s <= t)) | ((s > t) & (s <= mid)))
        masks.append((t // m == s // m) & (t % m >= m // 2) & (s % m < m // 2))
        m //= 2
    mats.append(s > t)
    return (np.concatenate(mats, axis=0).astype(np.float32),
            np.stack(masks, axis=0).astype(np.float32))


def _gla_kernel(*refs, heads, dk, dv, n_chunks, zero_init):
    if zero_init:
        (q_ref, k_ref, v_ref, r_ref, lr_ref, wg_ref, bg_ref, gn_ref, tab_ref, msk_ref,
         o_ref, sout_ref, st_ref, la_ref) = refs
        s0_ref = None
    else:
        (q_ref, k_ref, v_ref, r_ref, lr_ref, wg_ref, bg_ref, gn_ref, tab_ref, msk_ref, s0_ref,
         o_ref, sout_ref, st_ref, la_ref) = refs
    t = pl.program_id(2)
    n_tiles = pl.num_programs(2)
    C = CHUNK
    n_levels = msk_ref.shape[0]

    @pl.when(t == 0)
    def _():
        for h in range(heads):
            if zero_init:
                st_ref[h] = jnp.zeros((dv, dk), F32)
            else:
                st_ref[h] = s0_ref[0, h].T

    z = _dot(lr_ref[...].astype(BF16), wg_ref[...]) + bg_ref[...]
    la_ref[...] = _log_sigmoid(z) * (1.0 / GLA_TAU)
    tab = tab_ref[...]

    def chunk(c, carry):
        r0 = pl.multiple_of(c * C, C)
        la = la_ref[pl.ds(r0, C), :]
        sums = None
        for part in _split_bf16(la, 2):
            p = _dot(tab, part)
            sums = p if sums is None else sums + p
        decay = jnp.exp(sums)
        for h in range(heads):
            ks = slice(h * dk, (h + 1) * dk)
            vs = slice(h * dv, (h + 1) * dv)
            q = q_ref[pl.ds(r0, C), ks].astype(F32)
            k = k_ref[pl.ds(r0, C), ks].astype(F32)
            v = v_ref[pl.ds(r0, C), vs]
            att = msk_ref[0] * _nt(q.astype(BF16), k.astype(BF16))
            for lvl in range(1, n_levels):
                e = decay[lvl * C:(lvl + 1) * C, ks]
                att = att + msk_ref[lvl] * _nt((q * e).astype(BF16), (k * e).astype(BF16))
            e_in = decay[0:C, ks]
            st = st_ref[h]
            o = _dot(att.astype(BF16), v) + _nt((q * e_in).astype(BF16), st.astype(BF16))
            k_end = (k * decay[n_levels * C:(n_levels + 1) * C, ks]).astype(BF16)
            st_ref[h] = st * e_in[C - 1:C, :] + _tn(v, k_end)
            on = o * lax.rsqrt(jnp.mean(o * o, axis=-1, keepdims=True) + EPS) * gn_ref[...]
            gate = _silu(r_ref[pl.ds(r0, C), vs].astype(F32))
            o_ref[pl.ds(r0, C), vs] = (on * gate).astype(o_ref.dtype)
        return carry

    lax.fori_loop(0, n_chunks, chunk, 0, unroll=2 if n_chunks % 2 == 0 else 1)

    @pl.when(t == n_tiles - 1)
    def _():
        for h in range(heads):
            sout_ref[0, h] = st_ref[h].T


def gla(proj, lr, wg, bg, gn, s0, *, nseq, seq_len, n_heads, dk, dv):
    rows = proj.shape[0]
    hg = min(4, n_heads)
    n_groups = n_heads // hg
    tt = min(512, seq_len)
    n_tiles = seq_len // tt
    tab, msk = _gla_tables()
    tab = jnp.asarray(tab, BF16)
    msk = jnp.asarray(msk, F32)
    qw, vw = hg * dk, hg * dv
    k_off = (n_heads * dk) // qw
    v_off = (2 * n_heads * dk) // vw
    r_off = (2 * n_heads * dk + n_heads * dv) // vw
    row = lambda b, g, t: b * n_tiles + t
    in_specs = [
        pl.BlockSpec((tt, qw), lambda b, g, t: (row(b, g, t), g)),
        pl.BlockSpec((tt, qw), lambda b, g, t: (row(b, g, t), k_off + g)),
        pl.BlockSpec((tt, vw), lambda b, g, t: (row(b, g, t), v_off + g)),
        pl.BlockSpec((tt, vw), lambda b, g, t: (row(b, g, t), r_off + g)),
        pl.BlockSpec((tt, LANES), lambda b, g, t: (row(b, g, t), 0)),
        pl.BlockSpec((LANES, qw), lambda b, g, t: (0, g)),
        pl.BlockSpec((1, qw), lambda b, g, t: (0, g)),
        pl.BlockSpec((1, dv), lambda b, g, t: (0, 0)),
        pl.BlockSpec(tab.shape, lambda b, g, t: (0, 0)),
        pl.BlockSpec(msk.shape, lambda b, g, t: (0, 0, 0)),
    ]
    args = [proj, proj, proj, proj, lr, wg, bg, gn, tab, msk]
    zero_init = s0 is None
    if not zero_init:
        in_specs.append(pl.BlockSpec((1, hg, dk, dv), lambda b, g, t: (b, g, 0, 0)))
        args.append(s0)
    est = (2 * tt * (2 * qw + 2 * vw) * 2 + 2 * tt * LANES * 4 + 2 * tt * vw * 2
           + 4 * hg * dk * dv * 4 + hg * dk * dv * 4 + tt * qw * 4 + 4 * tab.shape[0] * qw * 4)
    kern = functools.partial(_gla_kernel, heads=hg, dk=dk, dv=dv, n_chunks=tt // CHUNK, zero_init=zero_init)
    o, s_out = pl.pallas_call(
        kern,
        grid=(nseq, n_groups, n_tiles),
        in_specs=in_specs,
        out_specs=[pl.BlockSpec((tt, vw), lambda b, g, t: (row(b, g, t), g)),
                   pl.BlockSpec((1, hg, dk, dv), lambda b, g, t: (b, g, 0, 0))],
        out_shape=[jax.ShapeDtypeStruct((rows, n_heads * dv), BF16),
                   jax.ShapeDtypeStruct((nseq, n_heads, dk, dv), F32)],
        scratch_shapes=[pltpu.VMEM((hg, dv, dk), F32), pltpu.VMEM((tt, qw), F32)],
        compiler_params=_params(("parallel", "parallel", "arbitrary"), est),
        name="gla",
    )(*args)
    return o, s_out


def _mem_attn_kernel(q_ref, k_ref, v_ref, o_ref, *, n_heads):
    for h in range(n_heads):
        cs = slice(h * MEM_DH, (h + 1) * MEM_DH)
        s = _nt(q_ref[:, cs], k_ref[0, :, cs].astype(BF16))
        p = jnp.exp(s - jnp.max(s, axis=-1, keepdims=True))
        l = jnp.sum(p, axis=-1, keepdims=True)
        o = _dot(p.astype(BF16), v_ref[0, :, cs].astype(BF16))
        o_ref[:, cs] = (o / l).astype(o_ref.dtype)


def mem_attn(qsrc, q_col_block, mk, mv, *, seq_len):
    rows = qsrc.shape[0]
    nseq, n_mem, mq = mk.shape
    tm = min(1024, seq_len)
    tiles_per_seq = seq_len // tm
    est = 2 * tm * mq * 2 * 2 + 2 * 2 * n_mem * mq * 4 + 6 * tm * n_mem * 4
    return pl.pallas_call(
        functools.partial(_mem_attn_kernel, n_heads=mq // MEM_DH),
        grid=(rows // tm,),
        in_specs=[pl.BlockSpec((tm, mq), lambda i: (i, q_col_block)),
                  pl.BlockSpec((1, n_mem, mq), lambda i: (i // tiles_per_seq, 0, 0)),
                  pl.BlockSpec((1, n_mem, mq), lambda i: (i // tiles_per_seq, 0, 0))],
        out_specs=pl.BlockSpec((tm, mq), lambda i: (i, 0)),
        out_shape=jax.ShapeDtypeStruct((rows, mq), BF16),
        compiler_params=_params(("parallel",), est),
        name="mem_attn",
    )(qsrc, mk, mv)


def _cumsum_kernel(lf_ref, c_ref, ct_ref, crow_ref, ccol_ref):
    j = pl.program_id(1)
    tb = lf_ref.shape[1]

    @pl.when(j == 0)
    def _():
        crow_ref[...] = jnp.zeros_like(crow_ref)
        ccol_ref[...] = jnp.zeros_like(ccol_ref)

    r = lax.broadcasted_iota(jnp.int32, (tb, tb), 0)
    s = lax.broadcasted_iota(jnp.int32, (tb, tb), 1)
    lower = jnp.where(s <= r, 1.0, 0.0).astype(BF16)
    upper = jnp.where(r <= s, 1.0, 0.0).astype(BF16)
    c = crow_ref[0:1, :]
    ct = ccol_ref[:, 0:1]
    for part in _split_bf16(lf_ref[0], 3):
        c = c + _dot(lower, part)
        ct = ct + _tn(part, upper)
    c_ref[0] = c
    ct_ref[0] = ct
    crow_ref[...] = jnp.broadcast_to(c[tb - 1:tb, :], crow_ref.shape)
    ccol_ref[...] = jnp.broadcast_to(ct[:, tb - 1:tb], ccol_ref.shape)


def cumsum_time(lf):
    nseq, L, w = lf.shape
    tb = L if L <= 2048 else 512
    est = 4 * tb * w * 4 + 4 * tb * tb * 4 + 8 * tb * w * 4
    return pl.pallas_call(
        _cumsum_kernel,
        grid=(nseq, L // tb),
        in_specs=[pl.BlockSpec((1, tb, w), lambda b, j: (b, j, 0))],
        out_specs=[pl.BlockSpec((1, tb, w), lambda b, j: (b, j, 0)),
                   pl.BlockSpec((1, w, tb), lambda b, j: (b, 0, j))],
        out_shape=[jax.ShapeDtypeStruct((nseq, L, w), F32), jax.ShapeDtypeStruct((nseq, w, L), F32)],
        scratch_shapes=[pltpu.VMEM((SUBLANES, w), F32), pltpu.VMEM((w, LANES), F32)],
        compiler_params=_params(("parallel", "arbitrary"), est),
        name="cumsum_time",
    )(lf)


def _pick_lane(x, lane):
    ids = lax.broadcasted_iota(jnp.int32, x.shape, 1)
    return jnp.sum(jnp.where(ids == lane, x, 0.0), axis=-1, keepdims=True)


LOG2E = 1.4426950408889634
FOX_Q_BLOCK = 2048
FOX_ROW_CHAIN = 512
FOX_K_BLOCK = 1024
FOX_AUG = 3


def _fox_kernel(q_ref, k_ref, v_ref, c_ref, o_ref, augq_ref, augk_ref, vaug_ref, *chain_refs, tk, tkf):
    h = pl.program_id(1)
    qi = pl.program_id(2)
    seq_len, dh = k_ref.shape
    tq = q_ref.shape[0]
    n_sub = tq // tk
    chains = [chain_refs[3 * i:3 * i + 3] for i in range(n_sub)]

    @pl.when(qi == 0)
    def _():
        def build(r, carry):
            r0 = pl.multiple_of(r * tk, tk)
            cc = _pick_lane(c_ref[0, pl.ds(r0, tk), :], h) * LOG2E
            lane = lax.broadcasted_iota(jnp.int32, (tk, dh), 1)
            aq = jnp.where(lane < 2 * FOX_AUG, 1.0, 0.0)
            ak = aq
            for n, part in enumerate(_split_bf16(cc, FOX_AUG)):
                aq = jnp.where(lane == n, part.astype(F32), aq)
                ak = jnp.where(lane == FOX_AUG + n, -part.astype(F32), ak)
            augq_ref[pl.ds(r0, tk), :] = aq.astype(BF16)
            augk_ref[pl.ds(r0, tk), :] = ak.astype(BF16)
            vaug_ref[pl.ds(r0, tk), 0:dh] = v_ref[pl.ds(r0, tk), :]
            vaug_ref[pl.ds(r0, tk), dh:2 * dh] = jnp.ones((tk, dh), BF16)
            return carry

        lax.fori_loop(0, seq_len // tk, build, 0)

    q0 = pl.multiple_of(qi * tq, tq)
    for sub, (qa_ref, m_ref, acc_ref) in enumerate(chains):
        qa_ref[:, 0:dh] = q_ref[sub * tk:(sub + 1) * tk, :]
        qa_ref[:, dh:2 * dh] = augq_ref[pl.ds(pl.multiple_of(q0 + sub * tk, tk), tk), :]
        m_ref[...] = jnp.full_like(m_ref, -jnp.inf)
        acc_ref[...] = jnp.zeros_like(acc_ref)

    def scores(sub, k0, width):
        ka = jnp.concatenate([k_ref[pl.ds(k0, width), :], augk_ref[pl.ds(k0, width), :]], axis=1)
        return _nt(chains[sub][0][...], ka)

    def softmax_update(sub, s, masked):
        m_ref = chains[sub][1]
        if masked:
            row = lax.broadcasted_iota(jnp.int32, s.shape, 0)
            col = lax.broadcasted_iota(jnp.int32, s.shape, 1)
            s = jnp.where(col <= row, s, -jnp.inf)
        blocks = [s[:, j * LANES:(j + 1) * LANES] for j in range(s.shape[1] // LANES)]
        mx = blocks[0]
        for blk in blocks[1:]:
            mx = jnp.maximum(mx, blk)
        m_old = m_ref[...]
        m_new = jnp.maximum(m_old, jnp.max(mx, axis=-1, keepdims=True))
        m_ref[...] = m_new
        alpha = jnp.exp2(m_old - m_new)
        pr = jnp.concatenate([jnp.exp2(blk - m_new) for blk in blocks], axis=1)
        return alpha, pr.astype(BF16)

    def accumulate(sub, k0, width, alpha, pr):
        acc_ref = chains[sub][2]
        acc_ref[...] = (jnp.concatenate([alpha] * (2 * dh // LANES), axis=1) * acc_ref[...]
                        + _dot(pr, vaug_ref[pl.ds(k0, width), :]))

    def step(jobs, width):
        ss = [scores(sub, k0, width) for sub, k0, _ in jobs]
        ps = [softmax_update(sub, s, masked) for (sub, _, masked), s in zip(jobs, ss)]
        for (sub, k0, _), (alpha, pr) in zip(jobs, ps):
            accumulate(sub, k0, width, alpha, pr)

    def full_chunk(j, carry):
        k0 = pl.multiple_of(j * tkf, tkf)
        step([(sub, k0, False) for sub in range(n_sub)], tkf)
        return carry

    lax.fori_loop(0, qi * (tq // tkf), full_chunk, 0)
    step([(sub, pl.multiple_of(q0 + d * tk, tk), sub == d) for d in range(n_sub) for sub in range(d, n_sub)], tk)
    for sub, (_, _, acc_ref) in enumerate(chains):
        o_ref[sub * tk:(sub + 1) * tk, :] = (acc_ref[:, 0:dh] / acc_ref[:, dh:2 * dh]).astype(o_ref.dtype)


def fox_prompt(qsrc, kbf, vbf, c, *, nseq, seq_len, n_heads, dh):
    rows = qsrc.shape[0]
    tq = min(FOX_Q_BLOCK, seq_len)
    tk = min(FOX_ROW_CHAIN, tq)
    tkf = min(FOX_K_BLOCK, tq)
    nq = seq_len // tq
    est = (2 * tq * dh * 2 * 2 + 2 * 2 * seq_len * dh * 2 + 2 * seq_len * LANES * 4 + 4 * seq_len * dh * 2
           + tq * 2 * dh * 2 + 3 * tq * LANES * 4 + (tq // tk) * 4 * tk * tkf * 4)
    return pl.pallas_call(
        functools.partial(_fox_kernel, tk=tk, tkf=tkf),
        grid=(nseq, n_heads, nq),
        in_specs=[
            pl.BlockSpec((tq, dh), lambda b, h, qi: (b * nq + qi, h)),
            pl.BlockSpec((seq_len, dh), lambda b, h, qi: (b, h)),
            pl.BlockSpec((seq_len, dh), lambda b, h, qi: (b, h)),
            pl.BlockSpec((1, seq_len, LANES), lambda b, h, qi: (b, 0, 0)),
        ],
        out_specs=pl.BlockSpec((tq, dh), lambda b, h, qi: (b * nq + qi, h)),
        out_shape=jax.ShapeDtypeStruct((rows, n_heads * dh), BF16),
        scratch_shapes=[pltpu.VMEM((seq_len, dh), BF16), pltpu.VMEM((seq_len, dh), BF16),
                        pltpu.VMEM((seq_len, 2 * dh), BF16)]
        + [pltpu.VMEM((tk, 2 * dh), BF16), pltpu.VMEM((tk, LANES), F32),
           pltpu.VMEM((tk, 2 * dh), F32)] * (tq // tk),
        compiler_params=_params(("parallel", "parallel", "arbitrary"), est),
        name="fox_prompt",
    )(qsrc, kbf, vbf, c)


FOX_DECODE_HEADS = 8


def _fox_decode_kernel(q_ref, kc_ref, vc_ref, kn_ref, vn_ref, c_ref, ct_ref, o_ref, *, dh):
    g = pl.program_id(1)
    ts = q_ref.shape[0]
    past = kc_ref.shape[1]
    heads = q_ref.shape[1] // dh
    row = lax.broadcasted_iota(jnp.int32, (ts, ts), 0)
    col = lax.broadcasted_iota(jnp.int32, (ts, ts), 1)
    for hh in range(heads):
        cs = slice(hh * dh, (hh + 1) * dh)
        q = q_ref[:, cs]
        cq = _pick_lane(c_ref[0], g * heads + hh) * LOG2E
        ck = ct_ref[0, hh] * LOG2E
        s1 = _nt(q, kc_ref[0, :, cs].astype(BF16)) + cq - ck[:, 0:past]
        s2 = _nt(q, kn_ref[:, cs]) + cq - ck[:, past:past + ts]
        s2 = jnp.where(col <= row, s2, -jnp.inf)
        m = jnp.maximum(jnp.max(s1, axis=-1, keepdims=True), jnp.max(s2, axis=-1, keepdims=True))
        p1 = jnp.exp2(s1 - m)
        p2 = jnp.exp2(s2 - m)
        l = jnp.sum(p1, axis=-1, keepdims=True) + jnp.sum(p2, axis=-1, keepdims=True)
        o = _dot(p1.astype(BF16), vc_ref[0, :, cs].astype(BF16)) + _dot(p2.astype(BF16), vn_ref[:, cs])
        o_ref[:, cs] = (o / l).astype(o_ref.dtype)


def fox_decode(qsrc, kcache, vcache, kbf, vbf, c, ct4, *, n_heads, dh):
    nseq, past, _ = kcache.shape
    rows = qsrc.shape[0]
    ts = rows // nseq
    tot = past + ts
    hg = FOX_DECODE_HEADS if n_heads % FOX_DECODE_HEADS == 0 else n_heads
    w = hg * dh
    est = 2 * 2 * past * w * 4 + 8 * ts * w * 2 + 2 * ts * LANES * 4 + 8 * ts * tot * 4 + 4 * past * dh * 4
    return pl.pallas_call(
        functools.partial(_fox_decode_kernel, dh=dh),
        grid=(nseq, n_heads // hg),
        in_specs=[
            pl.BlockSpec((ts, w), lambda b, g: (b, g)),
            pl.BlockSpec((1, past, w), lambda b, g: (b, 0, g)),
            pl.BlockSpec((1, past, w), lambda b, g: (b, 0, g)),
            pl.BlockSpec((ts, w), lambda b, g: (b, g)),
            pl.BlockSpec((ts, w), lambda b, g: (b, g)),
            pl.BlockSpec((1, ts, LANES), lambda b, g: (b, past // ts, 0)),
            pl.BlockSpec((1, hg, 1, tot), lambda b, g: (b, g, 0, 0)),
        ],
        out_specs=pl.BlockSpec((ts, w), lambda b, g: (b, g)),
        out_shape=jax.ShapeDtypeStruct((rows, n_heads * dh), BF16),
        compiler_params=_params(("parallel", "parallel"), est),
        name="fox_decode",
    )(qsrc, kcache, vcache, kbf, vbf, c, ct4)


def _pad_cols(x, width):
    return jnp.pad(x, ((0, 0), (0, width - x.shape[1])))


def _ffn_block(x, xg, ssq, w_up, w_down, layer, w_conv, b_conv, prev, next_gains, *, nseq, seq_len):
    tf = _pick_tile(w_down.shape[1], 256, LANES)
    h, tails = ffn_up(xg, ssq, w_up, layer, w_conv, b_conv, prev, nseq=nseq, seq_len=seq_len, tf=tf)
    outs = matmul([h], w_down, [F32], tm=512, tn=512, w_layer=layer, res=x, next_gains=next_gains, name="ffn_down")
    return outs, tails


def kernel(x_prompt, x_sample, state_gla, cache_fox_k, cache_fox_v, cache_fox_logf, cache_mem_k, cache_mem_v,
           state_ffn_conv, mem_prompt, g_attn, g_ffn, g_mem, w_mem_kv, w_in_a, w_gate_a, b_gate_a, g_gla,
           w_in_b, g_kv, w_kv, b_forget, w_out, w_up, w_conv, b_conv, w_down, g_final):
    n_p, t_p, d = x_prompt.shape
    n_s, t_s, _ = x_sample.shape
    depth = g_attn.shape[0]
    n_a = state_gla.shape[0]
    _, _, gla_heads, dk, dv = state_gla.shape
    _, past, fox_heads, dh = cache_fox_k.shape
    n_mem, mq = cache_mem_k.shape[2:]
    a_q, a_v, b_q = gla_heads * dk, gla_heads * dv, fox_heads * dh

    def mixer_gains(l):
        return jnp.stack([g_attn[l]] + ([g_kv] if l == n_a else []))

    streams = []
    for x3, nseq, seq_len in ((x_prompt, n_p, t_p), (x_sample, n_s, t_s)):
        x = x3.reshape(nseq * seq_len, d)
        streams.append(dict(x=x, nseq=nseq, seq_len=seq_len, normed=rmsnorm(x, mixer_gains(0), BF16), ssq=None))
    mem_rows = mem_prompt.reshape(n_p * n_mem, d)
    mk_out, mv_out, conv_out = [], [], [[], []]
    gla_out = [[], []]
    fox = [None, None]

    w_mem_bf, w_in_a_bf, w_in_b_bf, w_kv_bf = (w.astype(BF16) for w in (w_mem_kv, w_in_a, w_in_b, w_kv))
    w_out_bf, w_up_bf, w_down_bf = (w.astype(BF16) for w in (w_out, w_up, w_down))
    mem_scale = jnp.full((1, mq), MEM_DH ** -0.5, F32)
    main = 2 * a_q + 2 * a_v

    for l in range(depth):
        (mem_n,) = rmsnorm(mem_rows, g_mem[l][None], BF16)
        (mkv,) = matmul([mem_n], w_mem_bf, [F32], tm=512, tn=1024, w_layer=l, name="mem_kv")
        mk_p = mkv[:, :mq].reshape(n_p, n_mem, mq)
        mv_p = mkv[:, mq:].reshape(n_p, n_mem, mq)
        mk_out.append(mk_p)
        mv_out.append(mv_p)
        mem_kv = [(mk_p, mv_p), (cache_mem_k[l], cache_mem_v[l])]

        if l < n_a:
            w_lr = _pad_cols(w_in_a[l][:, main:main + GLA_RANK], LANES).astype(BF16)
            w_qm = w_in_a[l][:, main + GLA_RANK:].astype(BF16)
            col_scale = jnp.concatenate([jnp.full((a_q,), dk ** -0.5, F32), jnp.ones((a_q + 2 * a_v,), F32)])[None]
            wg = jnp.pad(w_gate_a[l], ((0, LANES - GLA_RANK), (0, 0))).astype(BF16)
            bg = b_gate_a[l][None]
            gn = g_gla[l][None]
        else:
            j = l - n_a
            col_scale = jnp.concatenate([jnp.full((b_q,), LOG2E * dh ** -0.5, F32), mem_scale[0]])[None]
            if l == n_a:
                w_kv_f = _pad_cols(w_kv[:, 2 * b_q:], LANES).astype(BF16)
                b_f = _pad_cols(b_forget[None], LANES)
        ffn_next_gains = mixer_gains(l + 1) if l + 1 < depth else None

        for si, st in enumerate(streams):
            x, nseq, seq_len = st["x"], st["nseq"], st["seq_len"]
            mk, mv = mem_kv[si]
            xn = st["normed"][0]
            norm = dict(row_ssq=st["ssq"], norm_dim=d)
            if l < n_a:
                (proj,) = matmul([xn], w_in_a_bf, [BF16], tm=1024, tn=1024, w_layer=l, n=main, scale=col_scale,
                                 name="in_proj_a", **norm)
                (qm,) = matmul([xn], w_qm, [BF16], tm=1024, tn=1024, scale=mem_scale, name="in_proj_a_mem", **norm)
                (lr,) = matmul([xn], w_lr, [F32], tm=1024, tn=LANES, name="in_proj_a_gate", **norm)
                s0 = None if si == 0 else state_gla[l]
                o, s_new = gla(proj, lr, wg, bg, gn, s0, nseq=nseq, seq_len=seq_len,
                               n_heads=gla_heads, dk=dk, dv=dv)
                gla_out[si].append(s_new)
                mo = mem_attn(qm, 0, mk, mv, seq_len=seq_len)
            else:
                if l == n_a:
                    xkv = st["normed"][1]
                    k32, kbf = matmul([xkv], w_kv_bf, [F32, BF16], tm=1024, tn=1024, n=b_q, head_dims=[dh, None],
                                      name="k_proj", **norm)
                    v32, vbf = matmul([xkv], w_kv_bf, [F32, BF16], tm=1024, tn=1024, w_col0=b_q, n=b_q,
                                      head_dims=[dh, None], name="v_proj", **norm)
                    (lf,) = matmul([xkv], w_kv_f, [F32], tm=1024, tn=LANES, bias=b_f, act="log_sigmoid",
                                   name="kv_proj_forget", **norm)
                    lf3 = lf.reshape(nseq, seq_len, LANES)
                    if si == 0:
                        lf_all = lf3
                    else:
                        lf_all = jnp.concatenate(
                            [jnp.pad(cache_fox_logf, ((0, 0), (0, 0), (0, LANES - fox_heads))), lf3], axis=1)
                    c, ct = cumsum_time(lf_all)
                    ct4 = ct[:, :fox_heads].reshape(nseq, fox_heads, 1, ct.shape[-1])
                    fox[si] = dict(k32=k32, v32=v32, kbf=kbf, vbf=vbf, lf=lf3[:, :, :fox_heads], c=c, ct4=ct4)
                fx = fox[si]
                (proj,) = matmul([xn], w_in_b_bf, [BF16], tm=1024, tn=1024, w_layer=j, scale=col_scale,
                                 name="in_proj_b", **norm)
                kb, vb = fx["kbf"], fx["vbf"]
                if si == 0:
                    o = fox_prompt(proj, kb, vb, fx["c"], nseq=nseq, seq_len=seq_len,
                                   n_heads=fox_heads, dh=dh)
                else:
                    o = fox_decode(proj, cache_fox_k.reshape(nseq, past, b_q), cache_fox_v.reshape(nseq, past, b_q),
                                   kb, vb, fx["c"], fx["ct4"], n_heads=fox_heads, dh=dh)
                mo = mem_attn(proj, b_q // mq, mk, mv, seq_len=seq_len)
            x, xg, ssq = matmul([o, mo], w_out_bf, [F32], tm=1024, tn=512, w_layer=l, res=x,
                                next_gains=g_ffn[l][None], name="out_proj")
            prev = None if si == 0 else state_ffn_conv[l]
            outs, tails = _ffn_block(x, xg, ssq, w_up_bf, w_down_bf, l, w_conv[l], b_conv[l][None], prev,
                                     ffn_next_gains, nseq=nseq, seq_len=seq_len)
            conv_out[si].append(tails)
            st["x"] = outs[0]
            if ffn_next_gains is not None:
                st["normed"], st["ssq"] = outs[1:-1], outs[-1]

    ys = []
    for st in streams:
        (y,) = rmsnorm(st["x"], g_final[None], F32)
        ys.append(y)
    fp, fs = fox
    return (ys[0].reshape(n_p, t_p, d), ys[1].reshape(n_s, t_s, d),
            jnp.stack(gla_out[0]), jnp.stack(gla_out[1]),
            fp["k32"].reshape(n_p, t_p, fox_heads, dh), fp["v32"].reshape(n_p, t_p, fox_heads, dh), fp["lf"],
            fs["k32"].reshape(n_s, t_s, fox_heads, dh), fs["v32"].reshape(n_s, t_s, fox_heads, dh), fs["lf"],
            jnp.stack(mk_out), jnp.stack(mv_out),
            jnp.stack(conv_out[0]), jnp.stack(conv_out[1]))
```

```python
import functools

import numpy as np
import jax
import jax.numpy as jnp
from jax import lax
from jax.experimental import pallas as pl
from jax.experimental.pallas import tpu as pltpu

F32 = jnp.float32
BF16 = jnp.bfloat16

CHUNK = 64
GLA_RANK = 16
GLA_TAU = 16.0
MEM_DH = 256
CONV_W = 3
EPS = 1e-6

LANES = 128
SUBLANES = 8
VMEM_PHYSICAL_BYTES = 64 * 1024 * 1024
VMEM_HEADROOM_BYTES = 6 * 1024 * 1024


def _vmem_limit(estimate_bytes):
    want = int(estimate_bytes * 1.25) + 4 * 1024 * 1024
    return int(min(max(want, 16 * 1024 * 1024), VMEM_PHYSICAL_BYTES - VMEM_HEADROOM_BYTES))


def _params(semantics, estimate_bytes):
    return pltpu.CompilerParams(dimension_semantics=semantics, vmem_limit_bytes=_vmem_limit(estimate_bytes))


def _pick_tile(dim, preferred, unit):
    best = None
    for t in range(unit, min(dim, preferred) + 1, unit):
        if dim % t == 0:
            best = t
    assert best is not None, (dim, preferred, unit)
    return best


def _nt(a, b):
    return lax.dot_general(a, b, (((1,), (1,)), ((), ())), preferred_element_type=F32)


def _tn(a, b):
    return lax.dot_general(a, b, (((0,), (0,)), ((), ())), preferred_element_type=F32)


def _dot(a, b):
    return jnp.dot(a, b, preferred_element_type=F32)


def _split_bf16(x, parts):
    out = []
    rem = x
    for _ in range(parts):
        hi = rem.astype(BF16)
        out.append(hi)
        rem = rem - hi.astype(F32)
    return out


def _log_sigmoid(z):
    return jnp.minimum(z, 0.0) - jnp.log(1.0 + jnp.exp(-jnp.abs(z)))


def _silu(z):
    return z * (1.0 / (1.0 + jnp.exp(-z)))


def _rmsnorm_kernel(x_ref, g_ref, *o_refs):
    x = x_ref[...]
    y = x * lax.rsqrt(jnp.mean(x * x, axis=-1, keepdims=True) + EPS)
    for i, o_ref in enumerate(o_refs):
        o_ref[...] = (y * g_ref[i:i + 1, :]).astype(o_ref.dtype)


def rmsnorm(x, gains, out_dtype):
    rows, d = x.shape
    n = gains.shape[0]
    tm = min(256, rows)
    out_bytes = jnp.dtype(out_dtype).itemsize
    est = 2 * tm * d * 4 + n * 2 * tm * d * out_bytes + 2 * tm * d * 4
    outs = pl.pallas_call(
        _rmsnorm_kernel,
        grid=(rows // tm,),
        in_specs=[pl.BlockSpec((tm, d), lambda i: (i, 0)),
                  pl.BlockSpec((n, d), lambda i: (0, 0))],
        out_specs=[pl.BlockSpec((tm, d), lambda i: (i, 0)) for _ in range(n)],
        out_shape=[jax.ShapeDtypeStruct((rows, d), out_dtype) for _ in range(n)],
        compiler_params=_params(("parallel",), est),
        name="rmsnorm",
    )(x, gains)
    return list(outs)


def _row_rsqrt(ssq, norm_dim):
    total = ssq[:, 0:LANES]
    for j in range(1, ssq.shape[1] // LANES):
        total = total + ssq[:, j * LANES:(j + 1) * LANES]
    return lax.rsqrt(total * (1.0 / norm_dim) + EPS)


def _tile_lanes(v, width):
    return jnp.concatenate([v] * (width // LANES), axis=1) if width > LANES else v


def _matmul_kernel(*refs, n_parts, norm_dim, has_scale, has_bias, has_res, act, n_out, n_gains):
    a_refs = refs[:n_parts]
    w_refs = refs[n_parts:2 * n_parts]
    pos = 2 * n_parts
    ssq_ref = scale_ref = bias_ref = res_ref = gains_ref = None
    if norm_dim:
        ssq_ref = refs[pos]; pos += 1
    if has_scale:
        scale_ref = refs[pos]; pos += 1
    if has_bias:
        bias_ref = refs[pos]; pos += 1
    if has_res:
        res_ref = refs[pos]; pos += 1
    if n_gains:
        gains_ref = refs[pos]; pos += 1
    o_refs = refs[pos:pos + n_out]
    pos += n_out
    xg_refs = refs[pos:pos + n_gains]
    ssq_out_ref = refs[pos + n_gains] if n_gains else None

    y = _dot(a_refs[0][...], w_refs[0][...])
    for a_ref, w_ref in zip(a_refs[1:], w_refs[1:]):
        y = y + _dot(a_ref[...], w_ref[...])
    if norm_dim:
        y = y * _tile_lanes(_row_rsqrt(ssq_ref[...], norm_dim), y.shape[1])
    if has_scale:
        y = y * scale_ref[...]
    if has_bias:
        y = y + bias_ref[...]
    if act == "log_sigmoid":
        y = _log_sigmoid(y)
    if has_res:
        y = y + res_ref[...]
    for o_ref in o_refs:
        if len(o_ref.shape) == 3:
            dh = o_ref.shape[2]
            for hh in range(o_ref.shape[1]):
                o_ref[:, hh, :] = y[:, hh * dh:(hh + 1) * dh].astype(o_ref.dtype)
        else:
            o_ref[...] = y.astype(o_ref.dtype)
    for gi, xg_ref in enumerate(xg_refs):
        xg_ref[...] = (y * gains_ref[gi:gi + 1, :]).astype(xg_ref.dtype)
    if n_gains:
        part = jnp.broadcast_to(jnp.sum(y * y, axis=-1, keepdims=True), ssq_out_ref.shape)
        j = pl.program_id(1)

        @pl.when(j == 0)
        def _():
            ssq_out_ref[...] = part

        @pl.when(j > 0)
        def _():
            ssq_out_ref[...] += part


def matmul(a_parts, w, out_dtypes, *, tm, tn, w_layer=None, w_col0=0, n=None, row_ssq=None, norm_dim=None,
           scale=None, bias=None, res=None, act=None, head_dims=None, next_gains=None, name="matmul"):
    rows = a_parts[0].shape[0]
    kdim, n_w = w.shape[-2:]
    n = n_w if n is None else n
    tm = _pick_tile(rows, tm, SUBLANES)
    tn = _pick_tile(n, tn, LANES)
    assert sum(a.shape[1] for a in a_parts) == kdim and w_col0 % tn == 0 and w_col0 + n <= n_w
    col0 = w_col0 // tn
    a_specs, w_specs = [], []
    start = 0
    for a in a_parts:
        kp = a.shape[1]
        assert start % kp == 0, "a part must start at a multiple of its own width in W"
        a_specs.append(pl.BlockSpec((tm, kp), lambda i, j: (i, 0)))
        if w.ndim == 3:
            w_specs.append(pl.BlockSpec((None, kp, tn), lambda i, j, b=start // kp: (w_layer, b, col0 + j)))
        else:
            w_specs.append(pl.BlockSpec((kp, tn), lambda i, j, b=start // kp: (b, col0 + j)))
        start += kp
    in_specs = a_specs + w_specs
    args = list(a_parts) + [w] * len(a_parts)
    if row_ssq is not None:
        in_specs.append(pl.BlockSpec((tm, row_ssq.shape[1]), lambda i, j: (i, 0)))
        args.append(row_ssq)
    for extra in (scale, bias):
        if extra is not None:
            in_specs.append(pl.BlockSpec((1, tn), lambda i, j: (0, j)))
            args.append(extra)
    if res is not None:
        in_specs.append(pl.BlockSpec((tm, tn), lambda i, j: (i, j)))
        args.append(res)
    n_gains = 0 if next_gains is None else next_gains.shape[0]
    if n_gains:
        in_specs.append(pl.BlockSpec((n_gains, tn), lambda i, j: (0, j)))
        args.append(next_gains)
    out_bytes = sum(jnp.dtype(d).itemsize for d in out_dtypes) + 2 * n_gains
    est = (2 * tm * kdim * 2 + 2 * kdim * tn * 2 + 2 * tm * tn * out_bytes
           + (2 * tm * tn * 4 if res is not None else 0) + 2 * tm * tn * 4
           + (2 * tm * row_ssq.shape[1] * 4 if row_ssq is not None else 0))
    kern = functools.partial(
        _matmul_kernel, n_parts=len(a_parts), norm_dim=norm_dim if row_ssq is not None else None,
        has_scale=scale is not None, has_bias=bias is not None, has_res=res is not None, act=act,
        n_out=len(out_dtypes), n_gains=n_gains)
    head_dims = head_dims or [None] * len(out_dtypes)
    out_specs, out_shape = [], []
    for dt, dh in zip(out_dtypes, head_dims):
        if dh is None:
            out_specs.append(pl.BlockSpec((tm, tn), lambda i, j: (i, j)))
            out_shape.append(jax.ShapeDtypeStruct((rows, n), dt))
        else:
            assert tn % (SUBLANES * dh) == 0 or tn == n
            out_specs.append(pl.BlockSpec((tm, tn // dh, dh), lambda i, j: (i, j, 0)))
            out_shape.append(jax.ShapeDtypeStruct((rows, n // dh, dh), dt))
    for _ in range(n_gains):
        out_specs.append(pl.BlockSpec((tm, tn), lambda i, j: (i, j)))
        out_shape.append(jax.ShapeDtypeStruct((rows, n), BF16))
    if n_gains:
        out_specs.append(pl.BlockSpec((tm, LANES), lambda i, j: (i, 0)))
        out_shape.append(jax.ShapeDtypeStruct((rows, LANES), F32))
    outs = pl.pallas_call(
        kern,
        grid=(rows // tm, n // tn),
        in_specs=in_specs,
        out_specs=out_specs,
        out_shape=out_shape,
        compiler_params=_params(("parallel", "arbitrary" if n_gains else "parallel"), est),
        name=name,
    )(*args)
    return list(outs)


FFN_ROW_BLOCK = 256
FFN_ROW_TILE = 2048


def _shift_rows(u, d, prev):
    halo = CONV_W - 1
    rolled = pltpu.roll(u, d, 0)
    head = rolled[0:SUBLANES]
    row = lax.broadcasted_iota(jnp.int32, head.shape, 0)
    for r in range(d):
        head = jnp.where(row == r, prev[halo - d + r:halo - d + r + 1, :], head)
    return jnp.concatenate([head, rolled[SUBLANES:]], axis=0)


def _ffn_up_kernel(*refs, rows_per_seq, tiles_per_seq, has_prev, rb, norm_dim):
    if has_prev:
        (xg_ref, ssq_ref, wa_ref, wg_ref, ca_ref, cg_ref, ba_ref, bg_ref, pa_ref, pg_ref,
         h_ref, ta_ref, tg_ref, rs_ref) = refs
        carry_ref = None
        prev_refs = (pa_ref, pg_ref)
    else:
        (xg_ref, ssq_ref, wa_ref, wg_ref, ca_ref, cg_ref, ba_ref, bg_ref,
         h_ref, ta_ref, tg_ref, rs_ref, carry_ref) = refs
        prev_refs = (None, None)
    i = pl.program_id(0)
    f = pl.program_id(1)
    tm = xg_ref.shape[0]
    tf = wa_ref.shape[1]
    halo = CONV_W - 1
    seg = min(rb, rows_per_seq)
    conv_refs = ((ca_ref, ba_ref, ta_ref), (cg_ref, bg_ref, tg_ref))

    @pl.when(f == 0)
    def _():
        rs_ref[...] = _row_rsqrt(ssq_ref[...], norm_dim)

    tails = [None, None]
    if not has_prev:
        @pl.when((i % tiles_per_seq) == 0)
        def _():
            carry_ref[f] = jnp.zeros(carry_ref.shape[1:], F32)

        tails = [carry_ref[f, 0, 0:halo, :], carry_ref[f, 1, 0:halo, :]]

    wa, wg = wa_ref[...], wg_ref[...]
    for j in range(tm // rb):
        xs = xg_ref[j * rb:(j + 1) * rb, :]
        rs = _tile_lanes(rs_ref[j * rb:(j + 1) * rb, :], tf)
        us = (_dot(xs, wa) * rs, _dot(xs, wg) * rs)
        for s in range(rb // seg):
            r0 = j * rb + s * seg
            zs = []
            for which, (c_ref, b_ref, t_ref) in enumerate(conv_refs):
                u = us[which][s * seg:(s + 1) * seg]
                prev = prev_refs[which][r0 // rows_per_seq] if has_prev else tails[which]
                z = b_ref[...] + c_ref[halo:halo + 1, :] * u
                for d in range(1, CONV_W):
                    z = z + c_ref[halo - d:halo - d + 1, :] * _shift_rows(u, d, prev)
                tail = u[seg - halo:seg]
                if has_prev:
                    t_ref[r0 // rows_per_seq] = tail
                else:
                    tails[which] = tail
                zs.append(z)
            h_ref[r0:r0 + seg, :] = (_silu(zs[1]) * zs[0]).astype(h_ref.dtype)
    if not has_prev:
        for which, (_, _, t_ref) in enumerate(conv_refs):
            t_ref[0] = tails[which]
            carry_ref[f, which, 0:halo, :] = tails[which]


def ffn_up(xg, row_ssq, w_up, layer, w_conv, b_conv, prev, *, nseq, seq_len, tf):
    rows, d = xg.shape
    ff = w_up.shape[2] // 2
    nf = ff // tf
    has_prev = prev is not None
    halo = CONV_W - 1
    if has_prev:
        tm, tiles_per_seq, n_tiles, S = rows, 1, 1, nseq
    else:
        tm = min(FFN_ROW_TILE, seq_len)
        tiles_per_seq = seq_len // tm
        n_tiles, S = nseq * tiles_per_seq, 1
    rb = min(FFN_ROW_BLOCK, tm)
    assert seq_len % min(rb, seq_len) == 0 and rb % min(rb, seq_len) == 0
    in_specs = [
        pl.BlockSpec((tm, d), lambda i, f: (i, 0)),
        pl.BlockSpec((tm, row_ssq.shape[1]), lambda i, f: (i, 0)),
        pl.BlockSpec((None, d, tf), lambda i, f: (layer, 0, f)),
        pl.BlockSpec((None, d, tf), lambda i, f: (layer, 0, nf + f)),
        pl.BlockSpec((CONV_W, tf), lambda i, f: (0, f)),
        pl.BlockSpec((CONV_W, tf), lambda i, f: (0, nf + f)),
        pl.BlockSpec((1, tf), lambda i, f: (0, f)),
        pl.BlockSpec((1, tf), lambda i, f: (0, nf + f)),
    ]
    args = [xg, row_ssq, w_up, w_up, w_conv, w_conv, b_conv, b_conv]
    scratch = [pltpu.VMEM((tm, LANES), F32)]
    if has_prev:
        in_specs += [pl.BlockSpec((S, halo, tf), lambda i, f: (0, 0, f)),
                     pl.BlockSpec((S, halo, tf), lambda i, f: (0, 0, nf + f))]
        args += [prev, prev]
    else:
        scratch.append(pltpu.VMEM((nf, 2, SUBLANES, tf), F32))
    est = (2 * tm * d * 2 + 2 * 2 * d * tf * 2 + 2 * tm * tf * 2 + 12 * rb * tf * 4
           + 2 * tm * row_ssq.shape[1] * 4 + tm * LANES * 4)
    kern = functools.partial(_ffn_up_kernel, rows_per_seq=seq_len, tiles_per_seq=tiles_per_seq,
                             has_prev=has_prev, rb=rb, norm_dim=d)
    h, tail_a, tail_g = pl.pallas_call(
        kern,
        grid=(n_tiles, nf),
        in_specs=in_specs,
        out_specs=[pl.BlockSpec((tm, tf), lambda i, f: (i, f)),
                   pl.BlockSpec((S, halo, tf), lambda i, f: (i, 0, f)),
                   pl.BlockSpec((S, halo, tf), lambda i, f: (i, 0, f))],
        out_shape=[jax.ShapeDtypeStruct((rows, ff), BF16),
                   jax.ShapeDtypeStruct((n_tiles * S, halo, ff), F32),
                   jax.ShapeDtypeStruct((n_tiles * S, halo, ff), F32)],
        scratch_shapes=scratch,
        compiler_params=_params(("arbitrary", "arbitrary"), est),
        name="ffn_up",
    )(*args)
    tails = jnp.concatenate([tail_a, tail_g], axis=-1)
    if not has_prev:
        tails = tails[tiles_per_seq - 1::tiles_per_seq]
    return h, tails


def _gla_tables():
    C = CHUNK
    t = np.arange(C)[:, None]
    s = np.arange(C)[None, :]
    mats = [(s <= t)]
    masks = [(s == t)]
    m = C
    while m >= 2:
        mid = (t // m) * m + m // 2
        mats.append(((s > mid) & (s <= t)) | ((s > t) & (s <= mid)))
        masks.append((t // m == s // m) & (t % m >= m // 2) & (s % m < m // 2))
        m //= 2
    mats.append(s > t)
    return (np.concatenate(mats, axis=0).astype(np.float32),
            np.stack(masks, axis=0).astype(np.float32))


def _gla_kernel(*refs, heads, dk, dv, n_chunks, zero_init):
    if zero_init:
        (q_ref, k_ref, v_ref, r_ref, lr_ref, wg_ref, bg_ref, gn_ref, tab_ref, msk_ref,
         o_ref, sout_ref, st_ref, la_ref) = refs
        s0_ref = None
    else:
        (q_ref, k_ref, v_ref, r_ref, lr_ref, wg_ref, bg_ref, gn_ref, tab_ref, msk_ref, s0_ref,
         o_ref, sout_ref, st_ref, la_ref) = refs
    t = pl.program_id(2)
    n_tiles = pl.num_programs(2)
    C = CHUNK
    n_levels = msk_ref.shape[0]

    @pl.when(t == 0)
    def _():
        for h in range(heads):
            if zero_init:
                st_ref[h] = jnp.zeros((dv, dk), F32)
            else:
                st_ref[h] = s0_ref[0, h].T

    z = _dot(lr_ref[...].astype(BF16), wg_ref[...]) + bg_ref[...]
    la_ref[...] = _log_sigmoid(z) * (1.0 / GLA_TAU)
    tab = tab_ref[...]

    def chunk(c, carry):
        r0 = pl.multiple_of(c * C, C)
        la = la_ref[pl.ds(r0, C), :]
        sums = None
        for part in _split_bf16(la, 2):
            p = _dot(tab, part)
            sums = p if sums is None else sums + p
        decay = jnp.exp(sums)
        for h in range(heads):
            ks = slice(h * dk, (h + 1) * dk)
            vs = slice(h * dv, (h + 1) * dv)
            q = q_ref[pl.ds(r0, C), ks].astype(F32)
            k = k_ref[pl.ds(r0, C), ks].astype(F32)
            v = v_ref[pl.ds(r0, C), vs]
            att = msk_ref[0] * _nt(q.astype(BF16), k.astype(BF16))
            for lvl in range(1, n_levels):
                e = decay[lvl * C:(lvl + 1) * C, ks]
                att = att + msk_ref[lvl] * _nt((q * e).astype(BF16), (k * e).astype(BF16))
            e_in = decay[0:C, ks]
            st = st_ref[h]
            o = _dot(att.astype(BF16), v) + _nt((q * e_in).astype(BF16), st.astype(BF16))
            k_end = (k * decay[n_levels * C:(n_levels + 1) * C, ks]).astype(BF16)
            st_ref[h] = st * e_in[C - 1:C, :] + _tn(v, k_end)
            on = o * lax.rsqrt(jnp.mean(o * o, axis=-1, keepdims=True) + EPS) * gn_ref[...]
            gate = _silu(r_ref[pl.ds(r0, C), vs].astype(F32))
            o_ref[pl.ds(r0, C), vs] = (on * gate).astype(o_ref.dtype)
        return carry

    lax.fori_loop(0, n_chunks, chunk, 0, unroll=2 if n_chunks % 2 == 0 else 1)

    @pl.when(t == n_tiles - 1)
    def _():
        for h in range(heads):
            sout_ref[0, h] = st_ref[h].T


def gla(proj, lr, wg, bg, gn, s0, *, nseq, seq_len, n_heads, dk, dv):
    rows = proj.shape[0]
    hg = min(4, n_heads)
    n_groups = n_heads // hg
    tt = min(512, seq_len)
    n_tiles = seq_len // tt
    tab, msk = _gla_tables()
    tab = jnp.asarray(tab, BF16)
    msk = jnp.asarray(msk, F32)
    qw, vw = hg * dk, hg * dv
    k_off = (n_heads * dk) // qw
    v_off = (2 * n_heads * dk) // vw
    r_off = (2 * n_heads * dk + n_heads * dv) // vw
    row = lambda b, g, t: b * n_tiles + t
    in_specs = [
        pl.BlockSpec((tt, qw), lambda b, g, t: (row(b, g, t), g)),
        pl.BlockSpec((tt, qw), lambda b, g, t: (row(b, g, t), k_off + g)),
        pl.BlockSpec((tt, vw), lambda b, g, t: (row(b, g, t), v_off + g)),
        pl.BlockSpec((tt, vw), lambda b, g, t: (row(b, g, t), r_off + g)),
        pl.BlockSpec((tt, LANES), lambda b, g, t: (row(b, g, t), 0)),
        pl.BlockSpec((LANES, qw), lambda b, g, t: (0, g)),
        pl.BlockSpec((1, qw), lambda b, g, t: (0, g)),
        pl.BlockSpec((1, dv), lambda b, g, t: (0, 0)),
        pl.BlockSpec(tab.shape, lambda b, g, t: (0, 0)),
        pl.BlockSpec(msk.shape, lambda b, g, t: (0, 0, 0)),
    ]
    args = [proj, proj, proj, proj, lr, wg, bg, gn, tab, msk]
    zero_init = s0 is None
    if not zero_init:
        in_specs.append(pl.BlockSpec((1, hg, dk, dv), lambda b, g, t: (b, g, 0, 0)))
        args.append(s0)
    est = (2 * tt * (2 * qw + 2 * vw) * 2 + 2 * tt * LANES * 4 + 2 * tt * vw * 2
           + 4 * hg * dk * dv * 4 + hg * dk * dv * 4 + tt * qw * 4 + 4 * tab.shape[0] * qw * 4)
    kern = functools.partial(_gla_kernel, heads=hg, dk=dk, dv=dv, n_chunks=tt // CHUNK, zero_init=zero_init)
    o, s_out = pl.pallas_call(
        kern,
        grid=(nseq, n_groups, n_tiles),
        in_specs=in_specs,
        out_specs=[pl.BlockSpec((tt, vw), lambda b, g, t: (row(b, g, t), g)),
                   pl.BlockSpec((1, hg, dk, dv), lambda b, g, t: (b, g, 0, 0))],
        out_shape=[jax.ShapeDtypeStruct((rows, n_heads * dv), BF16),
                   jax.ShapeDtypeStruct((nseq, n_heads, dk, dv), F32)],
        scratch_shapes=[pltpu.VMEM((hg, dv, dk), F32), pltpu.VMEM((tt, qw), F32)],
        compiler_params=_params(("parallel", "parallel", "arbitrary"), est),
        name="gla",
    )(*args)
    return o, s_out


def _mem_attn_kernel(q_ref, k_ref, v_ref, o_ref, *, n_heads):
    for h in range(n_heads):
        cs = slice(h * MEM_DH, (h + 1) * MEM_DH)
        s = _nt(q_ref[:, cs], k_ref[0, :, cs].astype(BF16))
        p = jnp.exp(s - jnp.max(s, axis=-1, keepdims=True))
        l = jnp.sum(p, axis=-1, keepdims=True)
        o = _dot(p.astype(BF16), v_ref[0, :, cs].astype(BF16))
        o_ref[:, cs] = (o / l).astype(o_ref.dtype)


def mem_attn(qsrc, q_col_block, mk, mv, *, seq_len):
    rows = qsrc.shape[0]
    nseq, n_mem, mq = mk.shape
    tm = min(1024, seq_len)
    tiles_per_seq = seq_len // tm
    est = 2 * tm * mq * 2 * 2 + 2 * 2 * n_mem * mq * 4 + 6 * tm * n_mem * 4
    return pl.pallas_call(
        functools.partial(_mem_attn_kernel, n_heads=mq // MEM_DH),
        grid=(rows // tm,),
        in_specs=[pl.BlockSpec((tm, mq), lambda i: (i, q_col_block)),
                  pl.BlockSpec((1, n_mem, mq), lambda i: (i // tiles_per_seq, 0, 0)),
                  pl.BlockSpec((1, n_mem, mq), lambda i: (i // tiles_per_seq, 0, 0))],
        out_specs=pl.BlockSpec((tm, mq), lambda i: (i, 0)),
        out_shape=jax.ShapeDtypeStruct((rows, mq), BF16),
        compiler_params=_params(("parallel",), est),
        name="mem_attn",
    )(qsrc, mk, mv)


def _cumsum_kernel(lf_ref, c_ref, ct_ref, crow_ref, ccol_ref):
    j = pl.program_id(1)
    tb = lf_ref.shape[1]

    @pl.when(j == 0)
    def _():
        crow_ref[...] = jnp.zeros_like(crow_ref)
        ccol_ref[...] = jnp.zeros_like(ccol_ref)

    r = lax.broadcasted_iota(jnp.int32, (tb, tb), 0)
    s = lax.broadcasted_iota(jnp.int32, (tb, tb), 1)
    lower = jnp.where(s <= r, 1.0, 0.0).astype(BF16)
    upper = jnp.where(r <= s, 1.0, 0.0).astype(BF16)
    c = crow_ref[0:1, :]
    ct = ccol_ref[:, 0:1]
    for part in _split_bf16(lf_ref[0], 3):
        c = c + _dot(lower, part)
        ct = ct + _tn(part, upper)
    c_ref[0] = c
    ct_ref[0] = ct
    crow_ref[...] = jnp.broadcast_to(c[tb - 1:tb, :], crow_ref.shape)
    ccol_ref[...] = jnp.broadcast_to(ct[:, tb - 1:tb], ccol_ref.shape)


def cumsum_time(lf):
    nseq, L, w = lf.shape
    tb = L if L <= 2048 else 512
    est = 4 * tb * w * 4 + 4 * tb * tb * 4 + 8 * tb * w * 4
    return pl.pallas_call(
        _cumsum_kernel,
        grid=(nseq, L // tb),
        in_specs=[pl.BlockSpec((1, tb, w), lambda b, j: (b, j, 0))],
        out_specs=[pl.BlockSpec((1, tb, w), lambda b, j: (b, j, 0)),
                   pl.BlockSpec((1, w, tb), lambda b, j: (b, 0, j))],
        out_shape=[jax.ShapeDtypeStruct((nseq, L, w), F32), jax.ShapeDtypeStruct((nseq, w, L), F32)],
        scratch_shapes=[pltpu.VMEM((SUBLANES, w), F32), pltpu.VMEM((w, LANES), F32)],
        compiler_params=_params(("parallel", "arbitrary"), est),
        name="cumsum_time",
    )(lf)


def _pick_lane(x, lane):
    ids = lax.broadcasted_iota(jnp.int32, x.shape, 1)
    return jnp.sum(jnp.where(ids == lane, x, 0.0), axis=-1, keepdims=True)


LOG2E = 1.4426950408889634
FOX_Q_BLOCK = 2048
FOX_ROW_CHAIN = 512
FOX_K_BLOCK = 1024
FOX_AUG = 3


def _fox_kernel(q_ref, k_ref, v_ref, c_ref, o_ref, augq_ref, augk_ref, vaug_ref, *chain_refs, tk, tkf):
    h = pl.program_id(1)
    qi = pl.program_id(2)
    seq_len, dh = k_ref.shape
    tq = q_ref.shape[0]
    n_sub = tq // tk
    chains = [chain_refs[3 * i:3 * i + 3] for i in range(n_sub)]

    @pl.when(qi == 0)
    def _():
        def build(r, carry):
            r0 = pl.multiple_of(r * tk, tk)
            cc = _pick_lane(c_ref[0, pl.ds(r0, tk), :], h) * LOG2E
            lane = lax.broadcasted_iota(jnp.int32, (tk, dh), 1)
            aq = jnp.where(lane < 2 * FOX_AUG, 1.0, 0.0)
            ak = aq
            for n, part in enumerate(_split_bf16(cc, FOX_AUG)):
                aq = jnp.where(lane == n, part.astype(F32), aq)
                ak = jnp.where(lane == FOX_AUG + n, -part.astype(F32), ak)
            augq_ref[pl.ds(r0, tk), :] = aq.astype(BF16)
            augk_ref[pl.ds(r0, tk), :] = ak.astype(BF16)
            vaug_ref[pl.ds(r0, tk), 0:dh] = v_ref[pl.ds(r0, tk), :]
            vaug_ref[pl.ds(r0, tk), dh:2 * dh] = jnp.ones((tk, dh), BF16)
            return carry

        lax.fori_loop(0, seq_len // tk, build, 0)

    q0 = pl.multiple_of(qi * tq, tq)
    for sub, (qa_ref, m_ref, acc_ref) in enumerate(chains):
        qa_ref[:, 0:dh] = q_ref[sub * tk:(sub + 1) * tk, :]
        qa_ref[:, dh:2 * dh] = augq_ref[pl.ds(pl.multiple_of(q0 + sub * tk, tk), tk), :]
        m_ref[...] = jnp.full_like(m_ref, -jnp.inf)
        acc_ref[...] = jnp.zeros_like(acc_ref)

    def scores(sub, k0, width):
        ka = jnp.concatenate([k_ref[pl.ds(k0, width), :], augk_ref[pl.ds(k0, width), :]], axis=1)
        return _nt(chains[sub][0][...], ka)

    def softmax_update(sub, s, masked):
        m_ref = chains[sub][1]
        if masked:
            row = lax.broadcasted_iota(jnp.int32, s.shape, 0)
            col = lax.broadcasted_iota(jnp.int32, s.shape, 1)
            s = jnp.where(col <= row, s, -jnp.inf)
        blocks = [s[:, j * LANES:(j + 1) * LANES] for j in range(s.shape[1] // LANES)]
        mx = blocks[0]
        for blk in blocks[1:]:
            mx = jnp.maximum(mx, blk)
        m_old = m_ref[...]
        m_new = jnp.maximum(m_old, jnp.max(mx, axis=-1, keepdims=True))
        m_ref[...] = m_new
        alpha = jnp.exp2(m_old - m_new)
        pr = jnp.concatenate([jnp.exp2(blk - m_new) for blk in blocks], axis=1)
        return alpha, pr.astype(BF16)

    def accumulate(sub, k0, width, alpha, pr):
        acc_ref = chains[sub][2]
        acc_ref[...] = (jnp.concatenate([alpha] * (2 * dh // LANES), axis=1) * acc_ref[...]
                        + _dot(pr, vaug_ref[pl.ds(k0, width), :]))

    def step(jobs, width):
        ss = [scores(sub, k0, width) for sub, k0, _ in jobs]
        ps = [softmax_update(sub, s, masked) for (sub, _, masked), s in zip(jobs, ss)]
        for (sub, k0, _), (alpha, pr) in zip(jobs, ps):
            accumulate(sub, k0, width, alpha, pr)

    def full_chunk(j, carry):
        k0 = pl.multiple_of(j * tkf, tkf)
        step([(sub, k0, False) for sub in range(n_sub)], tkf)
        return carry

    lax.fori_loop(0, qi * (tq // tkf), full_chunk, 0)
    step([(sub, pl.multiple_of(q0 + d * tk, tk), sub == d) for d in range(n_sub) for sub in range(d, n_sub)], tk)
    for sub, (_, _, acc_ref) in enumerate(chains):
        o_ref[sub * tk:(sub + 1) * tk, :] = (acc_ref[:, 0:dh] / acc_ref[:, dh:2 * dh]).astype(o_ref.dtype)


def fox_prompt(qsrc, kbf, vbf, c, *, nseq, seq_len, n_heads, dh):
    rows = qsrc.shape[0]
    tq = min(FOX_Q_BLOCK, seq_len)
    tk = min(FOX_ROW_CHAIN, tq)
    tkf = min(FOX_K_BLOCK, tq)
    nq = seq_len // tq
    est = (2 * tq * dh * 2 * 2 + 2 * 2 * seq_len * dh * 2 + 2 * seq_len * LANES * 4 + 4 * seq_len * dh * 2
           + tq * 2 * dh * 2 + 3 * tq * LANES * 4 + (tq // tk) * 4 * tk * tkf * 4)
    return pl.pallas_call(
        functools.partial(_fox_kernel, tk=tk, tkf=tkf),
        grid=(nseq, n_heads, nq),
        in_specs=[
            pl.BlockSpec((tq, dh), lambda b, h, qi: (b * nq + qi, h)),
            pl.BlockSpec((seq_len, dh), lambda b, h, qi: (b, h)),
            pl.BlockSpec((seq_len, dh), lambda b, h, qi: (b, h)),
            pl.BlockSpec((1, seq_len, LANES), lambda b, h, qi: (b, 0, 0)),
        ],
        out_specs=pl.BlockSpec((tq, dh), lambda b, h, qi: (b * nq + qi, h)),
        out_shape=jax.ShapeDtypeStruct((rows, n_heads * dh), BF16),
        scratch_shapes=[pltpu.VMEM((seq_len, dh), BF16), pltpu.VMEM((seq_len, dh), BF16),
                        pltpu.VMEM((seq_len, 2 * dh), BF16)]
        + [pltpu.VMEM((tk, 2 * dh), BF16), pltpu.VMEM((tk, LANES), F32),
           pltpu.VMEM((tk, 2 * dh), F32)] * (tq // tk),
        compiler_params=_params(("parallel", "parallel", "arbitrary"), est),
        name="fox_prompt",
    )(qsrc, kbf, vbf, c)


FOX_DECODE_KEYS = 512


def _fox_decode_kernel(q_ref, kc_ref, vc_ref, kn_ref, vn_ref, c_ref, ctc_ref, ctn_ref, o_ref,
                       m_ref, l_ref, acc_ref, *, n_heads, dh):
    j = pl.program_id(1)
    ts = q_ref.shape[0]
    pc = kc_ref.shape[1] // n_heads

    def cached(h):
        cs = slice(h * dh, (h + 1) * dh)
        cq = _pick_lane(c_ref[0], h) * LOG2E
        kc = kc_ref[0, pl.ds(h, pc, stride=n_heads), :].astype(BF16)
        vc = vc_ref[0, pl.ds(h, pc, stride=n_heads), :].astype(BF16)
        return cs, cq, _nt(q_ref[:, cs], kc) + cq - ctc_ref[0, h] * LOG2E, vc

    @pl.when(j == 0)
    def _():
        row = lax.broadcasted_iota(jnp.int32, (ts, ts), 0)
        col = lax.broadcasted_iota(jnp.int32, (ts, ts), 1)
        for h in range(n_heads):
            cs, cq, s, vc = cached(h)
            sn = _nt(q_ref[:, cs], kn_ref[:, cs]) + cq - ctn_ref[0, h] * LOG2E
            s = jnp.concatenate([s, jnp.where(col <= row, sn, -jnp.inf)], axis=1)
            m = jnp.max(s, axis=-1, keepdims=True)
            p = jnp.exp2(s - m)
            m_ref[h] = m
            l_ref[h] = jnp.sum(p, axis=-1, keepdims=True)
            acc_ref[h] = _dot(p.astype(BF16), jnp.concatenate([vc, vn_ref[:, cs]], axis=0))

    @pl.when(j > 0)
    def _():
        for h in range(n_heads):
            _, _, s, vc = cached(h)
            m_old = m_ref[h]
            m_new = jnp.maximum(m_old, jnp.max(s, axis=-1, keepdims=True))
            alpha = jnp.exp2(m_old - m_new)
            p = jnp.exp2(s - m_new)
            l_ref[h] = alpha * l_ref[h] + jnp.sum(p, axis=-1, keepdims=True)
            acc_ref[h] = alpha * acc_ref[h] + _dot(p.astype(BF16), vc)
            m_ref[h] = m_new

    @pl.when(j == pl.num_programs(1) - 1)
    def _():
        for h in range(n_heads):
            cs = slice(h * dh, (h + 1) * dh)
            o_ref[:, cs] = (acc_ref[h] / l_ref[h]).astype(o_ref.dtype)


def fox_decode(qsrc, kcache, vcache, kbf, vbf, c, ct4, *, n_heads, dh):
    nseq = kcache.shape[0]
    past = kcache.shape[1] // n_heads
    rows = qsrc.shape[0]
    ts = rows // nseq
    w = n_heads * dh
    pc = _pick_tile(past, FOX_DECODE_KEYS, LANES)
    ct_new = ct4[..., past:]
    est = (2 * 2 * pc * w * 4 + 8 * ts * w * 2 + 2 * ts * LANES * 4 + n_heads * ts * (dh + 2 * LANES) * 4
           + 8 * ts * pc * 4 + 4 * pc * dh * 4)
    return pl.pallas_call(
        functools.partial(_fox_decode_kernel, n_heads=n_heads, dh=dh),
        grid=(nseq, past // pc),
        in_specs=[
            pl.BlockSpec((ts, w), lambda b, j: (b, 0)),
            pl.BlockSpec((1, pc * n_heads, dh), lambda b, j: (b, j, 0)),
            pl.BlockSpec((1, pc * n_heads, dh), lambda b, j: (b, j, 0)),
            pl.BlockSpec((ts, w), lambda b, j: (b, 0)),
            pl.BlockSpec((ts, w), lambda b, j: (b, 0)),
            pl.BlockSpec((1, ts, LANES), lambda b, j: (b, past // ts, 0)),
            pl.BlockSpec((1, n_heads, 1, pc), lambda b, j: (b, 0, 0, j)),
            pl.BlockSpec((1, n_heads, 1, ts), lambda b, j: (b, 0, 0, 0)),
        ],
        out_specs=pl.BlockSpec((ts, w), lambda b, j: (b, 0)),
        out_shape=jax.ShapeDtypeStruct((rows, w), BF16),
        scratch_shapes=[pltpu.VMEM((n_heads, ts, 1), F32), pltpu.VMEM((n_heads, ts, 1), F32),
                        pltpu.VMEM((n_heads, ts, dh), F32)],
        compiler_params=_params(("parallel", "arbitrary"), est),
        name="fox_decode",
    )(qsrc, kcache, vcache, kbf, vbf, c, ct4, ct_new)


def _pad_cols(x, width):
    return jnp.pad(x, ((0, 0), (0, width - x.shape[1])))


def _ffn_block(x, xg, ssq, w_up, w_down, layer, w_conv, b_conv, prev, next_gains, *, nseq, seq_len):
    tf = _pick_tile(w_down.shape[1], 256, LANES)
    h, tails = ffn_up(xg, ssq, w_up, layer, w_conv, b_conv, prev, nseq=nseq, seq_len=seq_len, tf=tf)
    outs = matmul([h], w_down, [F32], tm=512, tn=512, w_layer=layer, res=x, next_gains=next_gains, name="ffn_down")
    return outs, tails


def kernel(x_prompt, x_sample, state_gla, cache_fox_k, cache_fox_v, cache_fox_logf, cache_mem_k, cache_mem_v,
           state_ffn_conv, mem_prompt, g_attn, g_ffn, g_mem, w_mem_kv, w_in_a, w_gate_a, b_gate_a, g_gla,
           w_in_b, g_kv, w_kv, b_forget, w_out, w_up, w_conv, b_conv, w_down, g_final):
    n_p, t_p, d = x_prompt.shape
    n_s, t_s, _ = x_sample.shape
    depth = g_attn.shape[0]
    n_a = state_gla.shape[0]
    _, _, gla_heads, dk, dv = state_gla.shape
    _, past, fox_heads, dh = cache_fox_k.shape
    n_mem, mq = cache_mem_k.shape[2:]
    a_q, a_v, b_q = gla_heads * dk, gla_heads * dv, fox_heads * dh

    def mixer_gains(l):
        return jnp.stack([g_attn[l]] + ([g_kv] if l == n_a else []))

    streams = []
    for x3, nseq, seq_len in ((x_prompt, n_p, t_p), (x_sample, n_s, t_s)):
        x = x3.reshape(nseq * seq_len, d)
        streams.append(dict(x=x, nseq=nseq, seq_len=seq_len, normed=rmsnorm(x, mixer_gains(0), BF16), ssq=None))
    mem_rows = mem_prompt.reshape(n_p * n_mem, d)
    mk_out, mv_out, conv_out = [], [], [[], []]
    gla_out = [[], []]
    fox = [None, None]

    w_mem_bf, w_in_a_bf, w_in_b_bf, w_kv_bf = (w.astype(BF16) for w in (w_mem_kv, w_in_a, w_in_b, w_kv))
    w_out_bf, w_up_bf, w_down_bf = (w.astype(BF16) for w in (w_out, w_up, w_down))
    mem_scale = jnp.full((1, mq), MEM_DH ** -0.5, F32)
    main = 2 * a_q + 2 * a_v

    for l in range(depth):
        (mem_n,) = rmsnorm(mem_rows, g_mem[l][None], BF16)
        (mkv,) = matmul([mem_n], w_mem_bf, [F32], tm=512, tn=1024, w_layer=l, name="mem_kv")
        mk_p = mkv[:, :mq].reshape(n_p, n_mem, mq)
        mv_p = mkv[:, mq:].reshape(n_p, n_mem, mq)
        mk_out.append(mk_p)
        mv_out.append(mv_p)
        mem_kv = [(mk_p, mv_p), (cache_mem_k[l], cache_mem_v[l])]

        if l < n_a:
            w_lr = _pad_cols(w_in_a[l][:, main:main + GLA_RANK], LANES).astype(BF16)
            w_qm = w_in_a[l][:, main + GLA_RANK:].astype(BF16)
            col_scale = jnp.concatenate([jnp.full((a_q,), dk ** -0.5, F32), jnp.ones((a_q + 2 * a_v,), F32)])[None]
            wg = jnp.pad(w_gate_a[l], ((0, LANES - GLA_RANK), (0, 0))).astype(BF16)
            bg = b_gate_a[l][None]
            gn = g_gla[l][None]
        else:
            j = l - n_a
            col_scale = jnp.concatenate([jnp.full((b_q,), LOG2E * dh ** -0.5, F32), mem_scale[0]])[None]
            if l == n_a:
                w_kv_f = _pad_cols(w_kv[:, 2 * b_q:], LANES).astype(BF16)
                b_f = _pad_cols(b_forget[None], LANES)
        ffn_next_gains = mixer_gains(l + 1) if l + 1 < depth else None

        for si, st in enumerate(streams):
            x, nseq, seq_len = st["x"], st["nseq"], st["seq_len"]
            mk, mv = mem_kv[si]
            xn = st["normed"][0]
            norm = dict(row_ssq=st["ssq"], norm_dim=d)
            if l < n_a:
                (proj,) = matmul([xn], w_in_a_bf, [BF16], tm=1024, tn=1024, w_layer=l, n=main, scale=col_scale,
                                 name="in_proj_a", **norm)
                (qm,) = matmul([xn], w_qm, [BF16], tm=1024, tn=1024, scale=mem_scale, name="in_proj_a_mem", **norm)
                (lr,) = matmul([xn], w_lr, [F32], tm=1024, tn=LANES, name="in_proj_a_gate", **norm)
                s0 = None if si == 0 else state_gla[l]
                o, s_new = gla(proj, lr, wg, bg, gn, s0, nseq=nseq, seq_len=seq_len,
                               n_heads=gla_heads, dk=dk, dv=dv)
                gla_out[si].append(s_new)
                mo = mem_attn(qm, 0, mk, mv, seq_len=seq_len)
            else:
                if l == n_a:
                    xkv = st["normed"][1]
                    k32, kbf = matmul([xkv], w_kv_bf, [F32, BF16], tm=1024, tn=1024, n=b_q, head_dims=[dh, None],
                                      name="k_proj", **norm)
                    v32, vbf = matmul([xkv], w_kv_bf, [F32, BF16], tm=1024, tn=1024, w_col0=b_q, n=b_q,
                                      head_dims=[dh, None], name="v_proj", **norm)
                    (lf,) = matmul([xkv], w_kv_f, [F32], tm=1024, tn=LANES, bias=b_f, act="log_sigmoid",
                                   name="kv_proj_forget", **norm)
                    lf3 = lf.reshape(nseq, seq_len, LANES)
                    if si == 0:
                        lf_all = lf3
                    else:
                        lf_all = jnp.concatenate(
                            [jnp.pad(cache_fox_logf, ((0, 0), (0, 0), (0, LANES - fox_heads))), lf3], axis=1)
                    c, ct = cumsum_time(lf_all)
                    ct4 = ct[:, :fox_heads].reshape(nseq, fox_heads, 1, ct.shape[-1])
                    fox[si] = dict(k32=k32, v32=v32, kbf=kbf, vbf=vbf, lf=lf3[:, :, :fox_heads], c=c, ct4=ct4)
                fx = fox[si]
                (proj,) = matmul([xn], w_in_b_bf, [BF16], tm=1024, tn=1024, w_layer=j, scale=col_scale,
                                 name="in_proj_b", **norm)
                kb, vb = fx["kbf"], fx["vbf"]
                if si == 0:
                    o = fox_prompt(proj, kb, vb, fx["c"], nseq=nseq, seq_len=seq_len,
                                   n_heads=fox_heads, dh=dh)
                else:
                    o = fox_decode(proj, cache_fox_k.reshape(nseq, past * fox_heads, dh),
                                   cache_fox_v.reshape(nseq, past * fox_heads, dh),
                                   kb, vb, fx["c"], fx["ct4"], n_heads=fox_heads, dh=dh)
                mo = mem_attn(proj, b_q // mq, mk, mv, seq_len=seq_len)
            x, xg, ssq = matmul([o, mo], w_out_bf, [F32], tm=1024, tn=512, w_layer=l, res=x,
                                next_gains=g_ffn[l][None], name="out_proj")
            prev = None if si == 0 else state_ffn_conv[l]
            outs, tails = _ffn_block(x, xg, ssq, w_up_bf, w_down_bf, l, w_conv[l], b_conv[l][None], prev,
                                     ffn_next_gains, nseq=nseq, seq_len=seq_len)
            conv_out[si].append(tails)
            st["x"] = outs[0]
            if ffn_next_gains is not None:
                st["normed"], st["ssq"] = outs[1:-1], outs[-1]

    ys = []
    for st in streams:
        (y,) = rmsnorm(st["x"], g_final[None], F32)
        ys.append(y)
    fp, fs = fox
    return (ys[0].reshape(n_p, t_p, d), ys[1].reshape(n_s, t_s, d),
            jnp.stack(gla_out[0]), jnp.stack(gla_out[1]),
            fp["k32"].reshape(n_p, t_p, fox_heads, dh), fp["v32"].reshape(n_p, t_p, fox_heads, dh), fp["lf"],
            fs["k32"].reshape(n_s, t_s, fox_heads, dh), fs["v32"].reshape(n_s, t_s, fox_heads, dh), fs["lf"],
            jnp.stack(mk_out), jnp.stack(mv_out),
            jnp.stack(conv_out[0]), jnp.stack(conv_out[1]))
```

```python
import functools

import numpy as np
import jax
import jax.numpy as jnp
from jax import lax
from jax.experimental import pallas as pl
from jax.experimental.pallas import tpu as pltpu

F32 = jnp.float32
BF16 = jnp.bfloat16

CHUNK = 64
GLA_RANK = 16
GLA_TAU = 16.0
MEM_DH = 256
CONV_W = 3
EPS = 1e-6

LANES = 128
SUBLANES = 8
VMEM_PHYSICAL_BYTES = 64 * 1024 * 1024
VMEM_HEADROOM_BYTES = 6 * 1024 * 1024


def _vmem_limit(estimate_bytes):
    want = int(estimate_bytes * 1.25) + 4 * 1024 * 1024
    return int(min(max(want, 16 * 1024 * 1024), VMEM_PHYSICAL_BYTES - VMEM_HEADROOM_BYTES))


def _params(semantics, estimate_bytes):
    return pltpu.CompilerParams(dimension_semantics=semantics, vmem_limit_bytes=_vmem_limit(estimate_bytes))


def _pick_tile(dim, preferred, unit):
    best = None
    for t in range(unit, min(dim, preferred) + 1, unit):
        if dim % t == 0:
            best = t
    assert best is not None, (dim, preferred, unit)
    return best


def _nt(a, b):
    return lax.dot_general(a, b, (((1,), (1,)), ((), ())), preferred_element_type=F32)


def _tn(a, b):
    return lax.dot_general(a, b, (((0,), (0,)), ((), ())), preferred_element_type=F32)


def _dot(a, b):
    return jnp.dot(a, b, preferred_element_type=F32)


def _split_bf16(x, parts):
    out = []
    rem = x
    for _ in range(parts):
        hi = rem.astype(BF16)
        out.append(hi)
        rem = rem - hi.astype(F32)
    return out


def _log_sigmoid(z):
    return jnp.minimum(z, 0.0) - jnp.log(1.0 + jnp.exp(-jnp.abs(z)))


def _silu(z):
    half = 0.5 * z
    return half + half * jnp.tanh(half)


def _rmsnorm_kernel(x_ref, g_ref, *o_refs):
    x = x_ref[...]
    y = x * lax.rsqrt(jnp.mean(x * x, axis=-1, keepdims=True) + EPS)
    for i, o_ref in enumerate(o_refs):
        o_ref[...] = (y * g_ref[i:i + 1, :]).astype(o_ref.dtype)


def rmsnorm(x, gains, out_dtype):
    rows, d = x.shape
    n = gains.shape[0]
    tm = min(256, rows)
    out_bytes = jnp.dtype(out_dtype).itemsize
    est = 2 * tm * d * 4 + n * 2 * tm * d * out_bytes + 2 * tm * d * 4
    outs = pl.pallas_call(
        _rmsnorm_kernel,
        grid=(rows // tm,),
        in_specs=[pl.BlockSpec((tm, d), lambda i: (i, 0)),
                  pl.BlockSpec((n, d), lambda i: (0, 0))],
        out_specs=[pl.BlockSpec((tm, d), lambda i: (i, 0)) for _ in range(n)],
        out_shape=[jax.ShapeDtypeStruct((rows, d), out_dtype) for _ in range(n)],
        compiler_params=_params(("parallel",), est),
        name="rmsnorm",
    )(x, gains)
    return list(outs)


def _row_rsqrt(ssq, norm_dim):
    total = ssq[:, 0:LANES]
    for j in range(1, ssq.shape[1] // LANES):
        total = total + ssq[:, j * LANES:(j + 1) * LANES]
    return lax.rsqrt(total * (1.0 / norm_dim) + EPS)


def _tile_lanes(v, width):
    return jnp.concatenate([v] * (width // LANES), axis=1) if width > LANES else v


def _matmul_kernel(*refs, n_parts, norm_dim, has_scale, has_bias, has_res, act, n_out, n_gains):
    a_refs = refs[:n_parts]
    w_refs = refs[n_parts:2 * n_parts]
    pos = 2 * n_parts
    ssq_ref = scale_ref = bias_ref = res_ref = gains_ref = None
    if norm_dim:
        ssq_ref = refs[pos]; pos += 1
    if has_scale:
        scale_ref = refs[pos]; pos += 1
    if has_bias:
        bias_ref = refs[pos]; pos += 1
    if has_res:
        res_ref = refs[pos]; pos += 1
    if n_gains:
        gains_ref = refs[pos]; pos += 1
    o_refs = refs[pos:pos + n_out]
    pos += n_out
    xg_refs = refs[pos:pos + n_gains]
    ssq_out_ref = refs[pos + n_gains] if n_gains else None

    y = _dot(a_refs[0][...], w_refs[0][...])
    for a_ref, w_ref in zip(a_refs[1:], w_refs[1:]):
        y = y + _dot(a_ref[...], w_ref[...])
    if norm_dim:
        y = y * _tile_lanes(_row_rsqrt(ssq_ref[...], norm_dim), y.shape[1])
    if has_scale:
        y = y * scale_ref[...]
    if has_bias:
        y = y + bias_ref[...]
    if act == "log_sigmoid":
        y = _log_sigmoid(y)
    if has_res:
        y = y + res_ref[...]
    for o_ref in o_refs:
        if len(o_ref.shape) == 3:
            dh = o_ref.shape[2]
            for hh in range(o_ref.shape[1]):
                o_ref[:, hh, :] = y[:, hh * dh:(hh + 1) * dh].astype(o_ref.dtype)
        else:
            o_ref[...] = y.astype(o_ref.dtype)
    for gi, xg_ref in enumerate(xg_refs):
        xg_ref[...] = (y * gains_ref[gi:gi + 1, :]).astype(xg_ref.dtype)
    if n_gains:
        part = jnp.broadcast_to(jnp.sum(y * y, axis=-1, keepdims=True), ssq_out_ref.shape)
        j = pl.program_id(1)

        @pl.when(j == 0)
        def _():
            ssq_out_ref[...] = part

        @pl.when(j > 0)
        def _():
            ssq_out_ref[...] += part


def matmul(a_parts, w, out_dtypes, *, tm, tn, w_layer=None, w_col0=0, n=None, row_ssq=None, norm_dim=None,
           scale=None, bias=None, res=None, act=None, head_dims=None, next_gains=None, name="matmul"):
    rows = a_parts[0].shape[0]
    kdim, n_w = w.shape[-2:]
    n = n_w if n is None else n
    tm = _pick_tile(rows, tm, SUBLANES)
    tn = _pick_tile(n, tn, LANES)
    assert sum(a.shape[1] for a in a_parts) == kdim and w_col0 % tn == 0 and w_col0 + n <= n_w
    col0 = w_col0 // tn
    a_specs, w_specs = [], []
    start = 0
    for a in a_parts:
        kp = a.shape[1]
        assert start % kp == 0, "a part must start at a multiple of its own width in W"
        a_specs.append(pl.BlockSpec((tm, kp), lambda i, j: (i, 0)))
        if w.ndim == 3:
            w_specs.append(pl.BlockSpec((None, kp, tn), lambda i, j, b=start // kp: (w_layer, b, col0 + j)))
        else:
            w_specs.append(pl.BlockSpec((kp, tn), lambda i, j, b=start // kp: (b, col0 + j)))
        start += kp
    in_specs = a_specs + w_specs
    args = list(a_parts) + [w] * len(a_parts)
    if row_ssq is not None:
        in_specs.append(pl.BlockSpec((tm, row_ssq.shape[1]), lambda i, j: (i, 0)))
        args.append(row_ssq)
    for extra in (scale, bias):
        if extra is not None:
            in_specs.append(pl.BlockSpec((1, tn), lambda i, j: (0, j)))
            args.append(extra)
    if res is not None:
        in_specs.append(pl.BlockSpec((tm, tn), lambda i, j: (i, j)))
        args.append(res)
    n_gains = 0 if next_gains is None else next_gains.shape[0]
    if n_gains:
        in_specs.append(pl.BlockSpec((n_gains, tn), lambda i, j: (0, j)))
        args.append(next_gains)
    out_bytes = sum(jnp.dtype(d).itemsize for d in out_dtypes) + 2 * n_gains
    est = (2 * tm * kdim * 2 + 2 * kdim * tn * 2 + 2 * tm * tn * out_bytes
           + (2 * tm * tn * 4 if res is not None else 0) + 2 * tm * tn * 4
           + (2 * tm * row_ssq.shape[1] * 4 if row_ssq is not None else 0))
    kern = functools.partial(
        _matmul_kernel, n_parts=len(a_parts), norm_dim=norm_dim if row_ssq is not None else None,
        has_scale=scale is not None, has_bias=bias is not None, has_res=res is not None, act=act,
        n_out=len(out_dtypes), n_gains=n_gains)
    head_dims = head_dims or [None] * len(out_dtypes)
    out_specs, out_shape = [], []
    for dt, dh in zip(out_dtypes, head_dims):
        if dh is None:
            out_specs.append(pl.BlockSpec((tm, tn), lambda i, j: (i, j)))
            out_shape.append(jax.ShapeDtypeStruct((rows, n), dt))
        else:
            assert tn % (SUBLANES * dh) == 0 or tn == n
            out_specs.append(pl.BlockSpec((tm, tn // dh, dh), lambda i, j: (i, j, 0)))
            out_shape.append(jax.ShapeDtypeStruct((rows, n // dh, dh), dt))
    for _ in range(n_gains):
        out_specs.append(pl.BlockSpec((tm, tn), lambda i, j: (i, j)))
        out_shape.append(jax.ShapeDtypeStruct((rows, n), BF16))
    if n_gains:
        out_specs.append(pl.BlockSpec((tm, LANES), lambda i, j: (i, 0)))
        out_shape.append(jax.ShapeDtypeStruct((rows, LANES), F32))
    outs = pl.pallas_call(
        kern,
        grid=(rows // tm, n // tn),
        in_specs=in_specs,
        out_specs=out_specs,
        out_shape=out_shape,
        compiler_params=_params(("parallel", "arbitrary" if n_gains else "parallel"), est),
        name=name,
    )(*args)
    return list(outs)


FFN_ROW_BLOCK = 256
FFN_ROW_TILE = 2048


def _shift_rows(u, d, prev):
    halo = CONV_W - 1
    rolled = pltpu.roll(u, d, 0)
    head = rolled[0:SUBLANES]
    row = lax.broadcasted_iota(jnp.int32, head.shape, 0)
    for r in range(d):
        head = jnp.where(row == r, prev[halo - d + r:halo - d + r + 1, :], head)
    return jnp.concatenate([head, rolled[SUBLANES:]], axis=0)


def _ffn_up_kernel(*refs, rows_per_seq, tiles_per_seq, has_prev, rb, norm_dim):
    if has_prev:
        (xg_ref, ssq_ref, wa_ref, wg_ref, ca_ref, cg_ref, ba_ref, bg_ref, pa_ref, pg_ref,
         h_ref, ta_ref, tg_ref, rs_ref) = refs
        carry_ref = None
        prev_refs = (pa_ref, pg_ref)
    else:
        (xg_ref, ssq_ref, wa_ref, wg_ref, ca_ref, cg_ref, ba_ref, bg_ref,
         h_ref, ta_ref, tg_ref, rs_ref, carry_ref) = refs
        prev_refs = (None, None)
    i = pl.program_id(0)
    f = pl.program_id(1)
    tm = xg_ref.shape[0]
    tf = wa_ref.shape[1]
    halo = CONV_W - 1
    seg = min(rb, rows_per_seq)
    conv_refs = ((ca_ref, ba_ref, ta_ref), (cg_ref, bg_ref, tg_ref))

    @pl.when(f == 0)
    def _():
        rs_ref[...] = _row_rsqrt(ssq_ref[...], norm_dim)

    tails = [None, None]
    if not has_prev:
        @pl.when((i % tiles_per_seq) == 0)
        def _():
            carry_ref[f] = jnp.zeros(carry_ref.shape[1:], F32)

        tails = [carry_ref[f, 0, 0:halo, :], carry_ref[f, 1, 0:halo, :]]

    wa, wg = wa_ref[...], wg_ref[...]
    for j in range(tm // rb):
        xs = xg_ref[j * rb:(j + 1) * rb, :]
        rs = _tile_lanes(rs_ref[j * rb:(j + 1) * rb, :], tf)
        us = (_dot(xs, wa) * rs, _dot(xs, wg) * rs)
        for s in range(rb // seg):
            r0 = j * rb + s * seg
            zs = []
            for which, (c_ref, b_ref, t_ref) in enumerate(conv_refs):
                u = us[which][s * seg:(s + 1) * seg]
                prev = prev_refs[which][r0 // rows_per_seq] if has_prev else tails[which]
                z = b_ref[...] + c_ref[halo:halo + 1, :] * u
                for d in range(1, CONV_W):
                    z = z + c_ref[halo - d:halo - d + 1, :] * _shift_rows(u, d, prev)
                tail = u[seg - halo:seg]
                if has_prev:
                    t_ref[r0 // rows_per_seq] = tail
                else:
                    tails[which] = tail
                zs.append(z)
            h_ref[r0:r0 + seg, :] = (_silu(zs[1]) * zs[0]).astype(h_ref.dtype)
    if not has_prev:
        for which, (_, _, t_ref) in enumerate(conv_refs):
            t_ref[0] = tails[which]
            carry_ref[f, which, 0:halo, :] = tails[which]


def ffn_up(xg, row_ssq, w_up, layer, w_conv, b_conv, prev, *, nseq, seq_len, tf):
    rows, d = xg.shape
    ff = w_up.shape[2] // 2
    nf = ff // tf
    has_prev = prev is not None
    halo = CONV_W - 1
    if has_prev:
        tm, tiles_per_seq, n_tiles, S = rows, 1, 1, nseq
    else:
        tm = min(FFN_ROW_TILE, seq_len)
        tiles_per_seq = seq_len // tm
        n_tiles, S = nseq * tiles_per_seq, 1
    rb = min(FFN_ROW_BLOCK, tm)
    assert seq_len % min(rb, seq_len) == 0 and rb % min(rb, seq_len) == 0
    in_specs = [
        pl.BlockSpec((tm, d), lambda i, f: (i, 0)),
        pl.BlockSpec((tm, row_ssq.shape[1]), lambda i, f: (i, 0)),
        pl.BlockSpec((None, d, tf), lambda i, f: (layer, 0, f)),
        pl.BlockSpec((None, d, tf), lambda i, f: (layer, 0, nf + f)),
        pl.BlockSpec((CONV_W, tf), lambda i, f: (0, f)),
        pl.BlockSpec((CONV_W, tf), lambda i, f: (0, nf + f)),
        pl.BlockSpec((1, tf), lambda i, f: (0, f)),
        pl.BlockSpec((1, tf), lambda i, f: (0, nf + f)),
    ]
    args = [xg, row_ssq, w_up, w_up, w_conv, w_conv, b_conv, b_conv]
    scratch = [pltpu.VMEM((tm, LANES), F32)]
    if has_prev:
        in_specs += [pl.BlockSpec((S, halo, tf), lambda i, f: (0, 0, f)),
                     pl.BlockSpec((S, halo, tf), lambda i, f: (0, 0, nf + f))]
        args += [prev, prev]
    else:
        scratch.append(pltpu.VMEM((nf, 2, SUBLANES, tf), F32))
    est = (2 * tm * d * 2 + 2 * 2 * d * tf * 2 + 2 * tm * tf * 2 + 12 * rb * tf * 4
           + 2 * tm * row_ssq.shape[1] * 4 + tm * LANES * 4)
    kern = functools.partial(_ffn_up_kernel, rows_per_seq=seq_len, tiles_per_seq=tiles_per_seq,
                             has_prev=has_prev, rb=rb, norm_dim=d)
    h, tail_a, tail_g = pl.pallas_call(
        kern,
        grid=(n_tiles, nf),
        in_specs=in_specs,
        out_specs=[pl.BlockSpec((tm, tf), lambda i, f: (i, f)),
                   pl.BlockSpec((S, halo, tf), lambda i, f: (i, 0, f)),
                   pl.BlockSpec((S, halo, tf), lambda i, f: (i, 0, f))],
        out_shape=[jax.ShapeDtypeStruct((rows, ff), BF16),
                   jax.ShapeDtypeStruct((n_tiles * S, halo, ff), F32),
                   jax.ShapeDtypeStruct((n_tiles * S, halo, ff), F32)],
        scratch_shapes=scratch,
        compiler_params=_params(("arbitrary", "arbitrary"), est),
        name="ffn_up",
    )(*args)
    tails = jnp.concatenate([tail_a, tail_g], axis=-1)
    if not has_prev:
        tails = tails[tiles_per_seq - 1::tiles_per_seq]
    return h, tails


def _gla_tables():
    C = CHUNK
    t = np.arange(C)[:, None]
    s = np.arange(C)[None, :]
    mats = [(s <= t)]
    masks = [(s == t)]
    m = C
    while m >= 2:
        mid = (t // m) * m + m // 2
        mats.append(((s > mid) & (s <= t)) | ((s > t) & (s <= mid)))
        masks.append((t // m == s // m) & (t % m >= m // 2) & (s % m < m // 2))
        m //= 2
    mats.append(s > t)
    return (np.concatenate(mats, axis=0).astype(np.float32),
            np.stack(masks, axis=0).astype(np.float32))


def _gla_kernel(*refs, heads, dk, dv, n_chunks, zero_init):
    if zero_init:
        (q_ref, k_ref, v_ref, r_ref, lr_ref, wg_ref, bg_ref, gn_ref, tab_ref, msk_ref,
         o_ref, sout_ref, st_ref, la_ref) = refs
        s0_ref = None
    else:
        (q_ref, k_ref, v_ref, r_ref, lr_ref, wg_ref, bg_ref, gn_ref, tab_ref, msk_ref, s0_ref,
         o_ref, sout_ref, st_ref, la_ref) = refs
    t = pl.program_id(2)
    n_tiles = pl.num_programs(2)
    C = CHUNK
    n_levels = msk_ref.shape[0]

    @pl.when(t == 0)
    def _():
        for h in range(heads):
            if zero_init:
                st_ref[h] = jnp.zeros((dv, dk), F32)
            else:
                st_ref[h] = s0_ref[0, h].T

    z = _dot(lr_ref[...].astype(BF16), wg_ref[...]) + bg_ref[...]
    la_ref[...] = _log_sigmoid(z) * (1.0 / GLA_TAU)
    tab = tab_ref[...]

    def chunk(c, carry):
        r0 = pl.multiple_of(c * C, C)
        la = la_ref[pl.ds(r0, C), :]
        sums = None
        for part in _split_bf16(la, 2):
            p = _dot(tab, part)
            sums = p if sums is None else sums + p
        decay = jnp.exp(sums)
        for h in range(heads):
            ks = slice(h * dk, (h + 1) * dk)
            vs = slice(h * dv, (h + 1) * dv)
            q = q_ref[pl.ds(r0, C), ks].astype(F32)
            k = k_ref[pl.ds(r0, C), ks].astype(F32)
            v = v_ref[pl.ds(r0, C), vs]
            att = msk_ref[0] * _nt(q.astype(BF16), k.astype(BF16))
            for lvl in range(1, n_levels):
                e = decay[lvl * C:(lvl + 1) * C, ks]
                att = att + msk_ref[lvl] * _nt((q * e).astype(BF16), (k * e).astype(BF16))
            e_in = decay[0:C, ks]
            st = st_ref[h]
            o = _dot(att.astype(BF16), v) + _nt((q * e_in).astype(BF16), st.astype(BF16))
            k_end = (k * decay[n_levels * C:(n_levels + 1) * C, ks]).astype(BF16)
            st_ref[h] = st * e_in[C - 1:C, :] + _tn(v, k_end)
            on = o * lax.rsqrt(jnp.mean(o * o, axis=-1, keepdims=True) + EPS) * gn_ref[...]
            gate = _silu(r_ref[pl.ds(r0, C), vs].astype(F32))
            o_ref[pl.ds(r0, C), vs] = (on * gate).astype(o_ref.dtype)
        return carry

    lax.fori_loop(0, n_chunks, chunk, 0, unroll=4 if n_chunks % 4 == 0 else 1)

    @pl.when(t == n_tiles - 1)
    def _():
        for h in range(heads):
            sout_ref[0, h] = st_ref[h].T


def gla(proj, lr, wg, bg, gn, s0, *, nseq, seq_len, n_heads, dk, dv):
    rows = proj.shape[0]
    hg = min(4, n_heads)
    n_groups = n_heads // hg
    tt = min(512, seq_len)
    n_tiles = seq_len // tt
    tab, msk = _gla_tables()
    tab = jnp.asarray(tab, BF16)
    msk = jnp.asarray(msk, F32)
    qw, vw = hg * dk, hg * dv
    k_off = (n_heads * dk) // qw
    v_off = (2 * n_heads * dk) // vw
    r_off = (2 * n_heads * dk + n_heads * dv) // vw
    row = lambda b, g, t: b * n_tiles + t
    in_specs = [
        pl.BlockSpec((tt, qw), lambda b, g, t: (row(b, g, t), g)),
        pl.BlockSpec((tt, qw), lambda b, g, t: (row(b, g, t), k_off + g)),
        pl.BlockSpec((tt, vw), lambda b, g, t: (row(b, g, t), v_off + g)),
        pl.BlockSpec((tt, vw), lambda b, g, t: (row(b, g, t), r_off + g)),
        pl.BlockSpec((tt, LANES), lambda b, g, t: (row(b, g, t), 0)),
        pl.BlockSpec((LANES, qw), lambda b, g, t: (0, g)),
        pl.BlockSpec((1, qw), lambda b, g, t: (0, g)),
        pl.BlockSpec((1, dv), lambda b, g, t: (0, 0)),
        pl.BlockSpec(tab.shape, lambda b, g, t: (0, 0)),
        pl.BlockSpec(msk.shape, lambda b, g, t: (0, 0, 0)),
    ]
    args = [proj, proj, proj, proj, lr, wg, bg, gn, tab, msk]
    zero_init = s0 is None
    if not zero_init:
        in_specs.append(pl.BlockSpec((1, hg, dk, dv), lambda b, g, t: (b, g, 0, 0)))
        args.append(s0)
    est = (2 * tt * (2 * qw + 2 * vw) * 2 + 2 * tt * LANES * 4 + 2 * tt * vw * 2
           + 4 * hg * dk * dv * 4 + hg * dk * dv * 4 + tt * qw * 4 + 4 * tab.shape[0] * qw * 4)
    kern = functools.partial(_gla_kernel, heads=hg, dk=dk, dv=dv, n_chunks=tt // CHUNK, zero_init=zero_init)
    o, s_out = pl.pallas_call(
        kern,
        grid=(nseq, n_groups, n_tiles),
        in_specs=in_specs,
        out_specs=[pl.BlockSpec((tt, vw), lambda b, g, t: (row(b, g, t), g)),
                   pl.BlockSpec((1, hg, dk, dv), lambda b, g, t: (b, g, 0, 0))],
        out_shape=[jax.ShapeDtypeStruct((rows, n_heads * dv), BF16),
                   jax.ShapeDtypeStruct((nseq, n_heads, dk, dv), F32)],
        scratch_shapes=[pltpu.VMEM((hg, dv, dk), F32), pltpu.VMEM((tt, qw), F32)],
        compiler_params=_params(("parallel", "parallel", "arbitrary"), est),
        name="gla",
    )(*args)
    return o, s_out


def _mem_attn_kernel(q_ref, k_ref, v_ref, o_ref, *, n_heads):
    for h in range(n_heads):
        cs = slice(h * MEM_DH, (h + 1) * MEM_DH)
        s = _nt(q_ref[:, cs], k_ref[0, :, cs].astype(BF16))
        p = jnp.exp(s - jnp.max(s, axis=-1, keepdims=True))
        l = jnp.sum(p, axis=-1, keepdims=True)
        o = _dot(p.astype(BF16), v_ref[0, :, cs].astype(BF16))
        o_ref[:, cs] = (o / l).astype(o_ref.dtype)


def mem_attn(qsrc, q_col_block, mk, mv, *, seq_len):
    rows = qsrc.shape[0]
    nseq, n_mem, mq = mk.shape
    tm = min(1024, seq_len)
    tiles_per_seq = seq_len // tm
    est = 2 * tm * mq * 2 * 2 + 2 * 2 * n_mem * mq * 4 + 6 * tm * n_mem * 4
    return pl.pallas_call(
        functools.partial(_mem_attn_kernel, n_heads=mq // MEM_DH),
        grid=(rows // tm,),
        in_specs=[pl.BlockSpec((tm, mq), lambda i: (i, q_col_block)),
                  pl.BlockSpec((1, n_mem, mq), lambda i: (i // tiles_per_seq, 0, 0)),
                  pl.BlockSpec((1, n_mem, mq), lambda i: (i // tiles_per_seq, 0, 0))],
        out_specs=pl.BlockSpec((tm, mq), lambda i: (i, 0)),
        out_shape=jax.ShapeDtypeStruct((rows, mq), BF16),
        compiler_params=_params(("parallel",), est),
        name="mem_attn",
    )(qsrc, mk, mv)


def _cumsum_kernel(lf_ref, c_ref, ct_ref, crow_ref, ccol_ref):
    j = pl.program_id(1)
    tb = lf_ref.shape[1]

    @pl.when(j == 0)
    def _():
        crow_ref[...] = jnp.zeros_like(crow_ref)
        ccol_ref[...] = jnp.zeros_like(ccol_ref)

    r = lax.broadcasted_iota(jnp.int32, (tb, tb), 0)
    s = lax.broadcasted_iota(jnp.int32, (tb, tb), 1)
    lower = jnp.where(s <= r, 1.0, 0.0).astype(BF16)
    upper = jnp.where(r <= s, 1.0, 0.0).astype(BF16)
    c = crow_ref[0:1, :]
    ct = ccol_ref[:, 0:1]
    for part in _split_bf16(lf_ref[0], 3):
        c = c + _dot(lower, part)
        ct = ct + _tn(part, upper)
    c_ref[0] = c
    ct_ref[0] = ct
    crow_ref[...] = jnp.broadcast_to(c[tb - 1:tb, :], crow_ref.shape)
    ccol_ref[...] = jnp.broadcast_to(ct[:, tb - 1:tb], ccol_ref.shape)


def cumsum_time(lf):
    nseq, L, w = lf.shape
    tb = L if L <= 2048 else 512
    est = 4 * tb * w * 4 + 4 * tb * tb * 4 + 8 * tb * w * 4
    return pl.pallas_call(
        _cumsum_kernel,
        grid=(nseq, L // tb),
        in_specs=[pl.BlockSpec((1, tb, w), lambda b, j: (b, j, 0))],
        out_specs=[pl.BlockSpec((1, tb, w), lambda b, j: (b, j, 0)),
                   pl.BlockSpec((1, w, tb), lambda b, j: (b, 0, j))],
        out_shape=[jax.ShapeDtypeStruct((nseq, L, w), F32), jax.ShapeDtypeStruct((nseq, w, L), F32)],
        scratch_shapes=[pltpu.VMEM((SUBLANES, w), F32), pltpu.VMEM((w, LANES), F32)],
        compiler_params=_params(("parallel", "arbitrary"), est),
        name="cumsum_time",
    )(lf)


def _pick_lane(x, lane):
    ids = lax.broadcasted_iota(jnp.int32, x.shape, 1)
    return jnp.sum(jnp.where(ids == lane, x, 0.0), axis=-1, keepdims=True)


LOG2E = 1.4426950408889634
FOX_Q_BLOCK = 2048
FOX_ROW_CHAIN = 512
FOX_K_BLOCK = 1024
FOX_AUG = 3


def _fox_kernel(q_ref, k_ref, v_ref, c_ref, o_ref, augq_ref, augk_ref, vaug_ref, *chain_refs, tk, tkf):
    h = pl.program_id(1)
    qi = pl.program_id(2)
    seq_len, dh = k_ref.shape
    tq = q_ref.shape[0]
    n_sub = tq // tk
    chains = [chain_refs[3 * i:3 * i + 3] for i in range(n_sub)]

    @pl.when(qi == 0)
    def _():
        def build(r, carry):
            r0 = pl.multiple_of(r * tk, tk)
            cc = _pick_lane(c_ref[0, pl.ds(r0, tk), :], h) * LOG2E
            lane = lax.broadcasted_iota(jnp.int32, (tk, dh), 1)
            aq = jnp.where(lane < 2 * FOX_AUG, 1.0, 0.0)
            ak = aq
            for n, part in enumerate(_split_bf16(cc, FOX_AUG)):
                aq = jnp.where(lane == n, part.astype(F32), aq)
                ak = jnp.where(lane == FOX_AUG + n, -part.astype(F32), ak)
            augq_ref[pl.ds(r0, tk), :] = aq.astype(BF16)
            augk_ref[pl.ds(r0, tk), :] = ak.astype(BF16)
            vaug_ref[pl.ds(r0, tk), 0:dh] = v_ref[pl.ds(r0, tk), :]
            vaug_ref[pl.ds(r0, tk), dh:2 * dh] = jnp.ones((tk, dh), BF16)
            return carry

        lax.fori_loop(0, seq_len // tk, build, 0)

    q0 = pl.multiple_of(qi * tq, tq)
    for sub, (qa_ref, m_ref, acc_ref) in enumerate(chains):
        qa_ref[:, 0:dh] = q_ref[sub * tk:(sub + 1) * tk, :]
        qa_ref[:, dh:2 * dh] = augq_ref[pl.ds(pl.multiple_of(q0 + sub * tk, tk), tk), :]
        m_ref[...] = jnp.full_like(m_ref, -jnp.inf)
        acc_ref[...] = jnp.zeros_like(acc_ref)

    def scores(sub, k0, width):
        ka = jnp.concatenate([k_ref[pl.ds(k0, width), :], augk_ref[pl.ds(k0, width), :]], axis=1)
        return _nt(chains[sub][0][...], ka)

    def softmax_update(sub, s, masked):
        m_ref = chains[sub][1]
        if masked:
            row = lax.broadcasted_iota(jnp.int32, s.shape, 0)
            col = lax.broadcasted_iota(jnp.int32, s.shape, 1)
            s = jnp.where(col <= row, s, -jnp.inf)
        blocks = [s[:, j * LANES:(j + 1) * LANES] for j in range(s.shape[1] // LANES)]
        mx = blocks[0]
        for blk in blocks[1:]:
            mx = jnp.maximum(mx, blk)
        m_old = m_ref[...]
        m_new = jnp.maximum(m_old, jnp.max(mx, axis=-1, keepdims=True))
        m_ref[...] = m_new
        alpha = jnp.exp2(m_old - m_new)
        pr = jnp.concatenate([jnp.exp2(blk - m_new) for blk in blocks], axis=1)
        return alpha, pr.astype(BF16)

    def accumulate(sub, k0, width, alpha, pr):
        acc_ref = chains[sub][2]
        acc_ref[...] = (jnp.concatenate([alpha] * (2 * dh // LANES), axis=1) * acc_ref[...]
                        + _dot(pr, vaug_ref[pl.ds(k0, width), :]))

    def step(jobs, width):
        ss = [scores(sub, k0, width) for sub, k0, _ in jobs]
        ps = [softmax_update(sub, s, masked) for (sub, _, masked), s in zip(jobs, ss)]
        for (sub, k0, _), (alpha, pr) in zip(jobs, ps):
            accumulate(sub, k0, width, alpha, pr)

    def full_chunk(j, carry):
        k0 = pl.multiple_of(j * tkf, tkf)
        step([(sub, k0, False) for sub in range(n_sub)], tkf)
        return carry

    lax.fori_loop(0, qi * (tq // tkf), full_chunk, 0)
    step([(sub, pl.multiple_of(q0 + d * tk, tk), sub == d) for d in range(n_sub) for sub in range(d, n_sub)], tk)
    for sub, (_, _, acc_ref) in enumerate(chains):
        o_ref[sub * tk:(sub + 1) * tk, :] = (acc_ref[:, 0:dh] / acc_ref[:, dh:2 * dh]).astype(o_ref.dtype)


def fox_prompt(qsrc, kbf, vbf, c, *, nseq, seq_len, n_heads, dh):
    rows = qsrc.shape[0]
    tq = min(FOX_Q_BLOCK, seq_len)
    tk = min(FOX_ROW_CHAIN, tq)
    tkf = min(FOX_K_BLOCK, tq)
    nq = seq_len // tq
    est = (2 * tq * dh * 2 * 2 + 2 * 2 * seq_len * dh * 2 + 2 * seq_len * LANES * 4 + 4 * seq_len * dh * 2
           + tq * 2 * dh * 2 + 3 * tq * LANES * 4 + (tq // tk) * 4 * tk * tkf * 4)
    return pl.pallas_call(
        functools.partial(_fox_kernel, tk=tk, tkf=tkf),
        grid=(nseq, n_heads, nq),
        in_specs=[
            pl.BlockSpec((tq, dh), lambda b, h, qi: (b * nq + qi, h)),
            pl.BlockSpec((seq_len, dh), lambda b, h, qi: (b, h)),
            pl.BlockSpec((seq_len, dh), lambda b, h, qi: (b, h)),
            pl.BlockSpec((1, seq_len, LANES), lambda b, h, qi: (b, 0, 0)),
        ],
        out_specs=pl.BlockSpec((tq, dh), lambda b, h, qi: (b * nq + qi, h)),
        out_shape=jax.ShapeDtypeStruct((rows, n_heads * dh), BF16),
        scratch_shapes=[pltpu.VMEM((seq_len, dh), BF16), pltpu.VMEM((seq_len, dh), BF16),
                        pltpu.VMEM((seq_len, 2 * dh), BF16)]
        + [pltpu.VMEM((tk, 2 * dh), BF16), pltpu.VMEM((tk, LANES), F32),
           pltpu.VMEM((tk, 2 * dh), F32)] * (tq // tk),
        compiler_params=_params(("parallel", "parallel", "arbitrary"), est),
        name="fox_prompt",
    )(qsrc, kbf, vbf, c)


FOX_DECODE_KEYS = 512


def _fox_decode_kernel(q_ref, kc_ref, vc_ref, kn_ref, vn_ref, c_ref, ctc_ref, ctn_ref, o_ref,
                       m_ref, l_ref, acc_ref, *, n_heads, dh):
    j = pl.program_id(1)
    ts = q_ref.shape[0]
    pc = kc_ref.shape[1] // n_heads

    def cached(h):
        cs = slice(h * dh, (h + 1) * dh)
        cq = _pick_lane(c_ref[0], h) * LOG2E
        kc = kc_ref[0, pl.ds(h, pc, stride=n_heads), :].astype(BF16)
        vc = vc_ref[0, pl.ds(h, pc, stride=n_heads), :].astype(BF16)
        return cs, cq, _nt(q_ref[:, cs], kc) + cq - ctc_ref[0, h] * LOG2E, vc

    @pl.when(j == 0)
    def _():
        row = lax.broadcasted_iota(jnp.int32, (ts, ts), 0)
        col = lax.broadcasted_iota(jnp.int32, (ts, ts), 1)
        for h in range(n_heads):
            cs, cq, s, vc = cached(h)
            sn = _nt(q_ref[:, cs], kn_ref[:, cs]) + cq - ctn_ref[0, h] * LOG2E
            s = jnp.concatenate([s, jnp.where(col <= row, sn, -jnp.inf)], axis=1)
            m = jnp.max(s, axis=-1, keepdims=True)
            p = jnp.exp2(s - m)
            m_ref[h] = m
            l_ref[h] = jnp.sum(p, axis=-1, keepdims=True)
            acc_ref[h] = _dot(p.astype(BF16), jnp.concatenate([vc, vn_ref[:, cs]], axis=0))

    @pl.when(j > 0)
    def _():
        for h in range(n_heads):
            _, _, s, vc = cached(h)
            m_old = m_ref[h]
            m_new = jnp.maximum(m_old, jnp.max(s, axis=-1, keepdims=True))
            alpha = jnp.exp2(m_old - m_new)
            p = jnp.exp2(s - m_new)
            l_ref[h] = alpha * l_ref[h] + jnp.sum(p, axis=-1, keepdims=True)
            acc_ref[h] = alpha * acc_ref[h] + _dot(p.astype(BF16), vc)
            m_ref[h] = m_new

    @pl.when(j == pl.num_programs(1) - 1)
    def _():
        for h in range(n_heads):
            cs = slice(h * dh, (h + 1) * dh)
            o_ref[:, cs] = (acc_ref[h] / l_ref[h]).astype(o_ref.dtype)


def fox_decode(qsrc, kcache, vcache, kbf, vbf, c, ct4, *, n_heads, dh):
    nseq = kcache.shape[0]
    past = kcache.shape[1] // n_heads
    rows = qsrc.shape[0]
    ts = rows // nseq
    w = n_heads * dh
    pc = _pick_tile(past, FOX_DECODE_KEYS, LANES)
    ct_new = ct4[..., past:]
    est = (2 * 2 * pc * w * 4 + 8 * ts * w * 2 + 2 * ts * LANES * 4 + n_heads * ts * (dh + 2 * LANES) * 4
           + 8 * ts * pc * 4 + 4 * pc * dh * 4)
    return pl.pallas_call(
        functools.partial(_fox_decode_kernel, n_heads=n_heads, dh=dh),
        grid=(nseq, past // pc),
        in_specs=[
            pl.BlockSpec((ts, w), lambda b, j: (b, 0)),
            pl.BlockSpec((1, pc * n_heads, dh), lambda b, j: (b, j, 0)),
            pl.BlockSpec((1, pc * n_heads, dh), lambda b, j: (b, j, 0)),
            pl.BlockSpec((ts, w), lambda b, j: (b, 0)),
            pl.BlockSpec((ts, w), lambda b, j: (b, 0)),
            pl.BlockSpec((1, ts, LANES), lambda b, j: (b, past // ts, 0)),
            pl.BlockSpec((1, n_heads, 1, pc), lambda b, j: (b, 0, 0, j)),
            pl.BlockSpec((1, n_heads, 1, ts), lambda b, j: (b, 0, 0, 0)),
        ],
        out_specs=pl.BlockSpec((ts, w), lambda b, j: (b, 0)),
        out_shape=jax.ShapeDtypeStruct((rows, w), BF16),
        scratch_shapes=[pltpu.VMEM((n_heads, ts, 1), F32), pltpu.VMEM((n_heads, ts, 1), F32),
                        pltpu.VMEM((n_heads, ts, dh), F32)],
        compiler_params=_params(("parallel", "arbitrary"), est),
        name="fox_decode",
    )(qsrc, kcache, vcache, kbf, vbf, c, ct4, ct_new)


def _pad_cols(x, width):
    return jnp.pad(x, ((0, 0), (0, width - x.shape[1])))


def _ffn_block(x, xg, ssq, w_up, w_down, layer, w_conv, b_conv, prev, next_gains, *, nseq, seq_len):
    tf = _pick_tile(w_down.shape[1], 256, LANES)
    h, tails = ffn_up(xg, ssq, w_up, layer, w_conv, b_conv, prev, nseq=nseq, seq_len=seq_len, tf=tf)
    outs = matmul([h], w_down, [F32], tm=512, tn=512, w_layer=layer, res=x, next_gains=next_gains, name="ffn_down")
    return outs, tails


def kernel(x_prompt, x_sample, state_gla, cache_fox_k, cache_fox_v, cache_fox_logf, cache_mem_k, cache_mem_v,
           state_ffn_conv, mem_prompt, g_attn, g_ffn, g_mem, w_mem_kv, w_in_a, w_gate_a, b_gate_a, g_gla,
           w_in_b, g_kv, w_kv, b_forget, w_out, w_up, w_conv, b_conv, w_down, g_final):
    n_p, t_p, d = x_prompt.shape
    n_s, t_s, _ = x_sample.shape
    depth = g_attn.shape[0]
    n_a = state_gla.shape[0]
    _, _, gla_heads, dk, dv = state_gla.shape
    _, past, fox_heads, dh = cache_fox_k.shape
    n_mem, mq = cache_mem_k.shape[2:]
    a_q, a_v, b_q = gla_heads * dk, gla_heads * dv, fox_heads * dh

    def mixer_gains(l):
        return jnp.stack([g_attn[l]] + ([g_kv] if l == n_a else []))

    streams = []
    for x3, nseq, seq_len in ((x_prompt, n_p, t_p), (x_sample, n_s, t_s)):
        x = x3.reshape(nseq * seq_len, d)
        streams.append(dict(x=x, nseq=nseq, seq_len=seq_len, normed=rmsnorm(x, mixer_gains(0), BF16), ssq=None))
    mem_rows = mem_prompt.reshape(n_p * n_mem, d)
    mk_out, mv_out, conv_out = [], [], [[], []]
    gla_out = [[], []]
    fox = [None, None]

    w_mem_bf, w_in_a_bf, w_in_b_bf, w_kv_bf = (w.astype(BF16) for w in (w_mem_kv, w_in_a, w_in_b, w_kv))
    w_out_bf, w_up_bf, w_down_bf = (w.astype(BF16) for w in (w_out, w_up, w_down))
    mem_scale = jnp.full((1, mq), MEM_DH ** -0.5, F32)
    main = 2 * a_q + 2 * a_v

    for l in range(depth):
        (mem_n,) = rmsnorm(mem_rows, g_mem[l][None], BF16)
        (mkv,) = matmul([mem_n], w_mem_bf, [F32], tm=512, tn=1024, w_layer=l, name="mem_kv")
        mk_p = mkv[:, :mq].reshape(n_p, n_mem, mq)
        mv_p = mkv[:, mq:].reshape(n_p, n_mem, mq)
        mk_out.append(mk_p)
        mv_out.append(mv_p)
        mem_kv = [(mk_p, mv_p), (cache_mem_k[l], cache_mem_v[l])]

        if l < n_a:
            w_lr = _pad_cols(w_in_a[l][:, main:main + GLA_RANK], LANES).astype(BF16)
            w_qm = w_in_a[l][:, main + GLA_RANK:].astype(BF16)
            col_scale = jnp.concatenate([jnp.full((a_q,), dk ** -0.5, F32), jnp.ones((a_q + 2 * a_v,), F32)])[None]
            wg = jnp.pad(w_gate_a[l], ((0, LANES - GLA_RANK), (0, 0))).astype(BF16)
            bg = b_gate_a[l][None]
            gn = g_gla[l][None]
        else:
            j = l - n_a
            col_scale = jnp.concatenate([jnp.full((b_q,), LOG2E * dh ** -0.5, F32), mem_scale[0]])[None]
            if l == n_a:
                w_kv_f = _pad_cols(w_kv[:, 2 * b_q:], LANES).astype(BF16)
                b_f = _pad_cols(b_forget[None], LANES)
        ffn_next_gains = mixer_gains(l + 1) if l + 1 < depth else None

        for si, st in enumerate(streams):
            x, nseq, seq_len = st["x"], st["nseq"], st["seq_len"]
            mk, mv = mem_kv[si]
            xn = st["normed"][0]
            norm = dict(row_ssq=st["ssq"], norm_dim=d)
            if l < n_a:
                (proj,) = matmul([xn], w_in_a_bf, [BF16], tm=1024, tn=1024, w_layer=l, n=main, scale=col_scale,
                                 name="in_proj_a", **norm)
                (qm,) = matmul([xn], w_qm, [BF16], tm=1024, tn=1024, scale=mem_scale, name="in_proj_a_mem", **norm)
                (lr,) = matmul([xn], w_lr, [F32], tm=1024, tn=LANES, name="in_proj_a_gate", **norm)
                s0 = None if si == 0 else state_gla[l]
                o, s_new = gla(proj, lr, wg, bg, gn, s0, nseq=nseq, seq_len=seq_len,
                               n_heads=gla_heads, dk=dk, dv=dv)
                gla_out[si].append(s_new)
                mo = mem_attn(qm, 0, mk, mv, seq_len=seq_len)
            else:
                if l == n_a:
                    xkv = st["normed"][1]
                    k32, kbf = matmul([xkv], w_kv_bf, [F32, BF16], tm=1024, tn=1024, n=b_q, head_dims=[dh, None],
                                      name="k_proj", **norm)
                    v32, vbf = matmul([xkv], w_kv_bf, [F32, BF16], tm=1024, tn=1024, w_col0=b_q, n=b_q,
                                      head_dims=[dh, None], name="v_proj", **norm)
                    (lf,) = matmul([xkv], w_kv_f, [F32], tm=1024, tn=LANES, bias=b_f, act="log_sigmoid",
                                   name="kv_proj_forget", **norm)
                    lf3 = lf.reshape(nseq, seq_len, LANES)
                    if si == 0:
                        lf_all = lf3
                    else:
                        lf_all = jnp.concatenate(
                            [jnp.pad(cache_fox_logf, ((0, 0), (0, 0), (0, LANES - fox_heads))), lf3], axis=1)
                    c, ct = cumsum_time(lf_all)
                    ct4 = ct[:, :fox_heads].reshape(nseq, fox_heads, 1, ct.shape[-1])
                    fox[si] = dict(k32=k32, v32=v32, kbf=kbf, vbf=vbf, lf=lf3[:, :, :fox_heads], c=c, ct4=ct4)
                fx = fox[si]
                (proj,) = matmul([xn], w_in_b_bf, [BF16], tm=1024, tn=1024, w_layer=j, scale=col_scale,
                                 name="in_proj_b", **norm)
                kb, vb = fx["kbf"], fx["vbf"]
                if si == 0:
                    o = fox_prompt(proj, kb, vb, fx["c"], nseq=nseq, seq_len=seq_len,
                                   n_heads=fox_heads, dh=dh)
                else:
                    o = fox_decode(proj, cache_fox_k.reshape(nseq, past * fox_heads, dh),
                                   cache_fox_v.reshape(nseq, past * fox_heads, dh),
                                   kb, vb, fx["c"], fx["ct4"], n_heads=fox_heads, dh=dh)
                mo = mem_attn(proj, b_q // mq, mk, mv, seq_len=seq_len)
            x, xg, ssq = matmul([o, mo], w_out_bf, [F32], tm=1024, tn=512, w_layer=l, res=x,
                                next_gains=g_ffn[l][None], name="out_proj")
            prev = None if si == 0 else state_ffn_conv[l]
            outs, tails = _ffn_block(x, xg, ssq, w_up_bf, w_down_bf, l, w_conv[l], b_conv[l][None], prev,
                                     ffn_next_gains, nseq=nseq, seq_len=seq_len)
            conv_out[si].append(tails)
            st["x"] = outs[0]
            if ffn_next_gains is not None:
                st["normed"], st["ssq"] = outs[1:-1], outs[-1]

    ys = []
    for st in streams:
        (y,) = rmsnorm(st["x"], g_final[None], F32)
        ys.append(y)
    fp, fs = fox
    return (ys[0].reshape(n_p, t_p, d), ys[1].reshape(n_s, t_s, d),
            jnp.stack(gla_out[0]), jnp.stack(gla_out[1]),
            fp["k32"].reshape(n_p, t_p, fox_heads, dh), fp["v32"].reshape(n_p, t_p, fox_heads, dh), fp["lf"],
            fs["k32"].reshape(n_s, t_s, fox_heads, dh), fs["v32"].reshape(n_s, t_s, fox_heads, dh), fs["lf"],
            jnp.stack(mk_out), jnp.stack(mv_out),
            jnp.stack(conv_out[0]), jnp.stack(conv_out[1]))
```

```python
import functools

import numpy as np
import jax
import jax.numpy as jnp
from jax import lax
from jax.experimental import pallas as pl
from jax.experimental.pallas import tpu as pltpu

F32 = jnp.float32
BF16 = jnp.bfloat16

CHUNK = 64
GLA_RANK = 16
GLA_TAU = 16.0
MEM_DH = 256
CONV_W = 3
EPS = 1e-6

LANES = 128
SUBLANES = 8
VMEM_PHYSICAL_BYTES = 64 * 1024 * 1024
VMEM_HEADROOM_BYTES = 6 * 1024 * 1024


def _vmem_limit(estimate_bytes):
    want = int(estimate_bytes * 1.25) + 4 * 1024 * 1024
    return int(min(max(want, 16 * 1024 * 1024), VMEM_PHYSICAL_BYTES - VMEM_HEADROOM_BYTES))


def _params(semantics, estimate_bytes):
    return pltpu.CompilerParams(dimension_semantics=semantics, vmem_limit_bytes=_vmem_limit(estimate_bytes))


def _pick_tile(dim, preferred, unit):
    best = None
    for t in range(unit, min(dim, preferred) + 1, unit):
        if dim % t == 0:
            best = t
    assert best is not None, (dim, preferred, unit)
    return best


def _nt(a, b):
    return lax.dot_general(a, b, (((1,), (1,)), ((), ())), preferred_element_type=F32)


def _tn(a, b):
    return lax.dot_general(a, b, (((0,), (0,)), ((), ())), preferred_element_type=F32)


def _dot(a, b):
    return jnp.dot(a, b, preferred_element_type=F32)


def _split_bf16(x, parts):
    out = []
    rem = x
    for _ in range(parts):
        hi = rem.astype(BF16)
        out.append(hi)
        rem = rem - hi.astype(F32)
    return out


def _log_sigmoid(z):
    return jnp.minimum(z, 0.0) - jnp.log(1.0 + jnp.exp(-jnp.abs(z)))


def _silu(z):
    half = 0.5 * z
    return half + half * jnp.tanh(half)


def _rmsnorm_kernel(x_ref, g_ref, *o_refs):
    x = x_ref[...]
    y = x * lax.rsqrt(jnp.mean(x * x, axis=-1, keepdims=True) + EPS)
    for i, o_ref in enumerate(o_refs):
        o_ref[...] = (y * g_ref[i:i + 1, :]).astype(o_ref.dtype)


def rmsnorm(x, gains, out_dtype):
    rows, d = x.shape
    n = gains.shape[0]
    tm = min(256, rows)
    out_bytes = jnp.dtype(out_dtype).itemsize
    est = 2 * tm * d * 4 + n * 2 * tm * d * out_bytes + 2 * tm * d * 4
    outs = pl.pallas_call(
        _rmsnorm_kernel,
        grid=(rows // tm,),
        in_specs=[pl.BlockSpec((tm, d), lambda i: (i, 0)),
                  pl.BlockSpec((n, d), lambda i: (0, 0))],
        out_specs=[pl.BlockSpec((tm, d), lambda i: (i, 0)) for _ in range(n)],
        out_shape=[jax.ShapeDtypeStruct((rows, d), out_dtype) for _ in range(n)],
        compiler_params=_params(("parallel",), est),
        name="rmsnorm",
    )(x, gains)
    return list(outs)


def _row_rsqrt(ssq, norm_dim):
    total = ssq[:, 0:LANES]
    for j in range(1, ssq.shape[1] // LANES):
        total = total + ssq[:, j * LANES:(j + 1) * LANES]
    return lax.rsqrt(total * (1.0 / norm_dim) + EPS)


def _tile_lanes(v, width):
    return jnp.concatenate([v] * (width // LANES), axis=1) if width > LANES else v


def _matmul_kernel(*refs, n_parts, norm_dim, has_scale, has_bias, has_res, act, n_out, n_gains):
    a_refs = refs[:n_parts]
    w_refs = refs[n_parts:2 * n_parts]
    pos = 2 * n_parts
    ssq_ref = scale_ref = bias_ref = res_ref = gains_ref = None
    if norm_dim:
        ssq_ref = refs[pos]; pos += 1
    if has_scale:
        scale_ref = refs[pos]; pos += 1
    if has_bias:
        bias_ref = refs[pos]; pos += 1
    if has_res:
        res_ref = refs[pos]; pos += 1
    if n_gains:
        gains_ref = refs[pos]; pos += 1
    o_refs = refs[pos:pos + n_out]
    pos += n_out
    xg_refs = refs[pos:pos + n_gains]
    ssq_out_ref = refs[pos + n_gains] if n_gains else None

    y = _dot(a_refs[0][...], w_refs[0][...])
    for a_ref, w_ref in zip(a_refs[1:], w_refs[1:]):
        y = y + _dot(a_ref[...], w_ref[...])
    if norm_dim:
        y = y * _tile_lanes(_row_rsqrt(ssq_ref[...], norm_dim), y.shape[1])
    if has_scale:
        y = y * scale_ref[...]
    if has_bias:
        y = y + bias_ref[...]
    if act == "log_sigmoid":
        y = _log_sigmoid(y)
    if has_res:
        y = y + res_ref[...]
    for o_ref in o_refs:
        if len(o_ref.shape) == 3:
            dh = o_ref.shape[2]
            for hh in range(o_ref.shape[1]):
                o_ref[:, hh, :] = y[:, hh * dh:(hh + 1) * dh].astype(o_ref.dtype)
        else:
            o_ref[...] = y.astype(o_ref.dtype)
    for gi, xg_ref in enumerate(xg_refs):
        xg_ref[...] = (y * gains_ref[gi:gi + 1, :]).astype(xg_ref.dtype)
    if n_gains:
        part = jnp.broadcast_to(jnp.sum(y * y, axis=-1, keepdims=True), ssq_out_ref.shape)
        j = pl.program_id(1)

        @pl.when(j == 0)
        def _():
            ssq_out_ref[...] = part

        @pl.when(j > 0)
        def _():
            ssq_out_ref[...] += part


def matmul(a_parts, w, out_dtypes, *, tm, tn, w_layer=None, w_col0=0, n=None, row_ssq=None, norm_dim=None,
           scale=None, bias=None, res=None, act=None, head_dims=None, next_gains=None, name="matmul"):
    rows = a_parts[0].shape[0]
    kdim, n_w = w.shape[-2:]
    n = n_w if n is None else n
    tm = _pick_tile(rows, tm, SUBLANES)
    tn = _pick_tile(n, tn, LANES)
    assert sum(a.shape[1] for a in a_parts) == kdim and w_col0 % tn == 0 and w_col0 + n <= n_w
    col0 = w_col0 // tn
    a_specs, w_specs = [], []
    start = 0
    for a in a_parts:
        kp = a.shape[1]
        assert start % kp == 0, "a part must start at a multiple of its own width in W"
        a_specs.append(pl.BlockSpec((tm, kp), lambda i, j: (i, 0)))
        if w.ndim == 3:
            w_specs.append(pl.BlockSpec((None, kp, tn), lambda i, j, b=start // kp: (w_layer, b, col0 + j)))
        else:
            w_specs.append(pl.BlockSpec((kp, tn), lambda i, j, b=start // kp: (b, col0 + j)))
        start += kp
    in_specs = a_specs + w_specs
    args = list(a_parts) + [w] * len(a_parts)
    if row_ssq is not None:
        in_specs.append(pl.BlockSpec((tm, row_ssq.shape[1]), lambda i, j: (i, 0)))
        args.append(row_ssq)
    for extra in (scale, bias):
        if extra is not None:
            in_specs.append(pl.BlockSpec((1, tn), lambda i, j: (0, j)))
            args.append(extra)
    if res is not None:
        in_specs.append(pl.BlockSpec((tm, tn), lambda i, j: (i, j)))
        args.append(res)
    n_gains = 0 if next_gains is None else next_gains.shape[0]
    if n_gains:
        in_specs.append(pl.BlockSpec((n_gains, tn), lambda i, j: (0, j)))
        args.append(next_gains)
    out_bytes = sum(jnp.dtype(d).itemsize for d in out_dtypes) + 2 * n_gains
    est = (2 * tm * kdim * 2 + 2 * kdim * tn * 2 + 2 * tm * tn * out_bytes
           + (2 * tm * tn * 4 if res is not None else 0) + 2 * tm * tn * 4
           + (2 * tm * row_ssq.shape[1] * 4 if row_ssq is not None else 0))
    kern = functools.partial(
        _matmul_kernel, n_parts=len(a_parts), norm_dim=norm_dim if row_ssq is not None else None,
        has_scale=scale is not None, has_bias=bias is not None, has_res=res is not None, act=act,
        n_out=len(out_dtypes), n_gains=n_gains)
    head_dims = head_dims or [None] * len(out_dtypes)
    out_specs, out_shape = [], []
    for dt, dh in zip(out_dtypes, head_dims):
        if dh is None:
            out_specs.append(pl.BlockSpec((tm, tn), lambda i, j: (i, j)))
            out_shape.append(jax.ShapeDtypeStruct((rows, n), dt))
        else:
            assert tn % (SUBLANES * dh) == 0 or tn == n
            out_specs.append(pl.BlockSpec((tm, tn // dh, dh), lambda i, j: (i, j, 0)))
            out_shape.append(jax.ShapeDtypeStruct((rows, n // dh, dh), dt))
    for _ in range(n_gains):
        out_specs.append(pl.BlockSpec((tm, tn), lambda i, j: (i, j)))
        out_shape.append(jax.ShapeDtypeStruct((rows, n), BF16))
    if n_gains:
        out_specs.append(pl.BlockSpec((tm, LANES), lambda i, j: (i, 0)))
        out_shape.append(jax.ShapeDtypeStruct((rows, LANES), F32))
    outs = pl.pallas_call(
        kern,
        grid=(rows // tm, n // tn),
        in_specs=in_specs,
        out_specs=out_specs,
        out_shape=out_shape,
        compiler_params=_params(("parallel", "arbitrary" if n_gains else "parallel"), est),
        name=name,
    )(*args)
    return list(outs)


FFN_ROW_BLOCK = 256
FFN_ROW_TILE = 2048


def _shift_one_row(u, first_row):
    rolled = pltpu.roll(u, 1, 0)
    head = rolled[0:SUBLANES]
    row = lax.broadcasted_iota(jnp.int32, head.shape, 0)
    head = jnp.where(row == 0, first_row, head)
    return jnp.concatenate([head, rolled[SUBLANES:]], axis=0)


def _ffn_up_kernel(*refs, rows_per_seq, tiles_per_seq, has_prev, rb, norm_dim):
    if has_prev:
        (xg_ref, ssq_ref, wa_ref, wg_ref, ca_ref, cg_ref, ba_ref, bg_ref, pa_ref, pg_ref,
         h_ref, ta_ref, tg_ref, rs_ref) = refs
        carry_ref = None
        prev_refs = (pa_ref, pg_ref)
    else:
        (xg_ref, ssq_ref, wa_ref, wg_ref, ca_ref, cg_ref, ba_ref, bg_ref,
         h_ref, ta_ref, tg_ref, rs_ref, carry_ref) = refs
        prev_refs = (None, None)
    i = pl.program_id(0)
    f = pl.program_id(1)
    tm = xg_ref.shape[0]
    tf = wa_ref.shape[1]
    halo = CONV_W - 1
    seg = min(rb, rows_per_seq)
    conv_refs = ((ca_ref, ba_ref, ta_ref), (cg_ref, bg_ref, tg_ref))

    @pl.when(f == 0)
    def _():
        rs_ref[...] = _row_rsqrt(ssq_ref[...], norm_dim)

    tails = [None, None]
    if not has_prev:
        @pl.when((i % tiles_per_seq) == 0)
        def _():
            carry_ref[f] = jnp.zeros(carry_ref.shape[1:], F32)

        tails = [carry_ref[f, 0, 0:halo, :], carry_ref[f, 1, 0:halo, :]]

    wa, wg = wa_ref[...], wg_ref[...]
    for j in range(tm // rb):
        xs = xg_ref[j * rb:(j + 1) * rb, :]
        rs = _tile_lanes(rs_ref[j * rb:(j + 1) * rb, :], tf)
        us = (_dot(xs, wa) * rs, _dot(xs, wg) * rs)
        for s in range(rb // seg):
            r0 = j * rb + s * seg
            zs = []
            for which, (c_ref, b_ref, t_ref) in enumerate(conv_refs):
                u = us[which][s * seg:(s + 1) * seg]
                prev = prev_refs[which][r0 // rows_per_seq] if has_prev else tails[which]
                z = b_ref[...] + c_ref[halo:halo + 1, :] * u
                shifted = u
                for d in range(1, CONV_W):
                    shifted = _shift_one_row(shifted, prev[halo - d:halo - d + 1, :])
                    z = z + c_ref[halo - d:halo - d + 1, :] * shifted
                tail = u[seg - halo:seg]
                if has_prev:
                    t_ref[r0 // rows_per_seq] = tail
                else:
                    tails[which] = tail
                zs.append(z)
            h_ref[r0:r0 + seg, :] = (_silu(zs[1]) * zs[0]).astype(h_ref.dtype)
    if not has_prev:
        for which, (_, _, t_ref) in enumerate(conv_refs):
            t_ref[0] = tails[which]
            carry_ref[f, which, 0:halo, :] = tails[which]


def ffn_up(xg, row_ssq, w_up, layer, w_conv, b_conv, prev, *, nseq, seq_len, tf):
    rows, d = xg.shape
    ff = w_up.shape[2] // 2
    nf = ff // tf
    has_prev = prev is not None
    halo = CONV_W - 1
    if has_prev:
        tm, tiles_per_seq, n_tiles, S = rows, 1, 1, nseq
    else:
        tm = min(FFN_ROW_TILE, seq_len)
        tiles_per_seq = seq_len // tm
        n_tiles, S = nseq * tiles_per_seq, 1
    rb = min(FFN_ROW_BLOCK, tm)
    assert seq_len % min(rb, seq_len) == 0 and rb % min(rb, seq_len) == 0
    in_specs = [
        pl.BlockSpec((tm, d), lambda i, f: (i, 0)),
        pl.BlockSpec((tm, row_ssq.shape[1]), lambda i, f: (i, 0)),
        pl.BlockSpec((None, d, tf), lambda i, f: (layer, 0, f)),
        pl.BlockSpec((None, d, tf), lambda i, f: (layer, 0, nf + f)),
        pl.BlockSpec((CONV_W, tf), lambda i, f: (0, f)),
        pl.BlockSpec((CONV_W, tf), lambda i, f: (0, nf + f)),
        pl.BlockSpec((1, tf), lambda i, f: (0, f)),
        pl.BlockSpec((1, tf), lambda i, f: (0, nf + f)),
    ]
    args = [xg, row_ssq, w_up, w_up, w_conv, w_conv, b_conv, b_conv]
    scratch = [pltpu.VMEM((tm, LANES), F32)]
    if has_prev:
        in_specs += [pl.BlockSpec((S, halo, tf), lambda i, f: (0, 0, f)),
                     pl.BlockSpec((S, halo, tf), lambda i, f: (0, 0, nf + f))]
        args += [prev, prev]
    else:
        scratch.append(pltpu.VMEM((nf, 2, SUBLANES, tf), F32))
    est = (2 * tm * d * 2 + 2 * 2 * d * tf * 2 + 2 * tm * tf * 2 + 12 * rb * tf * 4
           + 2 * tm * row_ssq.shape[1] * 4 + tm * LANES * 4)
    kern = functools.partial(_ffn_up_kernel, rows_per_seq=seq_len, tiles_per_seq=tiles_per_seq,
                             has_prev=has_prev, rb=rb, norm_dim=d)
    h, tail_a, tail_g = pl.pallas_call(
        kern,
        grid=(n_tiles, nf),
        in_specs=in_specs,
        out_specs=[pl.BlockSpec((tm, tf), lambda i, f: (i, f)),
                   pl.BlockSpec((S, halo, tf), lambda i, f: (i, 0, f)),
                   pl.BlockSpec((S, halo, tf), lambda i, f: (i, 0, f))],
        out_shape=[jax.ShapeDtypeStruct((rows, ff), BF16),
                   jax.ShapeDtypeStruct((n_tiles * S, halo, ff), F32),
                   jax.ShapeDtypeStruct((n_tiles * S, halo, ff), F32)],
        scratch_shapes=scratch,
        compiler_params=_params(("arbitrary", "arbitrary"), est),
        name="ffn_up",
    )(*args)
    tails = jnp.concatenate([tail_a, tail_g], axis=-1)
    if not has_prev:
        tails = tails[tiles_per_seq - 1::tiles_per_seq]
    return h, tails


def _gla_tables():
    C = CHUNK
    t = np.arange(C)[:, None]
    s = np.arange(C)[None, :]
    mats = [(s <= t)]
    masks = [(s == t)]
    m = C
    while m >= 2:
        mid = (t // m) * m + m // 2
        mats.append(((s > mid) & (s <= t)) | ((s > t) & (s <= mid)))
        masks.append((t // m == s // m) & (t % m >= m // 2) & (s % m < m // 2))
        m //= 2
    mats.append(s > t)
    return (np.concatenate(mats, axis=0).astype(np.float32),
            np.stack(masks, axis=0).astype(np.float32))


def _gla_kernel(*refs, heads, dk, dv, n_chunks, zero_init):
    if zero_init:
        (q_ref, k_ref, v_ref, r_ref, lr_ref, wg_ref, bg_ref, gn_ref, tab_ref, msk_ref,
         o_ref, sout_ref, st_ref, la_ref) = refs
        s0_ref = None
    else:
        (q_ref, k_ref, v_ref, r_ref, lr_ref, wg_ref, bg_ref, gn_ref, tab_ref, msk_ref, s0_ref,
         o_ref, sout_ref, st_ref, la_ref) = refs
    t = pl.program_id(2)
    n_tiles = pl.num_programs(2)
    C = CHUNK
    n_levels = msk_ref.shape[0]

    @pl.when(t == 0)
    def _():
        for h in range(heads):
            if zero_init:
                st_ref[h] = jnp.zeros((dv, dk), F32)
            else:
                st_ref[h] = s0_ref[0, h].T

    z = _dot(lr_ref[...].astype(BF16), wg_ref[...]) + bg_ref[...]
    la_ref[...] = _log_sigmoid(z) * (1.0 / GLA_TAU)
    tab = tab_ref[...]

    def chunk(c, carry):
        r0 = pl.multiple_of(c * C, C)
        la = la_ref[pl.ds(r0, C), :]
        sums = None
        for part in _split_bf16(la, 2):
            p = _dot(tab, part)
            sums = p if sums is None else sums + p
        decay = jnp.exp(sums)
        for h in range(heads):
            ks = slice(h * dk, (h + 1) * dk)
            vs = slice(h * dv, (h + 1) * dv)
            q = q_ref[pl.ds(r0, C), ks].astype(F32)
            k = k_ref[pl.ds(r0, C), ks].astype(F32)
            v = v_ref[pl.ds(r0, C), vs]
            att = msk_ref[0] * _nt(q.astype(BF16), k.astype(BF16))
            for lvl in range(1, n_levels):
                e = decay[lvl * C:(lvl + 1) * C, ks]
                att = att + msk_ref[lvl] * _nt((q * e).astype(BF16), (k * e).astype(BF16))
            e_in = decay[0:C, ks]
            st = st_ref[h]
            o = _dot(att.astype(BF16), v) + _nt((q * e_in).astype(BF16), st.astype(BF16))
            k_end = (k * decay[n_levels * C:(n_levels + 1) * C, ks]).astype(BF16)
            st_ref[h] = st * e_in[C - 1:C, :] + _tn(v, k_end)
            on = o * lax.rsqrt(jnp.mean(o * o, axis=-1, keepdims=True) + EPS) * gn_ref[...]
            gate = _silu(r_ref[pl.ds(r0, C), vs].astype(F32))
            o_ref[pl.ds(r0, C), vs] = (on * gate).astype(o_ref.dtype)
        return carry

    lax.fori_loop(0, n_chunks, chunk, 0, unroll=True)

    @pl.when(t == n_tiles - 1)
    def _():
        for h in range(heads):
            sout_ref[0, h] = st_ref[h].T


def gla(proj, lr, wg, bg, gn, s0, *, nseq, seq_len, n_heads, dk, dv):
    rows = proj.shape[0]
    hg = min(4, n_heads)
    n_groups = n_heads // hg
    tt = min(512, seq_len)
    n_tiles = seq_len // tt
    tab, msk = _gla_tables()
    tab = jnp.asarray(tab, BF16)
    msk = jnp.asarray(msk, F32)
    qw, vw = hg * dk, hg * dv
    k_off = (n_heads * dk) // qw
    v_off = (2 * n_heads * dk) // vw
    r_off = (2 * n_heads * dk + n_heads * dv) // vw
    row = lambda b, g, t: b * n_tiles + t
    in_specs = [
        pl.BlockSpec((tt, qw), lambda b, g, t: (row(b, g, t), g)),
        pl.BlockSpec((tt, qw), lambda b, g, t: (row(b, g, t), k_off + g)),
        pl.BlockSpec((tt, vw), lambda b, g, t: (row(b, g, t), v_off + g)),
        pl.BlockSpec((tt, vw), lambda b, g, t: (row(b, g, t), r_off + g)),
        pl.BlockSpec((tt, LANES), lambda b, g, t: (row(b, g, t), 0)),
        pl.BlockSpec((LANES, qw), lambda b, g, t: (0, g)),
        pl.BlockSpec((1, qw), lambda b, g, t: (0, g)),
        pl.BlockSpec((1, dv), lambda b, g, t: (0, 0)),
        pl.BlockSpec(tab.shape, lambda b, g, t: (0, 0)),
        pl.BlockSpec(msk.shape, lambda b, g, t: (0, 0, 0)),
    ]
    args = [proj, proj, proj, proj, lr, wg, bg, gn, tab, msk]
    zero_init = s0 is None
    if not zero_init:
        in_specs.append(pl.BlockSpec((1, hg, dk, dv), lambda b, g, t: (b, g, 0, 0)))
        args.append(s0)
    est = (2 * tt * (2 * qw + 2 * vw) * 2 + 2 * tt * LANES * 4 + 2 * tt * vw * 2
           + 4 * hg * dk * dv * 4 + hg * dk * dv * 4 + tt * qw * 4 + 4 * tab.shape[0] * qw * 4)
    kern = functools.partial(_gla_kernel, heads=hg, dk=dk, dv=dv, n_chunks=tt // CHUNK, zero_init=zero_init)
    o, s_out = pl.pallas_call(
        kern,
        grid=(nseq, n_groups, n_tiles),
        in_specs=in_specs,
        out_specs=[pl.BlockSpec((tt, vw), lambda b, g, t: (row(b, g, t), g)),
                   pl.BlockSpec((1, hg, dk, dv), lambda b, g, t: (b, g, 0, 0))],
        out_shape=[jax.ShapeDtypeStruct((rows, n_heads * dv), BF16),
                   jax.ShapeDtypeStruct((nseq, n_heads, dk, dv), F32)],
        scratch_shapes=[pltpu.VMEM((hg, dv, dk), F32), pltpu.VMEM((tt, qw), F32)],
        compiler_params=_params(("parallel", "parallel", "arbitrary"), est),
        name="gla",
    )(*args)
    return o, s_out


def _mem_attn_kernel(q_ref, k_ref, v_ref, o_ref, *, n_heads):
    for h in range(n_heads):
        cs = slice(h * MEM_DH, (h + 1) * MEM_DH)
        s = _nt(q_ref[:, cs], k_ref[0, :, cs].astype(BF16))
        p = jnp.exp(s - jnp.max(s, axis=-1, keepdims=True))
        l = jnp.sum(p, axis=-1, keepdims=True)
        o = _dot(p.astype(BF16), v_ref[0, :, cs].astype(BF16))
        o_ref[:, cs] = (o / l).astype(o_ref.dtype)


def mem_attn(qsrc, q_col_block, mk, mv, *, seq_len):
    rows = qsrc.shape[0]
    nseq, n_mem, mq = mk.shape
    tm = min(1024, seq_len)
    tiles_per_seq = seq_len // tm
    est = 2 * tm * mq * 2 * 2 + 2 * 2 * n_mem * mq * 4 + 6 * tm * n_mem * 4
    return pl.pallas_call(
        functools.partial(_mem_attn_kernel, n_heads=mq // MEM_DH),
        grid=(rows // tm,),
        in_specs=[pl.BlockSpec((tm, mq), lambda i: (i, q_col_block)),
                  pl.BlockSpec((1, n_mem, mq), lambda i: (i // tiles_per_seq, 0, 0)),
                  pl.BlockSpec((1, n_mem, mq), lambda i: (i // tiles_per_seq, 0, 0))],
        out_specs=pl.BlockSpec((tm, mq), lambda i: (i, 0)),
        out_shape=jax.ShapeDtypeStruct((rows, mq), BF16),
        compiler_params=_params(("parallel",), est),
        name="mem_attn",
    )(qsrc, mk, mv)


def _cumsum_kernel(lf_ref, c_ref, ct_ref, crow_ref, ccol_ref):
    j = pl.program_id(1)
    tb = lf_ref.shape[1]

    @pl.when(j == 0)
    def _():
        crow_ref[...] = jnp.zeros_like(crow_ref)
        ccol_ref[...] = jnp.zeros_like(ccol_ref)

    r = lax.broadcasted_iota(jnp.int32, (tb, tb), 0)
    s = lax.broadcasted_iota(jnp.int32, (tb, tb), 1)
    lower = jnp.where(s <= r, 1.0, 0.0).astype(BF16)
    upper = jnp.where(r <= s, 1.0, 0.0).astype(BF16)
    c = crow_ref[0:1, :]
    ct = ccol_ref[:, 0:1]
    for part in _split_bf16(lf_ref[0], 3):
        c = c + _dot(lower, part)
        ct = ct + _tn(part, upper)
    c_ref[0] = c
    ct_ref[0] = ct
    crow_ref[...] = jnp.broadcast_to(c[tb - 1:tb, :], crow_ref.shape)
    ccol_ref[...] = jnp.broadcast_to(ct[:, tb - 1:tb], ccol_ref.shape)


def cumsum_time(lf):
    nseq, L, w = lf.shape
    tb = L if L <= 2048 else 512
    est = 4 * tb * w * 4 + 4 * tb * tb * 4 + 8 * tb * w * 4
    return pl.pallas_call(
        _cumsum_kernel,
        grid=(nseq, L // tb),
        in_specs=[pl.BlockSpec((1, tb, w), lambda b, j: (b, j, 0))],
        out_specs=[pl.BlockSpec((1, tb, w), lambda b, j: (b, j, 0)),
                   pl.BlockSpec((1, w, tb), lambda b, j: (b, 0, j))],
        out_shape=[jax.ShapeDtypeStruct((nseq, L, w), F32), jax.ShapeDtypeStruct((nseq, w, L), F32)],
        scratch_shapes=[pltpu.VMEM((SUBLANES, w), F32), pltpu.VMEM((w, LANES), F32)],
        compiler_params=_params(("parallel", "arbitrary"), est),
        name="cumsum_time",
    )(lf)


def _pick_lane(x, lane):
    ids = lax.broadcasted_iota(jnp.int32, x.shape, 1)
    return jnp.sum(jnp.where(ids == lane, x, 0.0), axis=-1, keepdims=True)


LOG2E = 1.4426950408889634
FOX_Q_BLOCK = 2048
FOX_ROW_CHAIN = 512
FOX_K_BLOCK = 2048
FOX_AUG = 3


def _fox_kernel(q_ref, k_ref, v_ref, c_ref, o_ref, augq_ref, augk_ref, vaug_ref, *chain_refs, tk, tkf):
    h = pl.program_id(1)
    qi = pl.program_id(2)
    seq_len, dh = k_ref.shape
    tq = q_ref.shape[0]
    n_sub = tq // tk
    chains = [chain_refs[3 * i:3 * i + 3] for i in range(n_sub)]

    @pl.when(qi == 0)
    def _():
        def build(r, carry):
            r0 = pl.multiple_of(r * tk, tk)
            cc = _pick_lane(c_ref[0, pl.ds(r0, tk), :], h) * LOG2E
            lane = lax.broadcasted_iota(jnp.int32, (tk, dh), 1)
            aq = jnp.where(lane < 2 * FOX_AUG, 1.0, 0.0)
            ak = aq
            for n, part in enumerate(_split_bf16(cc, FOX_AUG)):
                aq = jnp.where(lane == n, part.astype(F32), aq)
                ak = jnp.where(lane == FOX_AUG + n, -part.astype(F32), ak)
            augq_ref[pl.ds(r0, tk), :] = aq.astype(BF16)
            augk_ref[pl.ds(r0, tk), :] = ak.astype(BF16)
            vaug_ref[pl.ds(r0, tk), 0:dh] = v_ref[pl.ds(r0, tk), :]
            vaug_ref[pl.ds(r0, tk), dh:2 * dh] = jnp.ones((tk, dh), BF16)
            return carry

        lax.fori_loop(0, seq_len // tk, build, 0)

    q0 = pl.multiple_of(qi * tq, tq)
    for sub, (qa_ref, m_ref, acc_ref) in enumerate(chains):
        qa_ref[:, 0:dh] = q_ref[sub * tk:(sub + 1) * tk, :]
        qa_ref[:, dh:2 * dh] = augq_ref[pl.ds(pl.multiple_of(q0 + sub * tk, tk), tk), :]
        m_ref[...] = jnp.full_like(m_ref, -jnp.inf)
        acc_ref[...] = jnp.zeros_like(acc_ref)

    def scores(sub, k0, width):
        ka = jnp.concatenate([k_ref[pl.ds(k0, width), :], augk_ref[pl.ds(k0, width), :]], axis=1)
        return _nt(chains[sub][0][...], ka)

    def softmax_update(sub, s, masked):
        m_ref = chains[sub][1]
        if masked:
            row = lax.broadcasted_iota(jnp.int32, s.shape, 0)
            col = lax.broadcasted_iota(jnp.int32, s.shape, 1)
            s = jnp.where(col <= row, s, -jnp.inf)
        blocks = [s[:, j * LANES:(j + 1) * LANES] for j in range(s.shape[1] // LANES)]
        mx = blocks[0]
        for blk in blocks[1:]:
            mx = jnp.maximum(mx, blk)
        m_old = m_ref[...]
        m_new = jnp.maximum(m_old, jnp.max(mx, axis=-1, keepdims=True))
        m_ref[...] = m_new
        alpha = jnp.exp2(m_old - m_new)
        pr = jnp.concatenate([jnp.exp2(blk - m_new) for blk in blocks], axis=1)
        return alpha, pr.astype(BF16)

    def accumulate(sub, k0, width, alpha, pr):
        acc_ref = chains[sub][2]
        acc_ref[...] = (jnp.concatenate([alpha] * (2 * dh // LANES), axis=1) * acc_ref[...]
                        + _dot(pr, vaug_ref[pl.ds(k0, width), :]))

    def step(jobs, width):
        ss = [scores(sub, k0, width) for sub, k0, _ in jobs]
        ps = [softmax_update(sub, s, masked) for (sub, _, masked), s in zip(jobs, ss)]
        for (sub, k0, _), (alpha, pr) in zip(jobs, ps):
            accumulate(sub, k0, width, alpha, pr)

    def full_chunk(j, carry):
        k0 = pl.multiple_of(j * tkf, tkf)
        step([(sub, k0, False) for sub in range(n_sub)], tkf)
        return carry

    lax.fori_loop(0, qi * (tq // tkf), full_chunk, 0)
    step([(sub, pl.multiple_of(q0 + d * tk, tk), sub == d) for d in range(n_sub) for sub in range(d, n_sub)], tk)
    for sub, (_, _, acc_ref) in enumerate(chains):
        o_ref[sub * tk:(sub + 1) * tk, :] = (acc_ref[:, 0:dh] / acc_ref[:, dh:2 * dh]).astype(o_ref.dtype)


def fox_prompt(qsrc, kbf, vbf, c, *, nseq, seq_len, n_heads, dh):
    rows = qsrc.shape[0]
    tq = min(FOX_Q_BLOCK, seq_len)
    tk = min(FOX_ROW_CHAIN, tq)
    tkf = min(FOX_K_BLOCK, tq)
    nq = seq_len // tq
    est = (2 * tq * dh * 2 * 2 + 2 * 2 * seq_len * dh * 2 + 2 * seq_len * LANES * 4 + 4 * seq_len * dh * 2
           + tq * 2 * dh * 2 + 3 * tq * LANES * 4 + (tq // tk) * 4 * tk * tkf * 4)
    return pl.pallas_call(
        functools.partial(_fox_kernel, tk=tk, tkf=tkf),
        grid=(nseq, n_heads, nq),
        in_specs=[
            pl.BlockSpec((tq, dh), lambda b, h, qi: (b * nq + qi, h)),
            pl.BlockSpec((seq_len, dh), lambda b, h, qi: (b, h)),
            pl.BlockSpec((seq_len, dh), lambda b, h, qi: (b, h)),
            pl.BlockSpec((1, seq_len, LANES), lambda b, h, qi: (b, 0, 0)),
        ],
        out_specs=pl.BlockSpec((tq, dh), lambda b, h, qi: (b * nq + qi, h)),
        out_shape=jax.ShapeDtypeStruct((rows, n_heads * dh), BF16),
        scratch_shapes=[pltpu.VMEM((seq_len, dh), BF16), pltpu.VMEM((seq_len, dh), BF16),
                        pltpu.VMEM((seq_len, 2 * dh), BF16)]
        + [pltpu.VMEM((tk, 2 * dh), BF16), pltpu.VMEM((tk, LANES), F32),
           pltpu.VMEM((tk, 2 * dh), F32)] * (tq // tk),
        compiler_params=_params(("parallel", "parallel", "arbitrary"), est),
        name="fox_prompt",
    )(qsrc, kbf, vbf, c)


FOX_DECODE_KEYS = 512


def _fox_decode_kernel(q_ref, kc_ref, vc_ref, kn_ref, vn_ref, c_ref, ctc_ref, ctn_ref, o_ref,
                       m_ref, l_ref, acc_ref, *, n_heads, dh):
    j = pl.program_id(1)
    ts = q_ref.shape[0]
    pc = kc_ref.shape[1] // n_heads

    def cached(h):
        cs = slice(h * dh, (h + 1) * dh)
        cq = _pick_lane(c_ref[0], h) * LOG2E
        kc = kc_ref[0, pl.ds(h, pc, stride=n_heads), :].astype(BF16)
        vc = vc_ref[0, pl.ds(h, pc, stride=n_heads), :].astype(BF16)
        return cs, cq, _nt(q_ref[:, cs], kc) + cq - ctc_ref[0, h] * LOG2E, vc

    @pl.when(j == 0)
    def _():
        row = lax.broadcasted_iota(jnp.int32, (ts, ts), 0)
        col = lax.broadcasted_iota(jnp.int32, (ts, ts), 1)
        for h in range(n_heads):
            cs, cq, s, vc = cached(h)
            sn = _nt(q_ref[:, cs], kn_ref[:, cs]) + cq - ctn_ref[0, h] * LOG2E
            s = jnp.concatenate([s, jnp.where(col <= row, sn, -jnp.inf)], axis=1)
            m = jnp.max(s, axis=-1, keepdims=True)
            p = jnp.exp2(s - m)
            m_ref[h] = m
            l_ref[h] = jnp.sum(p, axis=-1, keepdims=True)
            acc_ref[h] = _dot(p.astype(BF16), jnp.concatenate([vc, vn_ref[:, cs]], axis=0))

    @pl.when(j > 0)
    def _():
        for h in range(n_heads):
            _, _, s, vc = cached(h)
            m_old = m_ref[h]
            m_new = jnp.maximum(m_old, jnp.max(s, axis=-1, keepdims=True))
            alpha = jnp.exp2(m_old - m_new)
            p = jnp.exp2(s - m_new)
            l_ref[h] = alpha * l_ref[h] + jnp.sum(p, axis=-1, keepdims=True)
            acc_ref[h] = alpha * acc_ref[h] + _dot(p.astype(BF16), vc)
            m_ref[h] = m_new

    @pl.when(j == pl.num_programs(1) - 1)
    def _():
        for h in range(n_heads):
            cs = slice(h * dh, (h + 1) * dh)
            o_ref[:, cs] = (acc_ref[h] / l_ref[h]).astype(o_ref.dtype)


def fox_decode(qsrc, kcache, vcache, kbf, vbf, c, ct4, *, n_heads, dh):
    nseq = kcache.shape[0]
    past = kcache.shape[1] // n_heads
    rows = qsrc.shape[0]
    ts = rows // nseq
    w = n_heads * dh
    pc = _pick_tile(past, FOX_DECODE_KEYS, LANES)
    ct_new = ct4[..., past:]
    est = (2 * 2 * pc * w * 4 + 8 * ts * w * 2 + 2 * ts * LANES * 4 + n_heads * ts * (dh + 2 * LANES) * 4
           + 8 * ts * pc * 4 + 4 * pc * dh * 4)
    return pl.pallas_call(
        functools.partial(_fox_decode_kernel, n_heads=n_heads, dh=dh),
        grid=(nseq, past // pc),
        in_specs=[
            pl.BlockSpec((ts, w), lambda b, j: (b, 0)),
            pl.BlockSpec((1, pc * n_heads, dh), lambda b, j: (b, j, 0)),
            pl.BlockSpec((1, pc * n_heads, dh), lambda b, j: (b, j, 0)),
            pl.BlockSpec((ts, w), lambda b, j: (b, 0)),
            pl.BlockSpec((ts, w), lambda b, j: (b, 0)),
            pl.BlockSpec((1, ts, LANES), lambda b, j: (b, past // ts, 0)),
            pl.BlockSpec((1, n_heads, 1, pc), lambda b, j: (b, 0, 0, j)),
            pl.BlockSpec((1, n_heads, 1, ts), lambda b, j: (b, 0, 0, 0)),
        ],
        out_specs=pl.BlockSpec((ts, w), lambda b, j: (b, 0)),
        out_shape=jax.ShapeDtypeStruct((rows, w), BF16),
        scratch_shapes=[pltpu.VMEM((n_heads, ts, 1), F32), pltpu.VMEM((n_heads, ts, 1), F32),
                        pltpu.VMEM((n_heads, ts, dh), F32)],
        compiler_params=_params(("parallel", "arbitrary"), est),
        name="fox_decode",
    )(qsrc, kcache, vcache, kbf, vbf, c, ct4, ct_new)


def _pad_cols(x, width):
    return jnp.pad(x, ((0, 0), (0, width - x.shape[1])))


def _ffn_block(x, xg, ssq, w_up, w_down, layer, w_conv, b_conv, prev, next_gains, *, nseq, seq_len):
    tf = _pick_tile(w_down.shape[1], 256, LANES)
    h, tails = ffn_up(xg, ssq, w_up, layer, w_conv, b_conv, prev, nseq=nseq, seq_len=seq_len, tf=tf)
    outs = matmul([h], w_down, [F32], tm=512, tn=512, w_layer=layer, res=x, next_gains=next_gains, name="ffn_down")
    return outs, tails


def kernel(x_prompt, x_sample, state_gla, cache_fox_k, cache_fox_v, cache_fox_logf, cache_mem_k, cache_mem_v,
           state_ffn_conv, mem_prompt, g_attn, g_ffn, g_mem, w_mem_kv, w_in_a, w_gate_a, b_gate_a, g_gla,
           w_in_b, g_kv, w_kv, b_forget, w_out, w_up, w_conv, b_conv, w_down, g_final):
    n_p, t_p, d = x_prompt.shape
    n_s, t_s, _ = x_sample.shape
    depth = g_attn.shape[0]
    n_a = state_gla.shape[0]
    _, _, gla_heads, dk, dv = state_gla.shape
    _, past, fox_heads, dh = cache_fox_k.shape
    n_mem, mq = cache_mem_k.shape[2:]
    a_q, a_v, b_q = gla_heads * dk, gla_heads * dv, fox_heads * dh

    def mixer_gains(l):
        return jnp.stack([g_attn[l]] + ([g_kv] if l == n_a else []))

    streams = []
    for x3, nseq, seq_len in ((x_prompt, n_p, t_p), (x_sample, n_s, t_s)):
        x = x3.reshape(nseq * seq_len, d)
        streams.append(dict(x=x, nseq=nseq, seq_len=seq_len, normed=rmsnorm(x, mixer_gains(0), BF16), ssq=None))
    mem_rows = mem_prompt.reshape(n_p * n_mem, d)
    mk_out, mv_out, conv_out = [], [], [[], []]
    gla_out = [[], []]
    fox = [None, None]

    w_mem_bf, w_in_a_bf, w_in_b_bf, w_kv_bf = (w.astype(BF16) for w in (w_mem_kv, w_in_a, w_in_b, w_kv))
    w_out_bf, w_up_bf, w_down_bf = (w.astype(BF16) for w in (w_out, w_up, w_down))
    mem_scale = jnp.full((1, mq), MEM_DH ** -0.5, F32)
    main = 2 * a_q + 2 * a_v

    for l in range(depth):
        (mem_n,) = rmsnorm(mem_rows, g_mem[l][None], BF16)
        (mkv,) = matmul([mem_n], w_mem_bf, [F32], tm=512, tn=1024, w_layer=l, name="mem_kv")
        mk_p = mkv[:, :mq].reshape(n_p, n_mem, mq)
        mv_p = mkv[:, mq:].reshape(n_p, n_mem, mq)
        mk_out.append(mk_p)
        mv_out.append(mv_p)
        mem_kv = [(mk_p, mv_p), (cache_mem_k[l], cache_mem_v[l])]

        if l < n_a:
            w_lr = _pad_cols(w_in_a[l][:, main:main + GLA_RANK], LANES).astype(BF16)
            w_qm = w_in_a[l][:, main + GLA_RANK:].astype(BF16)
            col_scale = jnp.concatenate([jnp.full((a_q,), dk ** -0.5, F32), jnp.ones((a_q + 2 * a_v,), F32)])[None]
            wg = jnp.pad(w_gate_a[l], ((0, LANES - GLA_RANK), (0, 0))).astype(BF16)
            bg = b_gate_a[l][None]
            gn = g_gla[l][None]
        else:
            j = l - n_a
            col_scale = jnp.concatenate([jnp.full((b_q,), LOG2E * dh ** -0.5, F32), mem_scale[0]])[None]
            if l == n_a:
                w_kv_f = _pad_cols(w_kv[:, 2 * b_q:], LANES).astype(BF16)
                b_f = _pad_cols(b_forget[None], LANES)
        ffn_next_gains = mixer_gains(l + 1) if l + 1 < depth else None

        for si, st in enumerate(streams):
            x, nseq, seq_len = st["x"], st["nseq"], st["seq_len"]
            mk, mv = mem_kv[si]
            xn = st["normed"][0]
            norm = dict(row_ssq=st["ssq"], norm_dim=d)
            if l < n_a:
                (proj,) = matmul([xn], w_in_a_bf, [BF16], tm=1024, tn=1024, w_layer=l, n=main, scale=col_scale,
                                 name="in_proj_a", **norm)
                (qm,) = matmul([xn], w_qm, [BF16], tm=1024, tn=1024, scale=mem_scale, name="in_proj_a_mem", **norm)
                (lr,) = matmul([xn], w_lr, [F32], tm=1024, tn=LANES, name="in_proj_a_gate", **norm)
                s0 = None if si == 0 else state_gla[l]
                o, s_new = gla(proj, lr, wg, bg, gn, s0, nseq=nseq, seq_len=seq_len,
                               n_heads=gla_heads, dk=dk, dv=dv)
                gla_out[si].append(s_new)
                mo = mem_attn(qm, 0, mk, mv, seq_len=seq_len)
            else:
                if l == n_a:
                    xkv = st["normed"][1]
                    k32, kbf = matmul([xkv], w_kv_bf, [F32, BF16], tm=1024, tn=1024, n=b_q, head_dims=[dh, None],
                                      name="k_proj", **norm)
                    v32, vbf = matmul([xkv], w_kv_bf, [F32, BF16], tm=1024, tn=1024, w_col0=b_q, n=b_q,
                                      head_dims=[dh, None], name="v_proj", **norm)
                    (lf,) = matmul([xkv], w_kv_f, [F32], tm=1024, tn=LANES, bias=b_f, act="log_sigmoid",
                                   name="kv_proj_forget", **norm)
                    lf3 = lf.reshape(nseq, seq_len, LANES)
                    if si == 0:
                        lf_all = lf3
                    else:
                        lf_all = jnp.concatenate(
                            [jnp.pad(cache_fox_logf, ((0, 0), (0, 0), (0, LANES - fox_heads))), lf3], axis=1)
                    c, ct = cumsum_time(lf_all)
                    ct4 = ct[:, :fox_heads].reshape(nseq, fox_heads, 1, ct.shape[-1])
                    fox[si] = dict(k32=k32, v32=v32, kbf=kbf, vbf=vbf, lf=lf3[:, :, :fox_heads], c=c, ct4=ct4)
                fx = fox[si]
                (proj,) = matmul([xn], w_in_b_bf, [BF16], tm=1024, tn=1024, w_layer=j, scale=col_scale,
                                 name="in_proj_b", **norm)
                kb, vb = fx["kbf"], fx["vbf"]
                if si == 0:
                    o = fox_prompt(proj, kb, vb, fx["c"], nseq=nseq, seq_len=seq_len,
                                   n_heads=fox_heads, dh=dh)
                else:
                    o = fox_decode(proj, cache_fox_k.reshape(nseq, past * fox_heads, dh),
                                   cache_fox_v.reshape(nseq, past * fox_heads, dh),
                                   kb, vb, fx["c"], fx["ct4"], n_heads=fox_heads, dh=dh)
                mo = mem_attn(proj, b_q // mq, mk, mv, seq_len=seq_len)
            x, xg, ssq = matmul([o, mo], w_out_bf, [F32], tm=1024, tn=512, w_layer=l, res=x,
                                next_gains=g_ffn[l][None], name="out_proj")
            prev = None if si == 0 else state_ffn_conv[l]
            outs, tails = _ffn_block(x, xg, ssq, w_up_bf, w_down_bf, l, w_conv[l], b_conv[l][None], prev,
                                     ffn_next_gains, nseq=nseq, seq_len=seq_len)
            conv_out[si].append(tails)
            st["x"] = outs[0]
            if ffn_next_gains is not None:
                st["normed"], st["ssq"] = outs[1:-1], outs[-1]

    ys = []
    for st in streams:
        (y,) = rmsnorm(st["x"], g_final[None], F32)
        ys.append(y)
    fp, fs = fox
    return (ys[0].reshape(n_p, t_p, d), ys[1].reshape(n_s, t_s, d),
            jnp.stack(gla_out[0]), jnp.stack(gla_out[1]),
            fp["k32"].reshape(n_p, t_p, fox_heads, dh), fp["v32"].reshape(n_p, t_p, fox_heads, dh), fp["lf"],
            fs["k32"].reshape(n_s, t_s, fox_heads, dh), fs["v32"].reshape(n_s, t_s, fox_heads, dh), fs["lf"],
            jnp.stack(mk_out), jnp.stack(mv_out),
            jnp.stack(conv_out[0]), jnp.stack(conv_out[1]))
```

```python
import functools

import numpy as np
import jax
import jax.numpy as jnp
from jax import lax
from jax.experimental import pallas as pl
from jax.experimental.pallas import tpu as pltpu

F32 = jnp.float32
BF16 = jnp.bfloat16

CHUNK = 64
GLA_RANK = 16
GLA_TAU = 16.0
MEM_DH = 256
CONV_W = 3
EPS = 1e-6

LANES = 128
SUBLANES = 8
VMEM_PHYSICAL_BYTES = 64 * 1024 * 1024
VMEM_HEADROOM_BYTES = 6 * 1024 * 1024


def _vmem_limit(estimate_bytes):
    want = int(estimate_bytes * 1.25) + 4 * 1024 * 1024
    return int(min(max(want, 16 * 1024 * 1024), VMEM_PHYSICAL_BYTES - VMEM_HEADROOM_BYTES))


def _params(semantics, estimate_bytes):
    return pltpu.CompilerParams(dimension_semantics=semantics, vmem_limit_bytes=_vmem_limit(estimate_bytes))


def _pick_tile(dim, preferred, unit):
    best = None
    for t in range(unit, min(dim, preferred) + 1, unit):
        if dim % t == 0:
            best = t
    assert best is not None, (dim, preferred, unit)
    return best


def _nt(a, b):
    return lax.dot_general(a, b, (((1,), (1,)), ((), ())), preferred_element_type=F32)


def _tn(a, b):
    return lax.dot_general(a, b, (((0,), (0,)), ((), ())), preferred_element_type=F32)


def _dot(a, b):
    return jnp.dot(a, b, preferred_element_type=F32)


def _split_bf16(x, parts):
    out = []
    rem = x
    for _ in range(parts):
        hi = rem.astype(BF16)
        out.append(hi)
        rem = rem - hi.astype(F32)
    return out


def _log_sigmoid(z):
    return jnp.minimum(z, 0.0) - jnp.log(1.0 + jnp.exp(-jnp.abs(z)))


def _silu(z):
    half = 0.5 * z
    return half + half * jnp.tanh(half)


def _rmsnorm_kernel(x_ref, g_ref, *o_refs):
    x = x_ref[...]
    y = x * lax.rsqrt(jnp.mean(x * x, axis=-1, keepdims=True) + EPS)
    for i, o_ref in enumerate(o_refs):
        o_ref[...] = (y * g_ref[i:i + 1, :]).astype(o_ref.dtype)


def rmsnorm(x, gains, out_dtype):
    rows, d = x.shape
    n = gains.shape[0]
    tm = min(256, rows)
    out_bytes = jnp.dtype(out_dtype).itemsize
    est = 2 * tm * d * 4 + n * 2 * tm * d * out_bytes + 2 * tm * d * 4
    outs = pl.pallas_call(
        _rmsnorm_kernel,
        grid=(rows // tm,),
        in_specs=[pl.BlockSpec((tm, d), lambda i: (i, 0)),
                  pl.BlockSpec((n, d), lambda i: (0, 0))],
        out_specs=[pl.BlockSpec((tm, d), lambda i: (i, 0)) for _ in range(n)],
        out_shape=[jax.ShapeDtypeStruct((rows, d), out_dtype) for _ in range(n)],
        compiler_params=_params(("parallel",), est),
        name="rmsnorm",
    )(x, gains)
    return list(outs)


def _row_rsqrt(ssq, norm_dim):
    total = ssq[:, 0:LANES]
    for j in range(1, ssq.shape[1] // LANES):
        total = total + ssq[:, j * LANES:(j + 1) * LANES]
    return lax.rsqrt(total * (1.0 / norm_dim) + EPS)


def _tile_lanes(v, width):
    return jnp.concatenate([v] * (width // LANES), axis=1) if width > LANES else v


def _matmul_kernel(*refs, n_parts, norm_dim, has_scale, has_bias, has_res, act, n_out, n_gains):
    a_refs = refs[:n_parts]
    w_refs = refs[n_parts:2 * n_parts]
    pos = 2 * n_parts
    ssq_ref = scale_ref = bias_ref = res_ref = gains_ref = None
    if norm_dim:
        ssq_ref = refs[pos]; pos += 1
    if has_scale:
        scale_ref = refs[pos]; pos += 1
    if has_bias:
        bias_ref = refs[pos]; pos += 1
    if has_res:
        res_ref = refs[pos]; pos += 1
    if n_gains:
        gains_ref = refs[pos]; pos += 1
    o_refs = refs[pos:pos + n_out]
    pos += n_out
    xg_refs = refs[pos:pos + n_gains]
    ssq_out_ref = refs[pos + n_gains] if n_gains else None

    y = _dot(a_refs[0][...], w_refs[0][...])
    for a_ref, w_ref in zip(a_refs[1:], w_refs[1:]):
        y = y + _dot(a_ref[...], w_ref[...])
    if norm_dim:
        y = y * _tile_lanes(_row_rsqrt(ssq_ref[...], norm_dim), y.shape[1])
    if has_scale:
        y = y * scale_ref[...]
    if has_bias:
        y = y + bias_ref[...]
    if act == "log_sigmoid":
        y = _log_sigmoid(y)
    if has_res:
        y = y + res_ref[...]
    for o_ref in o_refs:
        if len(o_ref.shape) == 3:
            rows_blk, heads, dh = o_ref.shape
            flat = o_ref.reshape(rows_blk * heads, dh)
            for hh in range(heads):
                flat[pl.ds(hh, rows_blk, stride=heads), :] = y[:, hh * dh:(hh + 1) * dh].astype(o_ref.dtype)
        else:
            o_ref[...] = y.astype(o_ref.dtype)
    for gi, xg_ref in enumerate(xg_refs):
        xg_ref[...] = (y * gains_ref[gi:gi + 1, :]).astype(xg_ref.dtype)
    if n_gains:
        part = jnp.broadcast_to(jnp.sum(y * y, axis=-1, keepdims=True), ssq_out_ref.shape)
        j = pl.program_id(1)

        @pl.when(j == 0)
        def _():
            ssq_out_ref[...] = part

        @pl.when(j > 0)
        def _():
            ssq_out_ref[...] += part


def matmul(a_parts, w, out_dtypes, *, tm, tn, w_layer=None, w_col0=0, n=None, row_ssq=None, norm_dim=None,
           scale=None, bias=None, res=None, act=None, head_dims=None, next_gains=None, name="matmul"):
    rows = a_parts[0].shape[0]
    kdim, n_w = w.shape[-2:]
    n = n_w if n is None else n
    tm = _pick_tile(rows, tm, SUBLANES)
    tn = _pick_tile(n, tn, LANES)
    assert sum(a.shape[1] for a in a_parts) == kdim and w_col0 % tn == 0 and w_col0 + n <= n_w
    col0 = w_col0 // tn
    a_specs, w_specs = [], []
    start = 0
    for a in a_parts:
        kp = a.shape[1]
        assert start % kp == 0, "a part must start at a multiple of its own width in W"
        a_specs.append(pl.BlockSpec((tm, kp), lambda i, j: (i, 0)))
        if w.ndim == 3:
            w_specs.append(pl.BlockSpec((None, kp, tn), lambda i, j, b=start // kp: (w_layer, b, col0 + j)))
        else:
            w_specs.append(pl.BlockSpec((kp, tn), lambda i, j, b=start // kp: (b, col0 + j)))
        start += kp
    in_specs = a_specs + w_specs
    args = list(a_parts) + [w] * len(a_parts)
    if row_ssq is not None:
        in_specs.append(pl.BlockSpec((tm, row_ssq.shape[1]), lambda i, j: (i, 0)))
        args.append(row_ssq)
    for extra in (scale, bias):
        if extra is not None:
            in_specs.append(pl.BlockSpec((1, tn), lambda i, j: (0, j)))
            args.append(extra)
    if res is not None:
        in_specs.append(pl.BlockSpec((tm, tn), lambda i, j: (i, j)))
        args.append(res)
    n_gains = 0 if next_gains is None else next_gains.shape[0]
    if n_gains:
        in_specs.append(pl.BlockSpec((n_gains, tn), lambda i, j: (0, j)))
        args.append(next_gains)
    out_bytes = sum(jnp.dtype(d).itemsize for d in out_dtypes) + 2 * n_gains
    est = (2 * tm * kdim * 2 + 2 * kdim * tn * 2 + 2 * tm * tn * out_bytes
           + (2 * tm * tn * 4 if res is not None else 0) + 2 * tm * tn * 4
           + (2 * tm * row_ssq.shape[1] * 4 if row_ssq is not None else 0))
    kern = functools.partial(
        _matmul_kernel, n_parts=len(a_parts), norm_dim=norm_dim if row_ssq is not None else None,
        has_scale=scale is not None, has_bias=bias is not None, has_res=res is not None, act=act,
        n_out=len(out_dtypes), n_gains=n_gains)
    head_dims = head_dims or [None] * len(out_dtypes)
    out_specs, out_shape = [], []
    for dt, dh in zip(out_dtypes, head_dims):
        if dh is None:
            out_specs.append(pl.BlockSpec((tm, tn), lambda i, j: (i, j)))
            out_shape.append(jax.ShapeDtypeStruct((rows, n), dt))
        else:
            assert tn % (SUBLANES * dh) == 0 or tn == n
            out_specs.append(pl.BlockSpec((tm, tn // dh, dh), lambda i, j: (i, j, 0)))
            out_shape.append(jax.ShapeDtypeStruct((rows, n // dh, dh), dt))
    for _ in range(n_gains):
        out_specs.append(pl.BlockSpec((tm, tn), lambda i, j: (i, j)))
        out_shape.append(jax.ShapeDtypeStruct((rows, n), BF16))
    if n_gains:
        out_specs.append(pl.BlockSpec((tm, LANES), lambda i, j: (i, 0)))
        out_shape.append(jax.ShapeDtypeStruct((rows, LANES), F32))
    outs = pl.pallas_call(
        kern,
        grid=(rows // tm, n // tn),
        in_specs=in_specs,
        out_specs=out_specs,
        out_shape=out_shape,
        compiler_params=_params(("parallel", "arbitrary" if n_gains else "parallel"), est),
        name=name,
    )(*args)
    return list(outs)


FFN_ROW_BLOCK = 256
FFN_ROW_TILE = 2048


def _shift_one_row(u, first_row):
    rolled = pltpu.roll(u, 1, 0)
    head = rolled[0:SUBLANES]
    row = lax.broadcasted_iota(jnp.int32, head.shape, 0)
    head = jnp.where(row == 0, first_row, head)
    return jnp.concatenate([head, rolled[SUBLANES:]], axis=0)


def _ffn_up_kernel(*refs, rows_per_seq, tiles_per_seq, has_prev, rb, norm_dim):
    if has_prev:
        (xg_ref, ssq_ref, wa_ref, wg_ref, ca_ref, cg_ref, ba_ref, bg_ref, pa_ref, pg_ref,
         h_ref, ta_ref, tg_ref, rs_ref) = refs
        carry_ref = None
        prev_refs = (pa_ref, pg_ref)
    else:
        (xg_ref, ssq_ref, wa_ref, wg_ref, ca_ref, cg_ref, ba_ref, bg_ref,
         h_ref, ta_ref, tg_ref, rs_ref, carry_ref) = refs
        prev_refs = (None, None)
    i = pl.program_id(0)
    f = pl.program_id(1)
    tm = xg_ref.shape[0]
    tf = wa_ref.shape[1]
    halo = CONV_W - 1
    seg = min(rb, rows_per_seq)
    conv_refs = ((ca_ref, ba_ref, ta_ref), (cg_ref, bg_ref, tg_ref))

    @pl.when(f == 0)
    def _():
        rs_ref[...] = _row_rsqrt(ssq_ref[...], norm_dim)

    tails = [None, None]
    if not has_prev:
        @pl.when((i % tiles_per_seq) == 0)
        def _():
            carry_ref[f] = jnp.zeros(carry_ref.shape[1:], F32)

        tails = [carry_ref[f, 0, 0:halo, :], carry_ref[f, 1, 0:halo, :]]

    wa, wg = wa_ref[...], wg_ref[...]
    for j in range(tm // rb):
        xs = xg_ref[j * rb:(j + 1) * rb, :]
        rs = _tile_lanes(rs_ref[j * rb:(j + 1) * rb, :], tf)
        us = (_dot(xs, wa) * rs, _dot(xs, wg) * rs)
        for s in range(rb // seg):
            r0 = j * rb + s * seg
            zs = []
            for which, (c_ref, b_ref, t_ref) in enumerate(conv_refs):
                u = us[which][s * seg:(s + 1) * seg]
                prev = prev_refs[which][r0 // rows_per_seq] if has_prev else tails[which]
                z = b_ref[...] + c_ref[halo:halo + 1, :] * u
                shifted = u
                for d in range(1, CONV_W):
                    shifted = _shift_one_row(shifted, prev[halo - d:halo - d + 1, :])
                    z = z + c_ref[halo - d:halo - d + 1, :] * shifted
                tail = u[seg - halo:seg]
                if has_prev:
                    t_ref[r0 // rows_per_seq] = tail
                else:
                    tails[which] = tail
                zs.append(z)
            h_ref[r0:r0 + seg, :] = (_silu(zs[1]) * zs[0]).astype(h_ref.dtype)
    if not has_prev:
        for which, (_, _, t_ref) in enumerate(conv_refs):
            t_ref[0] = tails[which]
            carry_ref[f, which, 0:halo, :] = tails[which]


def ffn_up(xg, row_ssq, w_up, layer, w_conv, b_conv, prev, *, nseq, seq_len, tf):
    rows, d = xg.shape
    ff = w_up.shape[2] // 2
    nf = ff // tf
    has_prev = prev is not None
    halo = CONV_W - 1
    if has_prev:
        tm, tiles_per_seq, n_tiles, S = rows, 1, 1, nseq
    else:
        tm = min(FFN_ROW_TILE, seq_len)
        tiles_per_seq = seq_len // tm
        n_tiles, S = nseq * tiles_per_seq, 1
    rb = min(FFN_ROW_BLOCK, tm)
    assert seq_len % min(rb, seq_len) == 0 and rb % min(rb, seq_len) == 0
    in_specs = [
        pl.BlockSpec((tm, d), lambda i, f: (i, 0)),
        pl.BlockSpec((tm, row_ssq.shape[1]), lambda i, f: (i, 0)),
        pl.BlockSpec((None, d, tf), lambda i, f: (layer, 0, f)),
        pl.BlockSpec((None, d, tf), lambda i, f: (layer, 0, nf + f)),
        pl.BlockSpec((CONV_W, tf), lambda i, f: (0, f)),
        pl.BlockSpec((CONV_W, tf), lambda i, f: (0, nf + f)),
        pl.BlockSpec((1, tf), lambda i, f: (0, f)),
        pl.BlockSpec((1, tf), lambda i, f: (0, nf + f)),
    ]
    args = [xg, row_ssq, w_up, w_up, w_conv, w_conv, b_conv, b_conv]
    scratch = [pltpu.VMEM((tm, LANES), F32)]
    if has_prev:
        in_specs += [pl.BlockSpec((S, halo, tf), lambda i, f: (0, 0, f)),
                     pl.BlockSpec((S, halo, tf), lambda i, f: (0, 0, nf + f))]
        args += [prev, prev]
    else:
        scratch.append(pltpu.VMEM((nf, 2, SUBLANES, tf), F32))
    est = (2 * tm * d * 2 + 2 * 2 * d * tf * 2 + 2 * tm * tf * 2 + 12 * rb * tf * 4
           + 2 * tm * row_ssq.shape[1] * 4 + tm * LANES * 4)
    kern = functools.partial(_ffn_up_kernel, rows_per_seq=seq_len, tiles_per_seq=tiles_per_seq,
                             has_prev=has_prev, rb=rb, norm_dim=d)
    h, tail_a, tail_g = pl.pallas_call(
        kern,
        grid=(n_tiles, nf),
        in_specs=in_specs,
        out_specs=[pl.BlockSpec((tm, tf), lambda i, f: (i, f)),
                   pl.BlockSpec((S, halo, tf), lambda i, f: (i, 0, f)),
                   pl.BlockSpec((S, halo, tf), lambda i, f: (i, 0, f))],
        out_shape=[jax.ShapeDtypeStruct((rows, ff), BF16),
                   jax.ShapeDtypeStruct((n_tiles * S, halo, ff), F32),
                   jax.ShapeDtypeStruct((n_tiles * S, halo, ff), F32)],
        scratch_shapes=scratch,
        compiler_params=_params(("arbitrary", "arbitrary"), est),
        name="ffn_up",
    )(*args)
    tails = jnp.concatenate([tail_a, tail_g], axis=-1)
    if not has_prev:
        tails = tails[tiles_per_seq - 1::tiles_per_seq]
    return h, tails


def _gla_tables():
    C = CHUNK
    t = np.arange(C)[:, None]
    s = np.arange(C)[None, :]
    mats = [(s <= t)]
    masks = [(s == t)]
    m = C
    while m >= 2:
        mid = (t // m) * m + m // 2
        mats.append(((s > mid) & (s <= t)) | ((s > t) & (s <= mid)))
        masks.append((t // m == s // m) & (t % m >= m // 2) & (s % m < m // 2))
        m //= 2
    mats.append(s > t)
    return (np.concatenate(mats, axis=0).astype(np.float32),
            np.stack(masks, axis=0).astype(np.float32))


def _gla_kernel(*refs, heads, dk, dv, n_chunks, zero_init):
    if zero_init:
        (q_ref, k_ref, v_ref, r_ref, lr_ref, wg_ref, bg_ref, gn_ref, tab_ref, msk_ref,
         o_ref, sout_ref, st_ref, la_ref) = refs
        s0_ref = None
    else:
        (q_ref, k_ref, v_ref, r_ref, lr_ref, wg_ref, bg_ref, gn_ref, tab_ref, msk_ref, s0_ref,
         o_ref, sout_ref, st_ref, la_ref) = refs
    t = pl.program_id(2)
    n_tiles = pl.num_programs(2)
    C = CHUNK
    n_levels = msk_ref.shape[0]

    @pl.when(t == 0)
    def _():
        for h in range(heads):
            if zero_init:
                st_ref[h] = jnp.zeros((dv, dk), F32)
            else:
                st_ref[h] = s0_ref[0, h].T

    z = _dot(lr_ref[...].astype(BF16), wg_ref[...]) + bg_ref[...]
    la_ref[...] = _log_sigmoid(z) * (1.0 / GLA_TAU)
    tab = tab_ref[...]

    def chunk(c, carry):
        r0 = pl.multiple_of(c * C, C)
        la = la_ref[pl.ds(r0, C), :]
        sums = None
        for part in _split_bf16(la, 2):
            p = _dot(tab, part)
            sums = p if sums is None else sums + p
        decay = jnp.exp(sums)
        for h in range(heads):
            ks = slice(h * dk, (h + 1) * dk)
            vs = slice(h * dv, (h + 1) * dv)
            q = q_ref[pl.ds(r0, C), ks].astype(F32)
            k = k_ref[pl.ds(r0, C), ks].astype(F32)
            v = v_ref[pl.ds(r0, C), vs]
            att = msk_ref[0] * _nt(q.astype(BF16), k.astype(BF16))
            for lvl in range(1, n_levels):
                e = decay[lvl * C:(lvl + 1) * C, ks]
                att = att + msk_ref[lvl] * _nt((q * e).astype(BF16), (k * e).astype(BF16))
            e_in = decay[0:C, ks]
            st = st_ref[h]
            o = _dot(att.astype(BF16), v) + _nt((q * e_in).astype(BF16), st.astype(BF16))
            k_end = (k * decay[n_levels * C:(n_levels + 1) * C, ks]).astype(BF16)
            st_ref[h] = st * e_in[C - 1:C, :] + _tn(v, k_end)
            on = o * lax.rsqrt(jnp.mean(o * o, axis=-1, keepdims=True) + EPS) * gn_ref[...]
            gate = _silu(r_ref[pl.ds(r0, C), vs].astype(F32))
            o_ref[pl.ds(r0, C), vs] = (on * gate).astype(o_ref.dtype)
        return carry

    lax.fori_loop(0, n_chunks, chunk, 0, unroll=True)

    @pl.when(t == n_tiles - 1)
    def _():
        for h in range(heads):
            sout_ref[0, h] = st_ref[h].T


def gla(proj, lr, wg, bg, gn, s0, *, nseq, seq_len, n_heads, dk, dv):
    rows = proj.shape[0]
    hg = min(4, n_heads)
    n_groups = n_heads // hg
    tt = min(512, seq_len)
    n_tiles = seq_len // tt
    tab, msk = _gla_tables()
    tab = jnp.asarray(tab, BF16)
    msk = jnp.asarray(msk, F32)
    qw, vw = hg * dk, hg * dv
    k_off = (n_heads * dk) // qw
    v_off = (2 * n_heads * dk) // vw
    r_off = (2 * n_heads * dk + n_heads * dv) // vw
    row = lambda b, g, t: b * n_tiles + t
    in_specs = [
        pl.BlockSpec((tt, qw), lambda b, g, t: (row(b, g, t), g)),
        pl.BlockSpec((tt, qw), lambda b, g, t: (row(b, g, t), k_off + g)),
        pl.BlockSpec((tt, vw), lambda b, g, t: (row(b, g, t), v_off + g)),
        pl.BlockSpec((tt, vw), lambda b, g, t: (row(b, g, t), r_off + g)),
        pl.BlockSpec((tt, LANES), lambda b, g, t: (row(b, g, t), 0)),
        pl.BlockSpec((LANES, qw), lambda b, g, t: (0, g)),
        pl.BlockSpec((1, qw), lambda b, g, t: (0, g)),
        pl.BlockSpec((1, dv), lambda b, g, t: (0, 0)),
        pl.BlockSpec(tab.shape, lambda b, g, t: (0, 0)),
        pl.BlockSpec(msk.shape, lambda b, g, t: (0, 0, 0)),
    ]
    args = [proj, proj, proj, proj, lr, wg, bg, gn, tab, msk]
    zero_init = s0 is None
    if not zero_init:
        in_specs.append(pl.BlockSpec((1, hg, dk, dv), lambda b, g, t: (b, g, 0, 0)))
        args.append(s0)
    est = (2 * tt * (2 * qw + 2 * vw) * 2 + 2 * tt * LANES * 4 + 2 * tt * vw * 2
           + 4 * hg * dk * dv * 4 + hg * dk * dv * 4 + tt * qw * 4 + 4 * tab.shape[0] * qw * 4)
    kern = functools.partial(_gla_kernel, heads=hg, dk=dk, dv=dv, n_chunks=tt // CHUNK, zero_init=zero_init)
    o, s_out = pl.pallas_call(
        kern,
        grid=(nseq, n_groups, n_tiles),
        in_specs=in_specs,
        out_specs=[pl.BlockSpec((tt, vw), lambda b, g, t: (row(b, g, t), g)),
                   pl.BlockSpec((1, hg, dk, dv), lambda b, g, t: (b, g, 0, 0))],
        out_shape=[jax.ShapeDtypeStruct((rows, n_heads * dv), BF16),
                   jax.ShapeDtypeStruct((nseq, n_heads, dk, dv), F32)],
        scratch_shapes=[pltpu.VMEM((hg, dv, dk), F32), pltpu.VMEM((tt, qw), F32)],
        compiler_params=_params(("parallel", "parallel", "arbitrary"), est),
        name="gla",
    )(*args)
    return o, s_out


def _mem_attn_kernel(q_ref, k_ref, v_ref, o_ref, *, n_heads):
    for h in range(n_heads):
        cs = slice(h * MEM_DH, (h + 1) * MEM_DH)
        s = _nt(q_ref[:, cs], k_ref[0, :, cs].astype(BF16))
        p = jnp.exp(s - jnp.max(s, axis=-1, keepdims=True))
        l = jnp.sum(p, axis=-1, keepdims=True)
        o = _dot(p.astype(BF16), v_ref[0, :, cs].astype(BF16))
        o_ref[:, cs] = (o / l).astype(o_ref.dtype)


def mem_attn(qsrc, q_col_block, mk, mv, *, seq_len):
    rows = qsrc.shape[0]
    nseq, n_mem, mq = mk.shape
    tm = min(1024, seq_len)
    tiles_per_seq = seq_len // tm
    est = 2 * tm * mq * 2 * 2 + 2 * 2 * n_mem * mq * 4 + 6 * tm * n_mem * 4
    return pl.pallas_call(
        functools.partial(_mem_attn_kernel, n_heads=mq // MEM_DH),
        grid=(rows // tm,),
        in_specs=[pl.BlockSpec((tm, mq), lambda i: (i, q_col_block)),
                  pl.BlockSpec((1, n_mem, mq), lambda i: (i // tiles_per_seq, 0, 0)),
                  pl.BlockSpec((1, n_mem, mq), lambda i: (i // tiles_per_seq, 0, 0))],
        out_specs=pl.BlockSpec((tm, mq), lambda i: (i, 0)),
        out_shape=jax.ShapeDtypeStruct((rows, mq), BF16),
        compiler_params=_params(("parallel",), est),
        name="mem_attn",
    )(qsrc, mk, mv)


def _cumsum_kernel(lf_ref, c_ref, ct_ref, crow_ref, ccol_ref):
    j = pl.program_id(1)
    tb = lf_ref.shape[1]

    @pl.when(j == 0)
    def _():
        crow_ref[...] = jnp.zeros_like(crow_ref)
        ccol_ref[...] = jnp.zeros_like(ccol_ref)

    r = lax.broadcasted_iota(jnp.int32, (tb, tb), 0)
    s = lax.broadcasted_iota(jnp.int32, (tb, tb), 1)
    lower = jnp.where(s <= r, 1.0, 0.0).astype(BF16)
    upper = jnp.where(r <= s, 1.0, 0.0).astype(BF16)
    c = crow_ref[0:1, :]
    ct = ccol_ref[:, 0:1]
    for part in _split_bf16(lf_ref[0], 3):
        c = c + _dot(lower, part)
        ct = ct + _tn(part, upper)
    c_ref[0] = c
    ct_ref[0] = ct
    crow_ref[...] = jnp.broadcast_to(c[tb - 1:tb, :], crow_ref.shape)
    ccol_ref[...] = jnp.broadcast_to(ct[:, tb - 1:tb], ccol_ref.shape)


def cumsum_time(lf):
    nseq, L, w = lf.shape
    tb = L if L <= 2048 else 512
    est = 4 * tb * w * 4 + 4 * tb * tb * 4 + 8 * tb * w * 4
    return pl.pallas_call(
        _cumsum_kernel,
        grid=(nseq, L // tb),
        in_specs=[pl.BlockSpec((1, tb, w), lambda b, j: (b, j, 0))],
        out_specs=[pl.BlockSpec((1, tb, w), lambda b, j: (b, j, 0)),
                   pl.BlockSpec((1, w, tb), lambda b, j: (b, 0, j))],
        out_shape=[jax.ShapeDtypeStruct((nseq, L, w), F32), jax.ShapeDtypeStruct((nseq, w, L), F32)],
        scratch_shapes=[pltpu.VMEM((SUBLANES, w), F32), pltpu.VMEM((w, LANES), F32)],
        compiler_params=_params(("parallel", "arbitrary"), est),
        name="cumsum_time",
    )(lf)


def _pick_lane(x, lane):
    ids = lax.broadcasted_iota(jnp.int32, x.shape, 1)
    return jnp.sum(jnp.where(ids == lane, x, 0.0), axis=-1, keepdims=True)


LOG2E = 1.4426950408889634
FOX_Q_BLOCK = 2048
FOX_ROW_CHAIN = 512
FOX_K_BLOCK = 2048
FOX_AUG = 3


def _fox_kernel(q_ref, k_ref, v_ref, c_ref, o_ref, augq_ref, augk_ref, vaug_ref, *chain_refs, tk, tkf):
    h = pl.program_id(1)
    qi = pl.program_id(2)
    seq_len, dh = k_ref.shape
    tq = q_ref.shape[0]
    n_sub = tq // tk
    chains = [chain_refs[3 * i:3 * i + 3] for i in range(n_sub)]

    @pl.when(qi == 0)
    def _():
        def build(r, carry):
            r0 = pl.multiple_of(r * tk, tk)
            cc = _pick_lane(c_ref[0, pl.ds(r0, tk), :], h) * LOG2E
            lane = lax.broadcasted_iota(jnp.int32, (tk, dh), 1)
            aq = jnp.where(lane < 2 * FOX_AUG, 1.0, 0.0)
            ak = aq
            for n, part in enumerate(_split_bf16(cc, FOX_AUG)):
                aq = jnp.where(lane == n, part.astype(F32), aq)
                ak = jnp.where(lane == FOX_AUG + n, -part.astype(F32), ak)
            augq_ref[pl.ds(r0, tk), :] = aq.astype(BF16)
            augk_ref[pl.ds(r0, tk), :] = ak.astype(BF16)
            vaug_ref[pl.ds(r0, tk), 0:dh] = v_ref[pl.ds(r0, tk), :]
            vaug_ref[pl.ds(r0, tk), dh:2 * dh] = jnp.ones((tk, dh), BF16)
            return carry

        lax.fori_loop(0, seq_len // tk, build, 0)

    q0 = pl.multiple_of(qi * tq, tq)
    for sub, (qa_ref, m_ref, acc_ref) in enumerate(chains):
        qa_ref[:, 0:dh] = q_ref[sub * tk:(sub + 1) * tk, :]
        qa_ref[:, dh:2 * dh] = augq_ref[pl.ds(pl.multiple_of(q0 + sub * tk, tk), tk), :]
        m_ref[...] = jnp.full_like(m_ref, -jnp.inf)
        acc_ref[...] = jnp.zeros_like(acc_ref)

    def scores(sub, k0, width):
        ka = jnp.concatenate([k_ref[pl.ds(k0, width), :], augk_ref[pl.ds(k0, width), :]], axis=1)
        return _nt(chains[sub][0][...], ka)

    def softmax_update(sub, s, masked):
        m_ref = chains[sub][1]
        if masked:
            row = lax.broadcasted_iota(jnp.int32, s.shape, 0)
            col = lax.broadcasted_iota(jnp.int32, s.shape, 1)
            s = jnp.where(col <= row, s, -jnp.inf)
        blocks = [s[:, j * LANES:(j + 1) * LANES] for j in range(s.shape[1] // LANES)]
        mx = blocks[0]
        for blk in blocks[1:]:
            mx = jnp.maximum(mx, blk)
        m_old = m_ref[...]
        m_new = jnp.maximum(m_old, jnp.max(mx, axis=-1, keepdims=True))
        m_ref[...] = m_new
        alpha = jnp.exp2(m_old - m_new)
        pr = jnp.concatenate([jnp.exp2(blk - m_new) for blk in blocks], axis=1)
        return alpha, pr.astype(BF16)

    def accumulate(sub, k0, width, alpha, pr):
        acc_ref = chains[sub][2]
        acc_ref[...] = (jnp.concatenate([alpha] * (2 * dh // LANES), axis=1) * acc_ref[...]
                        + _dot(pr, vaug_ref[pl.ds(k0, width), :]))

    def step(jobs, width):
        ss = [scores(sub, k0, width) for sub, k0, _ in jobs]
        ps = [softmax_update(sub, s, masked) for (sub, _, masked), s in zip(jobs, ss)]
        for (sub, k0, _), (alpha, pr) in zip(jobs, ps):
            accumulate(sub, k0, width, alpha, pr)

    def full_chunk(j, carry):
        k0 = pl.multiple_of(j * tkf, tkf)
        step([(sub, k0, False) for sub in range(n_sub)], tkf)
        return carry

    lax.fori_loop(0, qi * (tq // tkf), full_chunk, 0)
    step([(sub, pl.multiple_of(q0 + d * tk, tk), sub == d) for d in range(n_sub) for sub in range(d, n_sub)], tk)
    for sub, (_, _, acc_ref) in enumerate(chains):
        o_ref[sub * tk:(sub + 1) * tk, :] = (acc_ref[:, 0:dh] / acc_ref[:, dh:2 * dh]).astype(o_ref.dtype)


def fox_prompt(qsrc, kbf, vbf, c, *, nseq, seq_len, n_heads, dh):
    rows = qsrc.shape[0]
    tq = min(FOX_Q_BLOCK, seq_len)
    tk = min(FOX_ROW_CHAIN, tq)
    tkf = min(FOX_K_BLOCK, tq)
    nq = seq_len // tq
    est = (2 * tq * dh * 2 * 2 + 2 * 2 * seq_len * dh * 2 + 2 * seq_len * LANES * 4 + 4 * seq_len * dh * 2
           + tq * 2 * dh * 2 + 3 * tq * LANES * 4 + (tq // tk) * 4 * tk * tkf * 4)
    return pl.pallas_call(
        functools.partial(_fox_kernel, tk=tk, tkf=tkf),
        grid=(nseq, n_heads, nq),
        in_specs=[
            pl.BlockSpec((tq, dh), lambda b, h, qi: (b * nq + qi, h)),
            pl.BlockSpec((seq_len, dh), lambda b, h, qi: (b, h)),
            pl.BlockSpec((seq_len, dh), lambda b, h, qi: (b, h)),
            pl.BlockSpec((1, seq_len, LANES), lambda b, h, qi: (b, 0, 0)),
        ],
        out_specs=pl.BlockSpec((tq, dh), lambda b, h, qi: (b * nq + qi, h)),
        out_shape=jax.ShapeDtypeStruct((rows, n_heads * dh), BF16),
        scratch_shapes=[pltpu.VMEM((seq_len, dh), BF16), pltpu.VMEM((seq_len, dh), BF16),
                        pltpu.VMEM((seq_len, 2 * dh), BF16)]
        + [pltpu.VMEM((tk, 2 * dh), BF16), pltpu.VMEM((tk, LANES), F32),
           pltpu.VMEM((tk, 2 * dh), F32)] * (tq // tk),
        compiler_params=_params(("parallel", "parallel", "arbitrary"), est),
        name="fox_prompt",
    )(qsrc, kbf, vbf, c)


FOX_DECODE_KEYS = 512


def _fox_decode_kernel(q_ref, kc_ref, vc_ref, kn_ref, vn_ref, c_ref, ctc_ref, ctn_ref, o_ref,
                       m_ref, l_ref, acc_ref, *, n_heads, dh):
    j = pl.program_id(1)
    ts = q_ref.shape[0]
    pc = kc_ref.shape[1] // n_heads

    def cached(h):
        cs = slice(h * dh, (h + 1) * dh)
        cq = _pick_lane(c_ref[0], h) * LOG2E
        kc = kc_ref[0, pl.ds(h, pc, stride=n_heads), :].astype(BF16)
        vc = vc_ref[0, pl.ds(h, pc, stride=n_heads), :].astype(BF16)
        return cs, cq, _nt(q_ref[:, cs], kc) + cq - ctc_ref[0, h] * LOG2E, vc

    @pl.when(j == 0)
    def _():
        row = lax.broadcasted_iota(jnp.int32, (ts, ts), 0)
        col = lax.broadcasted_iota(jnp.int32, (ts, ts), 1)
        for h in range(n_heads):
            cs, cq, s, vc = cached(h)
            sn = _nt(q_ref[:, cs], kn_ref[:, cs]) + cq - ctn_ref[0, h] * LOG2E
            s = jnp.concatenate([s, jnp.where(col <= row, sn, -jnp.inf)], axis=1)
            m = jnp.max(s, axis=-1, keepdims=True)
            p = jnp.exp2(s - m)
            m_ref[h] = m
            l_ref[h] = jnp.sum(p, axis=-1, keepdims=True)
            acc_ref[h] = _dot(p.astype(BF16), jnp.concatenate([vc, vn_ref[:, cs]], axis=0))

    @pl.when(j > 0)
    def _():
        for h in range(n_heads):
            _, _, s, vc = cached(h)
            m_old = m_ref[h]
            m_new = jnp.maximum(m_old, jnp.max(s, axis=-1, keepdims=True))
            alpha = jnp.exp2(m_old - m_new)
            p = jnp.exp2(s - m_new)
            l_ref[h] = alpha * l_ref[h] + jnp.sum(p, axis=-1, keepdims=True)
            acc_ref[h] = alpha * acc_ref[h] + _dot(p.astype(BF16), vc)
            m_ref[h] = m_new

    @pl.when(j == pl.num_programs(1) - 1)
    def _():
        for h in range(n_heads):
            cs = slice(h * dh, (h + 1) * dh)
            o_ref[:, cs] = (acc_ref[h] / l_ref[h]).astype(o_ref.dtype)


def fox_decode(qsrc, kcache, vcache, kbf, vbf, c, ct4, *, n_heads, dh):
    nseq = kcache.shape[0]
    past = kcache.shape[1] // n_heads
    rows = qsrc.shape[0]
    ts = rows // nseq
    w = n_heads * dh
    pc = _pick_tile(past, FOX_DECODE_KEYS, LANES)
    ct_new = ct4[..., past:]
    est = (2 * 2 * pc * w * 4 + 8 * ts * w * 2 + 2 * ts * LANES * 4 + n_heads * ts * (dh + 2 * LANES) * 4
           + 8 * ts * pc * 4 + 4 * pc * dh * 4)
    return pl.pallas_call(
        functools.partial(_fox_decode_kernel, n_heads=n_heads, dh=dh),
        grid=(nseq, past // pc),
        in_specs=[
            pl.BlockSpec((ts, w), lambda b, j: (b, 0)),
            pl.BlockSpec((1, pc * n_heads, dh), lambda b, j: (b, j, 0)),
            pl.BlockSpec((1, pc * n_heads, dh), lambda b, j: (b, j, 0)),
            pl.BlockSpec((ts, w), lambda b, j: (b, 0)),
            pl.BlockSpec((ts, w), lambda b, j: (b, 0)),
            pl.BlockSpec((1, ts, LANES), lambda b, j: (b, past // ts, 0)),
            pl.BlockSpec((1, n_heads, 1, pc), lambda b, j: (b, 0, 0, j)),
            pl.BlockSpec((1, n_heads, 1, ts), lambda b, j: (b, 0, 0, 0)),
        ],
        out_specs=pl.BlockSpec((ts, w), lambda b, j: (b, 0)),
        out_shape=jax.ShapeDtypeStruct((rows, w), BF16),
        scratch_shapes=[pltpu.VMEM((n_heads, ts, 1), F32), pltpu.VMEM((n_heads, ts, 1), F32),
                        pltpu.VMEM((n_heads, ts, dh), F32)],
        compiler_params=_params(("parallel", "arbitrary"), est),
        name="fox_decode",
    )(qsrc, kcache, vcache, kbf, vbf, c, ct4, ct_new)


def _pad_cols(x, width):
    return jnp.pad(x, ((0, 0), (0, width - x.shape[1])))


def _ffn_block(x, xg, ssq, w_up, w_down, layer, w_conv, b_conv, prev, next_gains, *, nseq, seq_len):
    tf = _pick_tile(w_down.shape[1], 256, LANES)
    h, tails = ffn_up(xg, ssq, w_up, layer, w_conv, b_conv, prev, nseq=nseq, seq_len=seq_len, tf=tf)
    outs = matmul([h], w_down, [F32], tm=512, tn=512, w_layer=layer, res=x, next_gains=next_gains, name="ffn_down")
    return outs, tails


def kernel(x_prompt, x_sample, state_gla, cache_fox_k, cache_fox_v, cache_fox_logf, cache_mem_k, cache_mem_v,
           state_ffn_conv, mem_prompt, g_attn, g_ffn, g_mem, w_mem_kv, w_in_a, w_gate_a, b_gate_a, g_gla,
           w_in_b, g_kv, w_kv, b_forget, w_out, w_up, w_conv, b_conv, w_down, g_final):
    n_p, t_p, d = x_prompt.shape
    n_s, t_s, _ = x_sample.shape
    depth = g_attn.shape[0]
    n_a = state_gla.shape[0]
    _, _, gla_heads, dk, dv = state_gla.shape
    _, past, fox_heads, dh = cache_fox_k.shape
    n_mem, mq = cache_mem_k.shape[2:]
    a_q, a_v, b_q = gla_heads * dk, gla_heads * dv, fox_heads * dh

    def mixer_gains(l):
        return jnp.stack([g_attn[l]] + ([g_kv] if l == n_a else []))

    streams = []
    for x3, nseq, seq_len in ((x_prompt, n_p, t_p), (x_sample, n_s, t_s)):
        x = x3.reshape(nseq * seq_len, d)
        streams.append(dict(x=x, nseq=nseq, seq_len=seq_len, normed=rmsnorm(x, mixer_gains(0), BF16), ssq=None))
    mem_rows = mem_prompt.reshape(n_p * n_mem, d)
    mk_out, mv_out, conv_out = [], [], [[], []]
    gla_out = [[], []]
    fox = [None, None]

    w_mem_bf, w_in_a_bf, w_in_b_bf, w_kv_bf = (w.astype(BF16) for w in (w_mem_kv, w_in_a, w_in_b, w_kv))
    w_out_bf, w_up_bf, w_down_bf = (w.astype(BF16) for w in (w_out, w_up, w_down))
    mem_scale = jnp.full((1, mq), MEM_DH ** -0.5, F32)
    main = 2 * a_q + 2 * a_v

    for l in range(depth):
        (mem_n,) = rmsnorm(mem_rows, g_mem[l][None], BF16)
        (mkv,) = matmul([mem_n], w_mem_bf, [F32], tm=512, tn=1024, w_layer=l, name="mem_kv")
        mk_p = mkv[:, :mq].reshape(n_p, n_mem, mq)
        mv_p = mkv[:, mq:].reshape(n_p, n_mem, mq)
        mk_out.append(mk_p)
        mv_out.append(mv_p)
        mem_kv = [(mk_p, mv_p), (cache_mem_k[l], cache_mem_v[l])]

        if l < n_a:
            w_lr = _pad_cols(w_in_a[l][:, main:main + GLA_RANK], LANES).astype(BF16)
            w_qm = w_in_a[l][:, main + GLA_RANK:].astype(BF16)
            col_scale = jnp.concatenate([jnp.full((a_q,), dk ** -0.5, F32), jnp.ones((a_q + 2 * a_v,), F32)])[None]
            wg = jnp.pad(w_gate_a[l], ((0, LANES - GLA_RANK), (0, 0))).astype(BF16)
            bg = b_gate_a[l][None]
            gn = g_gla[l][None]
        else:
            j = l - n_a
            col_scale = jnp.concatenate([jnp.full((b_q,), LOG2E * dh ** -0.5, F32), mem_scale[0]])[None]
            if l == n_a:
                w_kv_f = _pad_cols(w_kv[:, 2 * b_q:], LANES).astype(BF16)
                b_f = _pad_cols(b_forget[None], LANES)
        ffn_next_gains = mixer_gains(l + 1) if l + 1 < depth else None

        for si, st in enumerate(streams):
            x, nseq, seq_len = st["x"], st["nseq"], st["seq_len"]
            mk, mv = mem_kv[si]
            xn = st["normed"][0]
            norm = dict(row_ssq=st["ssq"], norm_dim=d)
            if l < n_a:
                (proj,) = matmul([xn], w_in_a_bf, [BF16], tm=1024, tn=1024, w_layer=l, n=main, scale=col_scale,
                                 name="in_proj_a", **norm)
                (qm,) = matmul([xn], w_qm, [BF16], tm=1024, tn=1024, scale=mem_scale, name="in_proj_a_mem", **norm)
                (lr,) = matmul([xn], w_lr, [F32], tm=1024, tn=LANES, name="in_proj_a_gate", **norm)
                s0 = None if si == 0 else state_gla[l]
                o, s_new = gla(proj, lr, wg, bg, gn, s0, nseq=nseq, seq_len=seq_len,
                               n_heads=gla_heads, dk=dk, dv=dv)
                gla_out[si].append(s_new)
                mo = mem_attn(qm, 0, mk, mv, seq_len=seq_len)
            else:
                if l == n_a:
                    xkv = st["normed"][1]
                    k32, kbf = matmul([xkv], w_kv_bf, [F32, BF16], tm=1024, tn=1024, n=b_q, head_dims=[dh, None],
                                      name="k_proj", **norm)
                    v32, vbf = matmul([xkv], w_kv_bf, [F32, BF16], tm=1024, tn=1024, w_col0=b_q, n=b_q,
                                      head_dims=[dh, None], name="v_proj", **norm)
                    (lf,) = matmul([xkv], w_kv_f, [F32], tm=1024, tn=LANES, bias=b_f, act="log_sigmoid",
                                   name="kv_proj_forget", **norm)
                    lf3 = lf.reshape(nseq, seq_len, LANES)
                    if si == 0:
                        lf_all = lf3
                    else:
                        lf_all = jnp.concatenate(
                            [jnp.pad(cache_fox_logf, ((0, 0), (0, 0), (0, LANES - fox_heads))), lf3], axis=1)
                    c, ct = cumsum_time(lf_all)
                    ct4 = ct[:, :fox_heads].reshape(nseq, fox_heads, 1, ct.shape[-1])
                    fox[si] = dict(k32=k32, v32=v32, kbf=kbf, vbf=vbf, lf=lf3[:, :, :fox_heads], c=c, ct4=ct4)
                fx = fox[si]
                (proj,) = matmul([xn], w_in_b_bf, [BF16], tm=1024, tn=1024, w_layer=j, scale=col_scale,
                                 name="in_proj_b", **norm)
                kb, vb = fx["kbf"], fx["vbf"]
                if si == 0:
                    o = fox_prompt(proj, kb, vb, fx["c"], nseq=nseq, seq_len=seq_len,
                                   n_heads=fox_heads, dh=dh)
                else:
                    o = fox_decode(proj, cache_fox_k.reshape(nseq, past * fox_heads, dh),
                                   cache_fox_v.reshape(nseq, past * fox_heads, dh),
                                   kb, vb, fx["c"], fx["ct4"], n_heads=fox_heads, dh=dh)
                mo = mem_attn(proj, b_q // mq, mk, mv, seq_len=seq_len)
            x, xg, ssq = matmul([o, mo], w_out_bf, [F32], tm=1024, tn=512, w_layer=l, res=x,
                                next_gains=g_ffn[l][None], name="out_proj")
            prev = None if si == 0 else state_ffn_conv[l]
            outs, tails = _ffn_block(x, xg, ssq, w_up_bf, w_down_bf, l, w_conv[l], b_conv[l][None], prev,
                                     ffn_next_gains, nseq=nseq, seq_len=seq_len)
            conv_out[si].append(tails)
            st["x"] = outs[0]
            if ffn_next_gains is not None:
                st["normed"], st["ssq"] = outs[1:-1], outs[-1]

    ys = []
    for st in streams:
        (y,) = rmsnorm(st["x"], g_final[None], F32)
        ys.append(y)
    fp, fs = fox
    return (ys[0].reshape(n_p, t_p, d), ys[1].reshape(n_s, t_s, d),
            jnp.stack(gla_out[0]), jnp.stack(gla_out[1]),
            fp["k32"].reshape(n_p, t_p, fox_heads, dh), fp["v32"].reshape(n_p, t_p, fox_heads, dh), fp["lf"],
            fs["k32"].reshape(n_s, t_s, fox_heads, dh), fs["v32"].reshape(n_s, t_s, fox_heads, dh), fs["lf"],
            jnp.stack(mk_out), jnp.stack(mv_out),
            jnp.stack(conv_out[0]), jnp.stack(conv_out[1]))
```
